```python
import jax, jax.numpy as jnp
from jax import lax
import numpy as np

D_MODEL = 2048
BATCH = 8
SEQ = 4096
DEPTH = 4

GRID_W = 64
CTX_LEN = 256
Q_BLOCK = 128
ROPE_THETA = 10000.0
EPS = 1e-6
A_HEADS = 8
A_KV_HEADS = 2
A_GROUP = A_HEADS // A_KV_HEADS
A_HEAD_DIM = 128
B_HEADS = 8
B_Q_RANK = 512
B_KV_RANK = 256
B_NOPE_DIM = 128
B_ROPE_DIM = 64
B_V_DIM = 128
A_SCALE = A_HEAD_DIM ** -0.5
B_SCALE = (B_NOPE_DIM + B_ROPE_DIM) ** -0.5
ATTN_SIZES = (A_HEADS * A_HEAD_DIM, A_KV_HEADS * A_HEAD_DIM, A_KV_HEADS * A_HEAD_DIM,
              B_Q_RANK, B_KV_RANK, B_ROPE_DIM)
ATTN_IN = sum(ATTN_SIZES)
ATTN_OUT = A_HEADS * A_HEAD_DIM + B_HEADS * B_V_DIM
SC_WIDTH = D_MODEL
CONV_W = 3
D_FF = 256 * ((8 * D_MODEL // 3 + 255) // 256)
N_ATTN_LAYERS = (DEPTH + 1) // 2
N_CONV_LAYERS = DEPTH // 2

kernel_name = "hybrid_dit_gqa_mla_shortconv_convffn"


def rms_norm(x, w):
    xf = x.astype(jnp.float32)
    y = xf * lax.rsqrt(jnp.mean(xf * xf, axis=-1, keepdims=True) + EPS)
    return (y * w.astype(jnp.float32)).astype(x.dtype)


def adaln(cvec, w, b):
    m = jax.nn.silu(cvec) @ w + b
    if m.ndim == 2:
        m = m[:, None, :]
    return jnp.split(m, 6, axis=-1)


def modulate(h, shift, scale):
    return h * (1.0 + scale) + shift


def axial_rope_tables(rows, rot_dim, dtype):
    r = jnp.repeat(jnp.arange(rows, dtype=jnp.float32), GRID_W)
    col = jnp.tile(jnp.arange(GRID_W, dtype=jnp.float32), rows)
    quarter = rot_dim // 4
    inv = ROPE_THETA ** (-jnp.arange(quarter, dtype=jnp.float32) / quarter)
    ar = r[:, None] * inv
    ac = col[:, None] * inv
    ang = jnp.concatenate([ar, ar, ac, ac], axis=-1)
    return jnp.cos(ang).astype(dtype), jnp.sin(ang).astype(dtype)


def apply_rope(x, cos, sin):
    shape = (cos.shape[0],) + (1,) * (x.ndim - 3) + (cos.shape[-1],)
    cos = cos.reshape(shape)
    sin = sin.reshape(shape)
    a, b, c, d = jnp.split(x, 4, axis=-1)
    rot = jnp.concatenate([-b, a, -d, c], axis=-1)
    return x * cos + rot * sin


def dwconv(x, w):
    s = x.shape[1]
    pad = CONV_W // 2
    xp = jnp.pad(x, ((0, 0), (pad, CONV_W - 1 - pad), (0, 0)))
    out = xp[:, 0:s] * w[0]
    for j in range(1, CONV_W):
        out = out + xp[:, j:j + s] * w[j]
    return out


def blocked_attention(q, k, v, scale):
    bn, s, kh, g, dq = q.shape
    nb = s // Q_BLOCK
    qb = q.reshape(bn, nb, Q_BLOCK, kh, g, dq).transpose(1, 0, 2, 3, 4, 5)

    def one_block(qblk):
        sc = jnp.einsum('bqhgd,bthd->bhgqt', qblk, k,
                        preferred_element_type=jnp.float32) * scale
        p = jax.nn.softmax(sc, axis=-1).astype(v.dtype)
        return jnp.einsum('bhgqt,bthd->bqhgd', p, v)

    o = lax.map(one_block, qb)
    return o.transpose(1, 0, 2, 3, 4, 5).reshape(bn, s, kh * g * v.shape[-1])


def attn_project(h, w_in, q_norm_a, k_norm_a, q_norm_b, kv_norm_b, w_uq, w_ukv,
                 rope_a, rope_b, with_queries):
    bn, s, _ = h.shape
    proj = h @ w_in
    idx, acc = [], 0
    for sz in ATTN_SIZES[:-1]:
        acc += sz
        idx.append(acc)
    qa, ka, va, cq, ckv, kr = jnp.split(proj, idx, axis=-1)
    ka = rms_norm(ka.reshape(bn, s, A_KV_HEADS, A_HEAD_DIM), k_norm_a)
    va = va.reshape(bn, s, A_KV_HEADS, A_HEAD_DIM)
    kv = (rms_norm(ckv, kv_norm_b) @ w_ukv).reshape(bn, s, B_HEADS, B_NOPE_DIM + B_V_DIM)
    k_nope, vb = jnp.split(kv, [B_NOPE_DIM], axis=-1)
    if rope_a is not None:
        ka = apply_rope(ka, *rope_a)
        kr = apply_rope(kr, *rope_b)
    kb = jnp.concatenate(
        [k_nope, jnp.broadcast_to(kr[:, :, None, :], (bn, s, B_HEADS, B_ROPE_DIM))], axis=-1)
    if not with_queries:
        return None, ka, va, None, kb, vb
    qa = rms_norm(qa.reshape(bn, s, A_KV_HEADS, A_GROUP, A_HEAD_DIM), q_norm_a)
    qb = (rms_norm(cq, q_norm_b) @ w_uq).reshape(bn, s, B_HEADS, B_NOPE_DIM + B_ROPE_DIM)
    q_nope, q_rope = jnp.split(qb, [B_NOPE_DIM], axis=-1)
    if rope_a is not None:
        qa = apply_rope(qa, *rope_a)
        q_rope = apply_rope(q_rope, *rope_b)
    qb = jnp.concatenate([q_nope, q_rope], axis=-1)[:, :, :, None, :]
    return qa, ka, va, qb, kb, vb


def short_conv_mixer(h, w_in, conv_w, w_out):
    b, cg, v = jnp.split(h @ w_in, 3, axis=-1)
    return (b * dwconv(cg * v, conv_w)) @ w_out


def conv_ffn(h, w_up, conv_w, w_down):
    g, u = jnp.split(dwconv(h @ w_up, conv_w), 2, axis=-1)
    return (jax.nn.silu(g) * u) @ w_down


def setup_inputs(seed: int = 0) -> dict:
    key = jax.random.key(seed)
    ks = jax.random.split(key, 23)
    f32 = jnp.float32

    def nrm(k, shape, scale):
        return jax.random.normal(k, shape, f32) * scale

    def gain(k, shape):
        return 1.0 + 0.02 * jax.random.normal(k, shape, f32)

    D = D_MODEL
    return {
        "x": nrm(ks[0], (BATCH, SEQ, D), 1.0),
        "c": nrm(ks[1], (BATCH, D), 1.0),
        "ctx": nrm(ks[2], (BATCH, CTX_LEN, D), 1.0),
        "c_ctx": nrm(ks[3], (D,), 1.0),
        "w_ada": nrm(ks[4], (DEPTH, D, 6 * D), 0.5 * D ** -0.5),
        "b_ada": nrm(ks[5], (DEPTH, 6 * D), 0.02),
        "norm_mix": gain(ks[6], (DEPTH, D)),
        "norm_ffn": gain(ks[7], (DEPTH, D)),
        "attn_w_in": nrm(ks[8], (N_ATTN_LAYERS, D, ATTN_IN), D ** -0.5),
        "attn_q_norm": gain(ks[9], (N_ATTN_LAYERS, A_HEAD_DIM)),
        "attn_k_norm": gain(ks[10], (N_ATTN_LAYERS, A_HEAD_DIM)),
        "mla_q_norm": gain(ks[11], (N_ATTN_LAYERS, B_Q_RANK)),
        "mla_kv_norm": gain(ks[12], (N_ATTN_LAYERS, B_KV_RANK)),
        "mla_w_uq": nrm(ks[13], (N_ATTN_LAYERS, B_Q_RANK, B_HEADS * (B_NOPE_DIM + B_ROPE_DIM)),
                        B_Q_RANK ** -0.5),
        "mla_w_ukv": nrm(ks[14], (N_ATTN_LAYERS, B_KV_RANK, B_HEADS * (B_NOPE_DIM + B_V_DIM)),
                         B_KV_RANK ** -0.5),
        "attn_w_o": nrm(ks[15], (N_ATTN_LAYERS, ATTN_OUT, D), ATTN_OUT ** -0.5),
        "sc_w_in": nrm(ks[16], (N_CONV_LAYERS, D, 3 * SC_WIDTH), D ** -0.5),
        "sc_conv": nrm(ks[17], (N_CONV_LAYERS, CONV_W, SC_WIDTH), CONV_W ** -0.5),
        "sc_w_out": nrm(ks[18], (N_CONV_LAYERS, SC_WIDTH, D), SC_WIDTH ** -0.5),
        "ffn_w_up": nrm(ks[19], (DEPTH, D, 2 * D_FF), D ** -0.5),
        "ffn_conv": nrm(ks[20], (DEPTH, CONV_W, 2 * D_FF), CONV_W ** -0.5),
        "ffn_w_down": nrm(ks[21], (DEPTH, D_FF, D), D_FF ** -0.5),
        "final_norm": gain(ks[22], (D,)),
    }


def reference(x, c, ctx, c_ctx, w_ada, b_ada, norm_mix, norm_ffn, attn_w_in, attn_q_norm,
              attn_k_norm, mla_q_norm, mla_kv_norm, mla_w_uq, mla_w_ukv, attn_w_o,
              sc_w_in, sc_conv, sc_w_out, ffn_w_up, ffn_conv, ffn_w_down, final_norm):
    ROWS = x.shape[1] // GRID_W
    rope_a = axial_rope_tables(ROWS, A_HEAD_DIM, x.dtype)
    rope_b = axial_rope_tables(ROWS, B_ROPE_DIM, x.dtype)
    xc = ctx
    for l in range(DEPTH):
        later_attn = any(j % 2 == 0 for j in range(l + 1, DEPTH))
        is_attn = (l % 2 == 0)
        sh, sc, g, shf, scf, gf = adaln(c, w_ada[l], b_ada[l])
        h = modulate(rms_norm(x, norm_mix[l]), sh, sc)
        if is_attn or later_attn:
            csh, csc, cg, cshf, cscf, cgf = adaln(c_ctx, w_ada[l], b_ada[l])
            hc = modulate(rms_norm(xc, norm_mix[l]), csh, csc)
        if is_attn:
            i = l // 2
            prm = (attn_w_in[i], attn_q_norm[i], attn_k_norm[i], mla_q_norm[i], mla_kv_norm[i],
                   mla_w_uq[i], mla_w_ukv[i])
            qa_c, ka_c, va_c, qb_c, kb_c, vb_c = attn_project(hc, *prm, None, None, later_attn)
            qa_l, ka_l, va_l, qb_l, kb_l, vb_l = attn_project(h, *prm, rope_a, rope_b, True)
            ka_all = jnp.concatenate([ka_c, ka_l], axis=1)
            va_all = jnp.concatenate([va_c, va_l], axis=1)
            kb_all = jnp.concatenate([kb_c, kb_l], axis=1)
            vb_all = jnp.concatenate([vb_c, vb_l], axis=1)
            y = jnp.concatenate([blocked_attention(qa_l, ka_all, va_all, A_SCALE),
                                 blocked_attention(qb_l, kb_all, vb_all, B_SCALE)],
                                axis=-1) @ attn_w_o[i]
            if later_attn:
                yc = jnp.concatenate([blocked_attention(qa_c, ka_c, va_c, A_SCALE),
                                      blocked_attention(qb_c, kb_c, vb_c, B_SCALE)],
                                     axis=-1) @ attn_w_o[i]
        else:
            i = l // 2
            y = short_conv_mixer(h, sc_w_in[i], sc_conv[i], sc_w_out[i])
            if later_attn:
                yc = short_conv_mixer(hc, sc_w_in[i], sc_conv[i], sc_w_out[i])
        x = x + g * y
        hf = modulate(rms_norm(x, norm_ffn[l]), shf, scf)
        x = x + gf * conv_ffn(hf, ffn_w_up[l], ffn_conv[l], ffn_w_down[l])
        if later_attn:
            xc = xc + cg * yc
            hcf = modulate(rms_norm(xc, norm_ffn[l]), cshf, cscf)
            xc = xc + cgf * conv_ffn(hcf, ffn_w_up[l], ffn_conv[l], ffn_w_down[l])
    return rms_norm(x, final_norm)
```

```python
import functools

import jax
import jax.numpy as jnp
from jax import lax
from jax.experimental import pallas as pl
from jax.experimental.pallas import tpu as pltpu

F32 = jnp.float32
BF16 = jnp.bfloat16

D_MODEL = 2048
DEPTH = 4
GRID_W = 64
ROPE_THETA = 10000.0
EPS = 1e-6
A_HEADS = 8
A_KV_HEADS = 2
A_GROUP = A_HEADS // A_KV_HEADS
A_HEAD_DIM = 128
B_HEADS = 8
B_Q_RANK = 512
B_KV_RANK = 256
B_NOPE_DIM = 128
B_ROPE_DIM = 64
B_V_DIM = 128
B_QK_PAD = 256
A_SCALE = A_HEAD_DIM ** -0.5
B_SCALE = (B_NOPE_DIM + B_ROPE_DIM) ** -0.5
SC_WIDTH = D_MODEL
D_FF = 256 * ((8 * D_MODEL // 3 + 255) // 256)
W_IN_PAD = 2432
MOD_ROWS = 16
HALO = 16
VMEM_LIMIT = 56 * 1024 * 1024


def _cparams(sem):
    return pltpu.CompilerParams(dimension_semantics=sem, vmem_limit_bytes=VMEM_LIMIT)


def _const_spec(shape):
    nd = len(shape)
    return pl.BlockSpec(shape, lambda *g: (0,) * nd, pipeline_mode=pl.Buffered(1))


def _mod_spec(layer, part, ctx):
    if ctx:
        idx = lambda b, *g: (layer, MOD_ROWS // 2, part, 0, 0)
    else:
        idx = lambda b, *g: (layer, b, part, 0, 0)
    return pl.BlockSpec((None, None, None, 1, D_MODEL), idx)


def _rms(x):
    return x * lax.rsqrt(jnp.mean(x * x, axis=-1, keepdims=True) + EPS)


def _norm_mod(x, nw, sc, sh):
    return (_rms(x) * nw) * (1.0 + sc) + sh


def _rope(x, cos, sin_up, sin_dn, quarter):
    n = x.shape[-1]
    return x * cos + pltpu.roll(x, n - quarter, 1) * sin_up + pltpu.roll(x, quarter, 1) * sin_dn


def _adaln_body(c_ref, w_ref, b_ref, o_ref):
    c = c_ref[...]
    s = (c / (1.0 + jnp.exp(-c))).astype(BF16)
    o_ref[...] = jnp.dot(s, w_ref[...].astype(BF16), preferred_element_type=F32) + b_ref[...]


def _adaln(cvec, w_ada, b_ada):
    tn = 1024
    n = 6 * D_MODEL
    out = pl.pallas_call(
        _adaln_body,
        grid=(DEPTH, n // tn),
        in_specs=[
            pl.BlockSpec((MOD_ROWS, D_MODEL), lambda l, j: (0, 0)),
            pl.BlockSpec((None, D_MODEL, tn), lambda l, j: (l, 0, j)),
            pl.BlockSpec((None, 1, tn), lambda l, j: (l, 0, j)),
        ],
        out_specs=pl.BlockSpec((None, MOD_ROWS, tn), lambda l, j: (l, 0, j)),
        out_shape=jax.ShapeDtypeStruct((DEPTH, MOD_ROWS, n), F32),
        compiler_params=_cparams(("parallel", "parallel")),
        name="adaln",
    )(cvec, w_ada, b_ada.reshape(DEPTH, 1, n))
    return out.reshape(DEPTH, MOD_ROWS, 6, 1, D_MODEL)


def _attn_proj_body(*refs, use_rope, with_q):
    it = iter(refs)
    x_ref, sh_ref, sc_ref, nw_ref, win_ref = (next(it) for _ in range(5))
    qn_ref, kn_ref, mqn_ref, mkvn_ref, wuq_ref, wukv_ref = (next(it) for _ in range(6))
    if use_rope:
        ca, sau, sad, cb, sbu, sbd = (next(it)[...] for _ in range(6))
    if with_q:
        qa_ref, qb_ref = next(it), next(it)
    ka_ref, va_ref, kb_ref, vb_ref = (next(it) for _ in range(4))

    def rope_a(v):
        return _rope(v, ca, sau, sad, A_HEAD_DIM // 4) if use_rope else v

    def rope_b(v):
        return _rope(v, cb, sbu, sbd, B_ROPE_DIM // 4) if use_rope else v

    h = _norm_mod(x_ref[...], nw_ref[...], sc_ref[...], sh_ref[...]).astype(BF16)
    proj = jnp.dot(h, win_ref[...], preferred_element_type=F32)
    q_end = A_HEADS * A_HEAD_DIM
    k_end = q_end + A_KV_HEADS * A_HEAD_DIM
    v_end = k_end + A_KV_HEADS * A_HEAD_DIM
    cq_end = v_end + B_Q_RANK
    ckv_end = cq_end + B_KV_RANK

    kn = kn_ref[...]
    for kh in range(A_KV_HEADS):
        lo = q_end + kh * A_HEAD_DIM
        k = rope_a(_rms(proj[:, lo:lo + A_HEAD_DIM]) * kn)
        ka_ref[:, kh * A_HEAD_DIM:(kh + 1) * A_HEAD_DIM] = k.astype(BF16)
    va_ref[...] = proj[:, k_end:v_end].astype(BF16)

    ckv = (_rms(proj[:, cq_end:ckv_end]) * mkvn_ref[...]).astype(BF16)
    kv = jnp.dot(ckv, wukv_ref[...], preferred_element_type=F32)
    nope_w = B_HEADS * B_NOPE_DIM
    vb_ref[...] = kv[:, nope_w:].astype(BF16)
    kr = rope_b(proj[:, ckv_end:W_IN_PAD]).astype(BF16)
    for hh in range(B_HEADS):
        lo = hh * B_QK_PAD
        kb_ref[:, lo:lo + B_NOPE_DIM] = kv[:, hh * B_NOPE_DIM:(hh + 1) * B_NOPE_DIM].astype(BF16)
        kb_ref[:, lo + B_NOPE_DIM:lo + B_QK_PAD] = kr

    if with_q:
        qn = qn_ref[...]
        for hh in range(A_HEADS):
            lo = hh * A_HEAD_DIM
            q = rope_a(_rms(proj[:, lo:lo + A_HEAD_DIM]) * qn) * A_SCALE
            qa_ref[:, lo:lo + A_HEAD_DIM] = q.astype(BF16)
        cq = (_rms(proj[:, v_end:cq_end]) * mqn_ref[...]).astype(BF16)
        qb = jnp.dot(cq, wuq_ref[...], preferred_element_type=F32)
        for hh in range(B_HEADS):
            lo = hh * B_QK_PAD
            qb_ref[:, lo:lo + B_NOPE_DIM] = (qb[:, hh * B_NOPE_DIM:(hh + 1) * B_NOPE_DIM] * B_SCALE).astype(BF16)
            r = rope_b(qb[:, nope_w + hh * 128:nope_w + (hh + 1) * 128]) * B_SCALE
            qb_ref[:, lo + B_NOPE_DIM:lo + B_QK_PAD] = r.astype(BF16)


def _attn_proj(x, mod, layer, ctx, nw, wts, rope, with_q, tm):
    bn, s, _ = x.shape
    use_rope = rope is not None
    w_in, qn, kn, mqn, mkvn, w_uq, w_ukv = wts
    row = lambda w: pl.BlockSpec((None, tm, w), lambda b, i: (b, i, 0))
    in_specs = [row(D_MODEL), _mod_spec(layer, 0, ctx), _mod_spec(layer, 1, ctx),
                _const_spec((1, D_MODEL)), _const_spec(w_in.shape),
                _const_spec(qn.shape), _const_spec(kn.shape), _const_spec(mqn.shape),
                _const_spec(mkvn.shape), _const_spec(w_uq.shape), _const_spec(w_ukv.shape)]
    args = [x, mod, mod, nw, w_in, qn, kn, mqn, mkvn, w_uq, w_ukv]
    if use_rope:
        in_specs += [pl.BlockSpec((tm, 128), lambda b, i: (i, 0))] * 6
        args += list(rope)
    widths = ([A_HEADS * A_HEAD_DIM, B_HEADS * B_QK_PAD] if with_q else []) + [
        A_KV_HEADS * A_HEAD_DIM, A_KV_HEADS * A_HEAD_DIM, B_HEADS * B_QK_PAD, B_HEADS * B_V_DIM]
    outs = pl.pallas_call(
        functools.partial(_attn_proj_body, use_rope=use_rope, with_q=with_q),
        grid=(bn, s // tm),
        in_specs=in_specs,
        out_specs=[row(w) for w in widths],
        out_shape=[jax.ShapeDtypeStruct((bn, s, w), BF16) for w in widths],
        compiler_params=_cparams(("parallel", "parallel")),
        name="attn_proj_ctx" if ctx else "attn_proj",
    )(*args)
    return outs if with_q else [None, None] + list(outs)


def _flash_body(*refs, group, dq, dv, tk, n_chunks):
    if n_chunks:
        q_ref, kc_ref, vc_ref, k_ref, v_ref, o_ref, q_scr, m_scr, l_scr, acc_scr = refs
    else:
        q_ref, kc_ref, vc_ref, o_ref, q_scr = refs
    tq = q_ref.shape[0]
    for g in range(group):
        q_scr[g * tq:(g + 1) * tq, :] = q_ref[:, g * dq:(g + 1) * dq]

    def scores(k):
        return lax.dot_general(q_scr[...], k, (((1,), (1,)), ((), ())), preferred_element_type=F32)

    s = scores(kc_ref[...])
    m = jnp.max(s, axis=-1, keepdims=True)
    p = jnp.exp(s - m)
    l = jnp.sum(p, axis=-1, keepdims=True)
    acc = jnp.dot(p.astype(BF16), vc_ref[...], preferred_element_type=F32)
    if n_chunks:
        m_scr[...] = m
        l_scr[...] = l
        acc_scr[...] = acc

        def step(c, carry):
            off = pl.multiple_of(c * tk, tk)
            s = scores(k_ref[pl.ds(off, tk), :])
            m_old = m_scr[...]
            m_new = jnp.maximum(m_old, jnp.max(s, axis=-1, keepdims=True))
            alpha = jnp.exp(m_old - m_new)
            p = jnp.exp(s - m_new)
            l_scr[...] = alpha * l_scr[...] + jnp.sum(p, axis=-1, keepdims=True)
            pv = jnp.dot(p.astype(BF16), v_ref[pl.ds(off, tk), :], preferred_element_type=F32)
            acc_scr[...] = alpha * acc_scr[...] + pv
            m_scr[...] = m_new
            return carry

        lax.fori_loop(0, n_chunks, step, 0)
        acc = acc_scr[...]
        l = l_scr[...]
    out = acc / l
    for g in range(group):
        o_ref[:, g * dv:(g + 1) * dv] = out[g * tq:(g + 1) * tq, :].astype(o_ref.dtype)


def _flash(q, kc, vc, k, v, *, group, dq, dv, tq, tk, name):
    bn, s, qw = q.shape
    kvh = qw // (group * dq)
    c = kc.shape[1]
    n_chunks = 0 if k is None else k.shape[1] // tk
    in_specs = [pl.BlockSpec((None, tq, group * dq), lambda b, h, i: (b, i, h)),
                pl.BlockSpec((None, c, dq), lambda b, h, i: (b, 0, h)),
                pl.BlockSpec((None, c, dv), lambda b, h, i: (b, 0, h))]
    args = [q, kc, vc]
    m_rows = group * tq
    scratch = [pltpu.VMEM((m_rows, dq), BF16)]
    if n_chunks:
        t = k.shape[1]
        in_specs += [pl.BlockSpec((None, t, dq), lambda b, h, i: (b, 0, h)),
                     pl.BlockSpec((None, t, dv), lambda b, h, i: (b, 0, h))]
        args += [k, v]
        scratch += [pltpu.VMEM((m_rows, 1), F32), pltpu.VMEM((m_rows, 1), F32),
                    pltpu.VMEM((m_rows, dv), F32)]
    return pl.pallas_call(
        functools.partial(_flash_body, group=group, dq=dq, dv=dv, tk=tk, n_chunks=n_chunks),
        grid=(bn, kvh, s // tq),
        in_specs=in_specs,
        out_specs=pl.BlockSpec((None, tq, group * dv), lambda b, h, i: (b, i, h)),
        out_shape=jax.ShapeDtypeStruct((bn, s, kvh * group * dv), BF16),
        scratch_shapes=scratch,
        compiler_params=_cparams(("parallel", "parallel", "parallel")),
        name=name,
    )(*args)


def _proj_res_body(*refs, n_lhs, final):
    x_ref, g_ref = refs[0], refs[1]
    lhs = refs[2:2 + n_lhs]
    ws = refs[2 + n_lhs:2 + 2 * n_lhs]
    rest = refs[2 + 2 * n_lhs:]
    y = jnp.dot(lhs[0][...], ws[0][...], preferred_element_type=F32)
    for a, w in zip(lhs[1:], ws[1:]):
        y = y + jnp.dot(a[...], w[...], preferred_element_type=F32)
    out = x_ref[...] + g_ref[...] * y
    if final:
        fw_ref, o_ref = rest
        out = _rms(out) * fw_ref[...]
    else:
        (o_ref,) = rest
    o_ref[...] = out


def _proj_res(x, mod, layer, part, ctx, lhs, ws, tm, final_w=None, name="proj_res"):
    bn, s, _ = x.shape
    row = lambda w: pl.BlockSpec((None, tm, w), lambda b, i: (b, i, 0))
    in_specs = [row(D_MODEL), _mod_spec(layer, part, ctx)]
    in_specs += [row(a.shape[-1]) for a in lhs] + [_const_spec(w.shape) for w in ws]
    args = [x, mod] + list(lhs) + list(ws)
    if final_w is not None:
        in_specs.append(_const_spec((1, D_MODEL)))
        args.append(final_w)
    return pl.pallas_call(
        functools.partial(_proj_res_body, n_lhs=len(lhs), final=final_w is not None),
        grid=(bn, s // tm),
        in_specs=in_specs,
        out_specs=row(D_MODEL),
        out_shape=jax.ShapeDtypeStruct(x.shape, F32),
        compiler_params=_cparams(("parallel", "parallel")),
        name=name + ("_ctx" if ctx else ""),
    )(*args)


def _up_body(*refs, mode, tm):
    x_ref, xp_ref, xn_ref, sh_ref, sc_ref, nw_ref = refs[:6]
    i = pl.program_id(1)
    j = pl.program_id(2)

    if mode == "ffn":
        wg_ref, wu_ref, cg_ref, cu_ref, o_ref, h_scr, g_scr, u_scr = refs[6:]
    else:
        wb_ref, wc_ref, wv_ref, cw_ref, o_ref, h_scr, g_scr = refs[6:]

    @pl.when(j == 0)
    def _():
        nw, sc, sh = nw_ref[...], sc_ref[...], sh_ref[...]
        rows = min(tm, 256)
        for r in range(0, tm, rows):
            h_scr[HALO + r:HALO + r + rows, :] = _norm_mod(x_ref[r:r + rows, :], nw, sc, sh).astype(BF16)
        hp = _norm_mod(xp_ref[...], nw, sc, sh)
        h_scr[0:HALO, :] = jnp.where(i == 0, 0.0, hp).astype(BF16)
        hn = _norm_mod(xn_ref[...], nw, sc, sh)
        h_scr[HALO + tm:2 * HALO + tm, :] = jnp.where(i == pl.num_programs(1) - 1, 0.0, hn).astype(BF16)

    def conv(scr, cw_ref):
        return (cw_ref[0:1, :] * scr[HALO - 1:HALO - 1 + tm, :]
                + cw_ref[1:2, :] * scr[HALO:HALO + tm, :]
                + cw_ref[2:3, :] * scr[HALO + 1:HALO + 1 + tm, :])

    h = h_scr[...]
    if mode == "ffn":
        g_scr[...] = jnp.dot(h, wg_ref[...], preferred_element_type=F32)
        u_scr[...] = jnp.dot(h, wu_ref[...], preferred_element_type=F32)
        g = conv(g_scr, cg_ref)
        u = conv(u_scr, cu_ref)
        o_ref[...] = ((g / (1.0 + jnp.exp(-g))) * u).astype(o_ref.dtype)
    else:
        g_scr[...] = (jnp.dot(h, wc_ref[...], preferred_element_type=F32)
                      * jnp.dot(h, wv_ref[...], preferred_element_type=F32))
        b = jnp.dot(h_scr[HALO:HALO + tm, :], wb_ref[...], preferred_element_type=F32)
        o_ref[...] = (b * conv(g_scr, cw_ref)).astype(o_ref.dtype)


def _up(x, mod, layer, ctx, nw, w, cw, mode, tm, tn):
    bn, s, _ = x.shape
    parts = 2 if mode == "ffn" else 3
    width = w.shape[1] // parts
    nj = width // tn
    hb = tm // HALO
    last_hb = s // HALO - 1
    in_specs = [
        pl.BlockSpec((None, tm, D_MODEL), lambda b, i, j: (b, i, 0)),
        pl.BlockSpec((None, HALO, D_MODEL), lambda b, i, j: (b, jnp.maximum(i * hb - 1, 0), 0)),
        pl.BlockSpec((None, HALO, D_MODEL), lambda b, i, j: (b, jnp.minimum((i + 1) * hb, last_hb), 0)),
        _mod_spec(layer, 3 if mode == "ffn" else 0, ctx),
        _mod_spec(layer, 4 if mode == "ffn" else 1, ctx),
        _const_spec((1, D_MODEL)),
    ]
    args = [x, x, x, mod, mod, nw]
    wspec = lambda p: pl.BlockSpec((D_MODEL, tn), lambda b, i, j: (0, p * nj + j))
    cspec = lambda p: pl.BlockSpec((3, tn), lambda b, i, j: (0, p * nj + j))
    scratch = [pltpu.VMEM((tm + 2 * HALO, D_MODEL), BF16), pltpu.VMEM((tm + 2 * HALO, tn), F32)]
    if mode == "ffn":
        in_specs += [wspec(0), wspec(1), cspec(0), cspec(1)]
        args += [w, w, cw, cw]
        scratch.append(pltpu.VMEM((tm + 2 * HALO, tn), F32))
    else:
        in_specs += [wspec(0), wspec(1), wspec(2), pl.BlockSpec((3, tn), lambda b, i, j: (0, j))]
        args += [w, w, w, cw]
    return pl.pallas_call(
        functools.partial(_up_body, mode=mode, tm=tm),
        grid=(bn, s // tm, nj),
        in_specs=in_specs,
        out_specs=pl.BlockSpec((None, tm, tn), lambda b, i, j: (b, i, j)),
        out_shape=jax.ShapeDtypeStruct((bn, s, width), BF16),
        scratch_shapes=scratch,
        compiler_params=_cparams(("parallel", "parallel", "arbitrary")),
        name=mode + "_up" + ("_ctx" if ctx else ""),
    )(*args)


def _rope_tables(seq, rot_dim):
    rows = seq // GRID_W
    r = jnp.repeat(jnp.arange(rows, dtype=F32), GRID_W)
    col = jnp.tile(jnp.arange(GRID_W, dtype=F32), rows)
    quarter = rot_dim // 4
    inv = ROPE_THETA ** (-jnp.arange(quarter, dtype=F32) / quarter)
    ar = r[:, None] * inv
    ac = col[:, None] * inv
    ang = jnp.concatenate([ar, ar, ac, ac], axis=-1)
    reps = 128 // rot_dim
    cos = jnp.tile(jnp.cos(ang), (1, reps))
    sin = jnp.tile(jnp.sin(ang), (1, reps))
    first = ((jnp.arange(128) // quarter) % 2 == 0)[None, :]
    return cos, jnp.where(first, -sin, 0.0), jnp.where(first, 0.0, sin)


def _attn_weights(w_in, qn, kn, mqn, mkvn, w_uq, w_ukv):
    w_in = jnp.pad(w_in, ((0, 0), (0, W_IN_PAD - w_in.shape[1]))).astype(BF16)
    uq = w_uq.reshape(B_Q_RANK, B_HEADS, B_NOPE_DIM + B_ROPE_DIM)
    uq_rope = jnp.pad(uq[:, :, B_NOPE_DIM:], ((0, 0), (0, 0), (0, 128 - B_ROPE_DIM)))
    w_uq = jnp.concatenate([uq[:, :, :B_NOPE_DIM].reshape(B_Q_RANK, -1),
                            uq_rope.reshape(B_Q_RANK, -1)], axis=1).astype(BF16)
    ukv = w_ukv.reshape(B_KV_RANK, B_HEADS, B_NOPE_DIM + B_V_DIM)
    w_ukv = jnp.concatenate([ukv[:, :, :B_NOPE_DIM].reshape(B_KV_RANK, -1),
                             ukv[:, :, B_NOPE_DIM:].reshape(B_KV_RANK, -1)], axis=1).astype(BF16)
    return (w_in, qn.reshape(1, -1), kn.reshape(1, -1), mqn.reshape(1, -1), mkvn.reshape(1, -1),
            w_uq, w_ukv)


def kernel(x, c, ctx, c_ctx, w_ada, b_ada, norm_mix, norm_ffn, attn_w_in, attn_q_norm, attn_k_norm,
           mla_q_norm, mla_kv_norm, mla_w_uq, mla_w_ukv, attn_w_o, sc_w_in, sc_conv, sc_w_out,
           ffn_w_up, ffn_conv, ffn_w_down, final_norm):
    bn, s, _ = x.shape
    cl = ctx.shape[1]
    assert bn <= MOD_ROWS // 2
    cvec = jnp.zeros((MOD_ROWS, D_MODEL), F32).at[:bn].set(c).at[MOD_ROWS // 2].set(c_ctx)
    mod = _adaln(cvec, w_ada, b_ada)
    rope = _rope_tables(s, A_HEAD_DIM) + _rope_tables(s, B_ROPE_DIM)

    xc = ctx
    for l in range(DEPTH):
        later_attn = any(j % 2 == 0 for j in range(l + 1, DEPTH))
        i = l // 2
        nw = norm_mix[l].reshape(1, -1)
        if l % 2 == 0:
            wts = _attn_weights(attn_w_in[i], attn_q_norm[i], attn_k_norm[i], mla_q_norm[i],
                                mla_kv_norm[i], mla_w_uq[i], mla_w_ukv[i])
            w_o = attn_w_o[i].astype(BF16)
            w_oa, w_ob = w_o[:A_HEADS * A_HEAD_DIM], w_o[A_HEADS * A_HEAD_DIM:]
            qa_c, qb_c, ka_c, va_c, kb_c, vb_c = _attn_proj(xc, mod, l, True, nw, wts, None, later_attn, cl)
            qa, qb, ka, va, kb, vb = _attn_proj(x, mod, l, False, nw, wts, rope, True, 256)
            oa = _flash(qa, ka_c, va_c, ka, va, group=A_GROUP, dq=A_HEAD_DIM, dv=A_HEAD_DIM,
                        tq=256, tk=512, name="gqa")
            ob = _flash(qb, kb_c, vb_c, kb, vb, group=1, dq=B_QK_PAD, dv=B_V_DIM,
                        tq=512, tk=512, name="mla")
            x = _proj_res(x, mod, l, 2, False, [oa, ob], [w_oa, w_ob], 512, name="attn_out")
            if later_attn:
                oa_c = _flash(qa_c, ka_c, va_c, None, None, group=A_GROUP, dq=A_HEAD_DIM, dv=A_HEAD_DIM,
                              tq=cl, tk=0, name="gqa_ctx")
                ob_c = _flash(qb_c, kb_c, vb_c, None, None, group=1, dq=B_QK_PAD, dv=B_V_DIM,
                              tq=cl, tk=0, name="mla_ctx")
                xc = _proj_res(xc, mod, l, 2, True, [oa_c, ob_c], [w_oa, w_ob], cl, name="attn_out")
        else:
            w_in = sc_w_in[i].astype(BF16)
            w_out = sc_w_out[i].astype(BF16)
            z = _up(x, mod, l, False, nw, w_in, sc_conv[i], "sc", 1024, 512)
            x = _proj_res(x, mod, l, 2, False, [z], [w_out], 512, name="sc_out")
            if later_attn:
                zc = _up(xc, mod, l, True, nw, w_in, sc_conv[i], "sc", cl, 512)
                xc = _proj_res(xc, mod, l, 2, True, [zc], [w_out], cl, name="sc_out")
        nwf = norm_ffn[l].reshape(1, -1)
        w_up = ffn_w_up[l].astype(BF16)
        w_down = ffn_w_down[l].astype(BF16)
        hid = _up(x, mod, l, False, nwf, w_up, ffn_conv[l], "ffn", 1024, 512)
        fw = final_norm.reshape(1, -1) if l == DEPTH - 1 else None
        x = _proj_res(x, mod, l, 5, False, [hid], [w_down], 256, final_w=fw, name="ffn_down")
        if later_attn:
            hid_c = _up(xc, mod, l, True, nwf, w_up, ffn_conv[l], "ffn", cl, 512)
            xc = _proj_res(xc, mod, l, 5, True, [hid_c], [w_down], cl, name="ffn_down")
    return x
```

```python
import functools

import jax
import jax.numpy as jnp
from jax import lax
from jax.experimental import pallas as pl
from jax.experimental.pallas import tpu as pltpu

F32 = jnp.float32
BF16 = jnp.bfloat16

D_MODEL = 2048
DEPTH = 4
GRID_W = 64
ROPE_THETA = 10000.0
EPS = 1e-6
A_HEADS = 8
A_KV_HEADS = 2
A_GROUP = A_HEADS // A_KV_HEADS
A_HEAD_DIM = 128
B_HEADS = 8
B_Q_RANK = 512
B_KV_RANK = 256
B_NOPE_DIM = 128
B_ROPE_DIM = 64
B_V_DIM = 128
B_QK_PAD = 256
LOG2E = 1.4426950408889634
A_SCALE = A_HEAD_DIM ** -0.5 * LOG2E
B_SCALE = (B_NOPE_DIM + B_ROPE_DIM) ** -0.5 * LOG2E
QGROUP = 256
SC_WIDTH = D_MODEL
D_FF = 256 * ((8 * D_MODEL // 3 + 255) // 256)
W_IN_PAD = 2432
MOD_ROWS = 16
HALO = 16
VMEM_LIMIT = 56 * 1024 * 1024


def _cparams(sem):
    return pltpu.CompilerParams(dimension_semantics=sem, vmem_limit_bytes=VMEM_LIMIT)


def _const_spec(shape):
    nd = len(shape)
    return pl.BlockSpec(shape, lambda *g: (0,) * nd, pipeline_mode=pl.Buffered(1))


def _mod_spec(layer, part, ctx):
    if ctx:
        idx = lambda b, *g: (layer, MOD_ROWS // 2, part, 0, 0)
    else:
        idx = lambda b, *g: (layer, b, part, 0, 0)
    return pl.BlockSpec((None, None, None, 1, D_MODEL), idx)


def _rms(x):
    return x * lax.rsqrt(jnp.mean(x * x, axis=-1, keepdims=True) + EPS)


def _norm_mod(x, nw, sc, sh):
    return (_rms(x) * nw) * (1.0 + sc) + sh


def _rope(x, cos, sin_up, sin_dn, quarter):
    n = x.shape[-1]
    return x * cos + pltpu.roll(x, n - quarter, 1) * sin_up + pltpu.roll(x, quarter, 1) * sin_dn


def _adaln_body(c_ref, w_ref, b_ref, o_ref):
    c = c_ref[...]
    s = (c / (1.0 + jnp.exp(-c))).astype(BF16)
    o_ref[...] = jnp.dot(s, w_ref[...].astype(BF16), preferred_element_type=F32) + b_ref[...]


def _adaln(cvec, w_ada, b_ada):
    tn = 1024
    n = 6 * D_MODEL
    out = pl.pallas_call(
        _adaln_body,
        grid=(DEPTH, n // tn),
        in_specs=[
            pl.BlockSpec((MOD_ROWS, D_MODEL), lambda l, j: (0, 0)),
            pl.BlockSpec((None, D_MODEL, tn), lambda l, j: (l, 0, j)),
            pl.BlockSpec((None, 1, tn), lambda l, j: (l, 0, j)),
        ],
        out_specs=pl.BlockSpec((None, MOD_ROWS, tn), lambda l, j: (l, 0, j)),
        out_shape=jax.ShapeDtypeStruct((DEPTH, MOD_ROWS, n), F32),
        compiler_params=_cparams(("parallel", "parallel")),
        name="adaln",
    )(cvec, w_ada, b_ada.reshape(DEPTH, 1, n))
    return out.reshape(DEPTH, MOD_ROWS, 6, 1, D_MODEL)


def _attn_proj_body(*refs, use_rope, with_q):
    it = iter(refs)
    x_ref, sh_ref, sc_ref, nw_ref, win_ref = (next(it) for _ in range(5))
    qn_ref, kn_ref, mqn_ref, mkvn_ref, wuq_ref, wukv_ref = (next(it) for _ in range(6))
    if use_rope:
        ca, sau, sad, cb, sbu, sbd = (next(it)[...] for _ in range(6))
    if with_q:
        qa_ref, qb_ref = next(it), next(it)
    ka_ref, va_ref, kb_ref, vb_ref = (next(it) for _ in range(4))

    def rope_a(v):
        return _rope(v, ca, sau, sad, A_HEAD_DIM // 4) if use_rope else v

    def rope_b(v):
        return _rope(v, cb, sbu, sbd, B_ROPE_DIM // 4) if use_rope else v

    h = _norm_mod(x_ref[...], nw_ref[...], sc_ref[...], sh_ref[...]).astype(BF16)
    proj = jnp.dot(h, win_ref[...], preferred_element_type=F32)
    q_end = A_HEADS * A_HEAD_DIM
    k_end = q_end + A_KV_HEADS * A_HEAD_DIM
    v_end = k_end + A_KV_HEADS * A_HEAD_DIM
    cq_end = v_end + B_Q_RANK
    ckv_end = cq_end + B_KV_RANK

    kn = kn_ref[...]
    for kh in range(A_KV_HEADS):
        lo = q_end + kh * A_HEAD_DIM
        k = rope_a(_rms(proj[:, lo:lo + A_HEAD_DIM]) * kn)
        ka_ref[:, kh * A_HEAD_DIM:(kh + 1) * A_HEAD_DIM] = k.astype(BF16)
    va_ref[...] = proj[:, k_end:v_end].T.astype(BF16)

    ckv = (_rms(proj[:, cq_end:ckv_end]) * mkvn_ref[...]).astype(BF16)
    kv = jnp.dot(ckv, wukv_ref[...], preferred_element_type=F32)
    nope_w = B_HEADS * B_NOPE_DIM
    vb_ref[...] = kv[:, nope_w:].T.astype(BF16)
    kr = rope_b(proj[:, ckv_end:W_IN_PAD]).astype(BF16)
    for hh in range(B_HEADS):
        lo = hh * B_QK_PAD
        kb_ref[:, lo:lo + B_NOPE_DIM] = kv[:, hh * B_NOPE_DIM:(hh + 1) * B_NOPE_DIM].astype(BF16)
        kb_ref[:, lo + B_NOPE_DIM:lo + B_QK_PAD] = kr

    if with_q:
        qn = qn_ref[...]
        for hh in range(A_HEADS):
            lo = hh * A_HEAD_DIM
            q = rope_a(_rms(proj[:, lo:lo + A_HEAD_DIM]) * qn) * A_SCALE
            qa_ref[:, lo:lo + A_HEAD_DIM] = q.astype(BF16)
        cq = (_rms(proj[:, v_end:cq_end]) * mqn_ref[...]).astype(BF16)
        qb = jnp.dot(cq, wuq_ref[...], preferred_element_type=F32)
        for hh in range(B_HEADS):
            lo = hh * B_QK_PAD
            qb_ref[:, lo:lo + B_NOPE_DIM] = (qb[:, hh * B_NOPE_DIM:(hh + 1) * B_NOPE_DIM] * B_SCALE).astype(BF16)
            r = rope_b(qb[:, nope_w + hh * 128:nope_w + (hh + 1) * 128]) * B_SCALE
            qb_ref[:, lo + B_NOPE_DIM:lo + B_QK_PAD] = r.astype(BF16)


def _attn_proj(x, mod, layer, ctx, nw, wts, rope, with_q, tm):
    bn, s, _ = x.shape
    use_rope = rope is not None
    w_in, qn, kn, mqn, mkvn, w_uq, w_ukv = wts
    row = lambda w: pl.BlockSpec((None, tm, w), lambda b, i: (b, i, 0))
    in_specs = [row(D_MODEL), _mod_spec(layer, 0, ctx), _mod_spec(layer, 1, ctx),
                _const_spec((1, D_MODEL)), _const_spec(w_in.shape),
                _const_spec(qn.shape), _const_spec(kn.shape), _const_spec(mqn.shape),
                _const_spec(mkvn.shape), _const_spec(w_uq.shape), _const_spec(w_ukv.shape)]
    args = [x, mod, mod, nw, w_in, qn, kn, mqn, mkvn, w_uq, w_ukv]
    if use_rope:
        in_specs += [pl.BlockSpec((tm, 128), lambda b, i: (i, 0))] * 6
        args += list(rope)
    col = lambda w: pl.BlockSpec((None, w, tm), lambda b, i: (b, 0, i))
    outs_desc = ([(A_HEADS * A_HEAD_DIM, False), (B_HEADS * B_QK_PAD, False)] if with_q else []) + [
        (A_KV_HEADS * A_HEAD_DIM, False), (A_KV_HEADS * A_HEAD_DIM, True),
        (B_HEADS * B_QK_PAD, False), (B_HEADS * B_V_DIM, True)]
    outs = pl.pallas_call(
        functools.partial(_attn_proj_body, use_rope=use_rope, with_q=with_q),
        grid=(bn, s // tm),
        in_specs=in_specs,
        out_specs=[col(w) if t else row(w) for w, t in outs_desc],
        out_shape=[jax.ShapeDtypeStruct((bn, w, s) if t else (bn, s, w), BF16) for w, t in outs_desc],
        compiler_params=_cparams(("parallel", "parallel")),
        name="attn_proj_ctx" if ctx else "attn_proj",
    )(*args)
    return outs if with_q else [None, None] + list(outs)


def _col_reduce(x, op, final):
    while x.shape[0] > 8 and x.shape[0] % 16 == 0:
        h = x.shape[0] // 2
        x = op(x[:h], x[h:])
    return final(x, axis=0, keepdims=True)


def _flash_body(*refs, group, dq, dv, tk, n_lat, depth=3):
    if n_lat:
        q_ref, kc_ref, vct_ref, k_ref, vt_ref, o_ref = refs
    else:
        q_ref, kc_ref, vct_ref, o_ref = refs
    tq = q_ref.shape[0]
    if group > 1:
        assert tq == QGROUP
        q_parts = [(slice(None), slice(g * dq, (g + 1) * dq)) for g in range(group)]
        o_parts = [(slice(None), slice(g * dv, (g + 1) * dv)) for g in range(group)]
    else:
        q_parts = [(slice(g * QGROUP, (g + 1) * QGROUP), slice(None)) for g in range(tq // QGROUP)]
        o_parts = q_parts
    n = len(q_parts)
    chunks = [(kc_ref, vct_ref, 0, kc_ref.shape[0])] + [(k_ref, vt_ref, c * tk, tk) for c in range(n_lat)]
    m, l, acc = [None] * n, [None] * n, [None] * n
    blocks = [(ci, g) for ci in range(len(chunks)) for g in range(n)]

    def scores(ci, g):
        kr, _, off, size = chunks[ci]
        return lax.dot_general(kr[off:off + size, :], q_ref[q_parts[g]], (((1,), (1,)), ((), ())),
                               preferred_element_type=F32)

    s_next = [scores(*blocks[d]) for d in range(min(depth, len(blocks)))]
    for idx, (ci, g) in enumerate(blocks):
        s = s_next.pop(0)
        if idx + depth < len(blocks):
            s_next.append(scores(*blocks[idx + depth]))
        _, vr, off, size = chunks[ci]
        vt = vr[:, off:off + size]
        smax = _col_reduce(s, jnp.maximum, jnp.max)
        if ci == 0:
            m_new = smax
            p = jnp.exp2(s - m_new)
            l[g] = _col_reduce(p, jnp.add, jnp.sum)
            acc[g] = jnp.dot(vt, p.astype(BF16), preferred_element_type=F32)
        else:
            m_new = jnp.maximum(m[g], smax)
            alpha = jnp.exp2(m[g] - m_new)
            p = jnp.exp2(s - m_new)
            l[g] = alpha * l[g] + _col_reduce(p, jnp.add, jnp.sum)
            acc[g] = alpha * acc[g] + jnp.dot(vt, p.astype(BF16), preferred_element_type=F32)
        m[g] = m_new
    for g in range(n):
        o_ref[o_parts[g]] = (acc[g] / l[g]).T.astype(o_ref.dtype)


def _flash(q, kc, vct, k, vt, *, group, dq, dv, tq, tk, name):
    bn, s, qw = q.shape
    kvh = qw // (group * dq)
    c = kc.shape[1]
    n_lat = 0 if k is None else k.shape[1] // tk
    in_specs = [pl.BlockSpec((None, tq, group * dq), lambda b, h, i: (b, i, h)),
                pl.BlockSpec((None, c, dq), lambda b, h, i: (b, 0, h)),
                pl.BlockSpec((None, dv, c), lambda b, h, i: (b, h, 0))]
    args = [q, kc, vct]
    if n_lat:
        t = k.shape[1]
        in_specs += [pl.BlockSpec((None, t, dq), lambda b, h, i: (b, 0, h)),
                     pl.BlockSpec((None, dv, t), lambda b, h, i: (b, h, 0))]
        args += [k, vt]
    return pl.pallas_call(
        functools.partial(_flash_body, group=group, dq=dq, dv=dv, tk=tk, n_lat=n_lat),
        grid=(bn, kvh, s // tq),
        in_specs=in_specs,
        out_specs=pl.BlockSpec((None, tq, group * dv), lambda b, h, i: (b, i, h)),
        out_shape=jax.ShapeDtypeStruct((bn, s, kvh * group * dv), BF16),
        compiler_params=_cparams(("parallel", "parallel", "parallel")),
        name=name,
    )(*args)


def _proj_res_body(*refs, n_lhs, final):
    x_ref, g_ref = refs[0], refs[1]
    lhs = refs[2:2 + n_lhs]
    ws = refs[2 + n_lhs:2 + 2 * n_lhs]
    rest = refs[2 + 2 * n_lhs:]
    y = jnp.dot(lhs[0][...], ws[0][...], preferred_element_type=F32)
    for a, w in zip(lhs[1:], ws[1:]):
        y = y + jnp.dot(a[...], w[...], preferred_element_type=F32)
    out = x_ref[...] + g_ref[...] * y
    if final:
        fw_ref, o_ref = rest
        out = _rms(out) * fw_ref[...]
    else:
        (o_ref,) = rest
    o_ref[...] = out


def _proj_res(x, mod, layer, part, ctx, lhs, ws, tm, final_w=None, name="proj_res"):
    bn, s, _ = x.shape
    row = lambda w: pl.BlockSpec((None, tm, w), lambda b, i: (b, i, 0))
    in_specs = [row(D_MODEL), _mod_spec(layer, part, ctx)]
    in_specs += [row(a.shape[-1]) for a in lhs] + [_const_spec(w.shape) for w in ws]
    args = [x, mod] + list(lhs) + list(ws)
    if final_w is not None:
        in_specs.append(_const_spec((1, D_MODEL)))
        args.append(final_w)
    return pl.pallas_call(
        functools.partial(_proj_res_body, n_lhs=len(lhs), final=final_w is not None),
        grid=(bn, s // tm),
        in_specs=in_specs,
        out_specs=row(D_MODEL),
        out_shape=jax.ShapeDtypeStruct(x.shape, F32),
        compiler_params=_cparams(("parallel", "parallel")),
        name=name + ("_ctx" if ctx else ""),
    )(*args)


def _up_body(*refs, mode, tm):
    x_ref, xp_ref, xn_ref, sh_ref, sc_ref, nw_ref = refs[:6]
    i = pl.program_id(1)
    j = pl.program_id(2)

    if mode == "ffn":
        wg_ref, wu_ref, cg_ref, cu_ref, o_ref, h_scr, g_scr, u_scr = refs[6:]
    else:
        wb_ref, wc_ref, wv_ref, cw_ref, o_ref, h_scr, g_scr = refs[6:]

    @pl.when(j == 0)
    def _():
        nw, sc, sh = nw_ref[...], sc_ref[...], sh_ref[...]
        rows = min(tm, 256)
        for r in range(0, tm, rows):
            h_scr[HALO + r:HALO + r + rows, :] = _norm_mod(x_ref[r:r + rows, :], nw, sc, sh).astype(BF16)
        hp = _norm_mod(xp_ref[...], nw, sc, sh)
        h_scr[0:HALO, :] = jnp.where(i == 0, 0.0, hp).astype(BF16)
        hn = _norm_mod(xn_ref[...], nw, sc, sh)
        h_scr[HALO + tm:2 * HALO + tm, :] = jnp.where(i == pl.num_programs(1) - 1, 0.0, hn).astype(BF16)

    def conv(scr, cw_ref):
        return (cw_ref[0:1, :] * scr[HALO - 1:HALO - 1 + tm, :]
                + cw_ref[1:2, :] * scr[HALO:HALO + tm, :]
                + cw_ref[2:3, :] * scr[HALO + 1:HALO + 1 + tm, :])

    h = h_scr[...]
    if mode == "ffn":
        g_scr[...] = jnp.dot(h, wg_ref[...], preferred_element_type=F32)
        u_scr[...] = jnp.dot(h, wu_ref[...], preferred_element_type=F32)
        g = conv(g_scr, cg_ref)
        u = conv(u_scr, cu_ref)
        o_ref[...] = ((g / (1.0 + jnp.exp(-g))) * u).astype(o_ref.dtype)
    else:
        g_scr[...] = (jnp.dot(h, wc_ref[...], preferred_element_type=F32)
                      * jnp.dot(h, wv_ref[...], preferred_element_type=F32))
        b = jnp.dot(h_scr[HALO:HALO + tm, :], wb_ref[...], preferred_element_type=F32)
        o_ref[...] = (b * conv(g_scr, cw_ref)).astype(o_ref.dtype)


def _up(x, mod, layer, ctx, nw, w, cw, mode, tm, tn):
    bn, s, _ = x.shape
    parts = 2 if mode == "ffn" else 3
    width = w.shape[1] // parts
    nj = width // tn
    hb = tm // HALO
    last_hb = s // HALO - 1
    in_specs = [
        pl.BlockSpec((None, tm, D_MODEL), lambda b, i, j: (b, i, 0)),
        pl.BlockSpec((None, HALO, D_MODEL), lambda b, i, j: (b, jnp.maximum(i * hb - 1, 0), 0)),
        pl.BlockSpec((None, HALO, D_MODEL), lambda b, i, j: (b, jnp.minimum((i + 1) * hb, last_hb), 0)),
        _mod_spec(layer, 3 if mode == "ffn" else 0, ctx),
        _mod_spec(layer, 4 if mode == "ffn" else 1, ctx),
        _const_spec((1, D_MODEL)),
    ]
    args = [x, x, x, mod, mod, nw]
    wspec = lambda p: pl.BlockSpec((D_MODEL, tn), lambda b, i, j: (0, p * nj + j))
    cspec = lambda p: pl.BlockSpec((3, tn), lambda b, i, j: (0, p * nj + j))
    scratch = [pltpu.VMEM((tm + 2 * HALO, D_MODEL), BF16), pltpu.VMEM((tm + 2 * HALO, tn), F32)]
    if mode == "ffn":
        in_specs += [wspec(0), wspec(1), cspec(0), cspec(1)]
        args += [w, w, cw, cw]
        scratch.append(pltpu.VMEM((tm + 2 * HALO, tn), F32))
    else:
        in_specs += [wspec(0), wspec(1), wspec(2), pl.BlockSpec((3, tn), lambda b, i, j: (0, j))]
        args += [w, w, w, cw]
    return pl.pallas_call(
        functools.partial(_up_body, mode=mode, tm=tm),
        grid=(bn, s // tm, nj),
        in_specs=in_specs,
        out_specs=pl.BlockSpec((None, tm, tn), lambda b, i, j: (b, i, j)),
        out_shape=jax.ShapeDtypeStruct((bn, s, width), BF16),
        scratch_shapes=scratch,
        compiler_params=_cparams(("parallel", "parallel", "arbitrary")),
        name=mode + "_up" + ("_ctx" if ctx else ""),
    )(*args)


def _rope_tables(seq, rot_dim):
    rows = seq // GRID_W
    r = jnp.repeat(jnp.arange(rows, dtype=F32), GRID_W)
    col = jnp.tile(jnp.arange(GRID_W, dtype=F32), rows)
    quarter = rot_dim // 4
    inv = ROPE_THETA ** (-jnp.arange(quarter, dtype=F32) / quarter)
    ar = r[:, None] * inv
    ac = col[:, None] * inv
    ang = jnp.concatenate([ar, ar, ac, ac], axis=-1)
    reps = 128 // rot_dim
    cos = jnp.tile(jnp.cos(ang), (1, reps))
    sin = jnp.tile(jnp.sin(ang), (1, reps))
    first = ((jnp.arange(128) // quarter) % 2 == 0)[None, :]
    return cos, jnp.where(first, -sin, 0.0), jnp.where(first, 0.0, sin)


def _attn_weights(w_in, qn, kn, mqn, mkvn, w_uq, w_ukv):
    w_in = jnp.pad(w_in, ((0, 0), (0, W_IN_PAD - w_in.shape[1]))).astype(BF16)
    uq = w_uq.reshape(B_Q_RANK, B_HEADS, B_NOPE_DIM + B_ROPE_DIM)
    uq_rope = jnp.pad(uq[:, :, B_NOPE_DIM:], ((0, 0), (0, 0), (0, 128 - B_ROPE_DIM)))
    w_uq = jnp.concatenate([uq[:, :, :B_NOPE_DIM].reshape(B_Q_RANK, -1),
                            uq_rope.reshape(B_Q_RANK, -1)], axis=1).astype(BF16)
    ukv = w_ukv.reshape(B_KV_RANK, B_HEADS, B_NOPE_DIM + B_V_DIM)
    w_ukv = jnp.concatenate([ukv[:, :, :B_NOPE_DIM].reshape(B_KV_RANK, -1),
                             ukv[:, :, B_NOPE_DIM:].reshape(B_KV_RANK, -1)], axis=1).astype(BF16)
    return (w_in, qn.reshape(1, -1), kn.reshape(1, -1), mqn.reshape(1, -1), mkvn.reshape(1, -1),
            w_uq, w_ukv)


def kernel(x, c, ctx, c_ctx, w_ada, b_ada, norm_mix, norm_ffn, attn_w_in, attn_q_norm, attn_k_norm,
           mla_q_norm, mla_kv_norm, mla_w_uq, mla_w_ukv, attn_w_o, sc_w_in, sc_conv, sc_w_out,
           ffn_w_up, ffn_conv, ffn_w_down, final_norm):
    bn, s, _ = x.shape
    cl = ctx.shape[1]
    assert bn <= MOD_ROWS // 2
    cvec = jnp.zeros((MOD_ROWS, D_MODEL), F32).at[:bn].set(c).at[MOD_ROWS // 2].set(c_ctx)
    mod = _adaln(cvec, w_ada, b_ada)
    rope = _rope_tables(s, A_HEAD_DIM) + _rope_tables(s, B_ROPE_DIM)

    xc = ctx
    for l in range(DEPTH):
        later_attn = any(j % 2 == 0 for j in range(l + 1, DEPTH))
        i = l // 2
        nw = norm_mix[l].reshape(1, -1)
        if l % 2 == 0:
            wts = _attn_weights(attn_w_in[i], attn_q_norm[i], attn_k_norm[i], mla_q_norm[i],
                                mla_kv_norm[i], mla_w_uq[i], mla_w_ukv[i])
            w_o = attn_w_o[i].astype(BF16)
            w_oa, w_ob = w_o[:A_HEADS * A_HEAD_DIM], w_o[A_HEADS * A_HEAD_DIM:]
            qa_c, qb_c, ka_c, va_c, kb_c, vb_c = _attn_proj(xc, mod, l, True, nw, wts, None, later_attn, cl)
            qa, qb, ka, va, kb, vb = _attn_proj(x, mod, l, False, nw, wts, rope, True, 256)
            oa = _flash(qa, ka_c, va_c, ka, va, group=A_GROUP, dq=A_HEAD_DIM, dv=A_HEAD_DIM,
                        tq=256, tk=512, name="gqa")
            ob = _flash(qb, kb_c, vb_c, kb, vb, group=1, dq=B_QK_PAD, dv=B_V_DIM,
                        tq=1024, tk=512, name="mla")
            x = _proj_res(x, mod, l, 2, False, [oa, ob], [w_oa, w_ob], 512, name="attn_out")
            if later_attn:
                oa_c = _flash(qa_c, ka_c, va_c, None, None, group=A_GROUP, dq=A_HEAD_DIM, dv=A_HEAD_DIM,
                              tq=cl, tk=0, name="gqa_ctx")
                ob_c = _flash(qb_c, kb_c, vb_c, None, None, group=1, dq=B_QK_PAD, dv=B_V_DIM,
                              tq=cl, tk=0, name="mla_ctx")
                xc = _proj_res(xc, mod, l, 2, True, [oa_c, ob_c], [w_oa, w_ob], cl, name="attn_out")
        else:
            w_in = sc_w_in[i].astype(BF16)
            w_out = sc_w_out[i].astype(BF16)
            z = _up(x, mod, l, False, nw, w_in, sc_conv[i], "sc", 1024, 512)
            x = _proj_res(x, mod, l, 2, False, [z], [w_out], 512, name="sc_out")
            if later_attn:
                zc = _up(xc, mod, l, True, nw, w_in, sc_conv[i], "sc", cl, 512)
                xc = _proj_res(xc, mod, l, 2, True, [zc], [w_out], cl, name="sc_out")
        nwf = norm_ffn[l].reshape(1, -1)
        w_up = ffn_w_up[l].astype(BF16)
        w_down = ffn_w_down[l].astype(BF16)
        hid = _up(x, mod, l, False, nwf, w_up, ffn_conv[l], "ffn", 1024, 512)
        fw = final_norm.reshape(1, -1) if l == DEPTH - 1 else None
        x = _proj_res(x, mod, l, 5, False, [hid], [w_down], 256, final_w=fw, name="ffn_down")
        if later_attn:
            hid_c = _up(xc, mod, l, True, nwf, w_up, ffn_conv[l], "ffn", cl, 512)
            xc = _proj_res(xc, mod, l, 5, True, [hid_c], [w_down], cl, name="ffn_down")
    return x
```

```python
import functools

import jax
import jax.numpy as jnp
from jax import lax
from jax.experimental import pallas as pl
from jax.experimental.pallas import tpu as pltpu

F32 = jnp.float32
BF16 = jnp.bfloat16

D_MODEL = 2048
DEPTH = 4
GRID_W = 64
ROPE_THETA = 10000.0
EPS = 1e-6
A_HEADS = 8
A_KV_HEADS = 2
A_GROUP = A_HEADS // A_KV_HEADS
A_HEAD_DIM = 128
B_HEADS = 8
B_Q_RANK = 512
B_KV_RANK = 256
B_NOPE_DIM = 128
B_ROPE_DIM = 64
B_V_DIM = 128
B_QK_PAD = 256
LOG2E = 1.4426950408889634
A_SCALE = A_HEAD_DIM ** -0.5 * LOG2E
B_SCALE = (B_NOPE_DIM + B_ROPE_DIM) ** -0.5 * LOG2E
QGROUP = 256
SC_WIDTH = D_MODEL
D_FF = 256 * ((8 * D_MODEL // 3 + 255) // 256)
W_IN_PAD = 2432
MOD_ROWS = 16
HALO = 16
EPI_ROWS = 32
VMEM_LIMIT = 56 * 1024 * 1024


def _cparams(sem):
    return pltpu.CompilerParams(dimension_semantics=sem, vmem_limit_bytes=VMEM_LIMIT)


def _const_spec(shape):
    nd = len(shape)
    return pl.BlockSpec(shape, lambda *g: (0,) * nd, pipeline_mode=pl.Buffered(1))


def _mod_spec(layer, part, ctx):
    if ctx:
        idx = lambda b, *g: (layer, MOD_ROWS // 2, part, 0, 0)
    else:
        idx = lambda b, *g: (layer, b, part, 0, 0)
    return pl.BlockSpec((None, None, None, 1, D_MODEL), idx)


def _rms(x):
    return x * lax.rsqrt(jnp.mean(x * x, axis=-1, keepdims=True) + EPS)


def _norm_mod(x, nw, sc, sh):
    return (_rms(x) * nw) * (1.0 + sc) + sh


def _rope(x, cos, sin_up, sin_dn, quarter):
    n = x.shape[-1]
    return x * cos + pltpu.roll(x, n - quarter, 1) * sin_up + pltpu.roll(x, quarter, 1) * sin_dn


def _adaln_body(c_ref, w_ref, b_ref, o_ref):
    c = c_ref[...]
    s = (c / (1.0 + jnp.exp(-c))).astype(BF16)
    o_ref[...] = jnp.dot(s, w_ref[...].astype(BF16), preferred_element_type=F32) + b_ref[...]


def _adaln(cvec, w_ada, b_ada):
    tn = 1024
    n = 6 * D_MODEL
    out = pl.pallas_call(
        _adaln_body,
        grid=(DEPTH, n // tn),
        in_specs=[
            pl.BlockSpec((MOD_ROWS, D_MODEL), lambda l, j: (0, 0)),
            pl.BlockSpec((None, D_MODEL, tn), lambda l, j: (l, 0, j)),
            pl.BlockSpec((None, 1, tn), lambda l, j: (l, 0, j)),
        ],
        out_specs=pl.BlockSpec((None, MOD_ROWS, tn), lambda l, j: (l, 0, j)),
        out_shape=jax.ShapeDtypeStruct((DEPTH, MOD_ROWS, n), F32),
        compiler_params=_cparams(("parallel", "parallel")),
        name="adaln",
    )(cvec, w_ada, b_ada.reshape(DEPTH, 1, n))
    return out.reshape(DEPTH, MOD_ROWS, 6, 1, D_MODEL)


def _attn_proj_body(*refs, use_rope, with_q):
    it = iter(refs)
    x_ref, sh_ref, sc_ref, nw_ref, win_ref = (next(it) for _ in range(5))
    qn_ref, kn_ref, mqn_ref, mkvn_ref, wuq_ref, wukv_ref = (next(it) for _ in range(6))
    if use_rope:
        ca, sau, sad, cb, sbu, sbd = (next(it)[...] for _ in range(6))
    if with_q:
        qa_ref, qb_ref = next(it), next(it)
    ka_ref, va_ref, kb_ref, vb_ref = (next(it) for _ in range(4))

    def rope_a(v):
        return _rope(v, ca, sau, sad, A_HEAD_DIM // 4) if use_rope else v

    def rope_b(v):
        return _rope(v, cb, sbu, sbd, B_ROPE_DIM // 4) if use_rope else v

    h = _norm_mod(x_ref[...], nw_ref[...], sc_ref[...], sh_ref[...]).astype(BF16)
    proj = jnp.dot(h, win_ref[...], preferred_element_type=F32)
    q_end = A_HEADS * A_HEAD_DIM
    k_end = q_end + A_KV_HEADS * A_HEAD_DIM
    v_end = k_end + A_KV_HEADS * A_HEAD_DIM
    cq_end = v_end + B_Q_RANK
    ckv_end = cq_end + B_KV_RANK

    kn = kn_ref[...]
    for kh in range(A_KV_HEADS):
        lo = q_end + kh * A_HEAD_DIM
        k = rope_a(_rms(proj[:, lo:lo + A_HEAD_DIM]) * kn)
        ka_ref[:, kh * A_HEAD_DIM:(kh + 1) * A_HEAD_DIM] = k.astype(BF16)
    va_ref[...] = proj[:, k_end:v_end].T.astype(BF16)

    ckv = (_rms(proj[:, cq_end:ckv_end]) * mkvn_ref[...]).astype(BF16)
    kv = jnp.dot(ckv, wukv_ref[...], preferred_element_type=F32)
    nope_w = B_HEADS * B_NOPE_DIM
    vb_ref[...] = kv[:, nope_w:].T.astype(BF16)
    kr = rope_b(proj[:, ckv_end:W_IN_PAD]).astype(BF16)
    for hh in range(B_HEADS):
        lo = hh * B_QK_PAD
        kb_ref[:, lo:lo + B_NOPE_DIM] = kv[:, hh * B_NOPE_DIM:(hh + 1) * B_NOPE_DIM].astype(BF16)
        kb_ref[:, lo + B_NOPE_DIM:lo + B_QK_PAD] = kr

    if with_q:
        qn = qn_ref[...]
        for hh in range(A_HEADS):
            lo = hh * A_HEAD_DIM
            q = rope_a(_rms(proj[:, lo:lo + A_HEAD_DIM]) * qn) * A_SCALE
            qa_ref[:, lo:lo + A_HEAD_DIM] = q.astype(BF16)
        cq = (_rms(proj[:, v_end:cq_end]) * mqn_ref[...]).astype(BF16)
        qb = jnp.dot(cq, wuq_ref[...], preferred_element_type=F32)
        for hh in range(B_HEADS):
            lo = hh * B_QK_PAD
            qb_ref[:, lo:lo + B_NOPE_DIM] = (qb[:, hh * B_NOPE_DIM:(hh + 1) * B_NOPE_DIM] * B_SCALE).astype(BF16)
            r = rope_b(qb[:, nope_w + hh * 128:nope_w + (hh + 1) * 128]) * B_SCALE
            qb_ref[:, lo + B_NOPE_DIM:lo + B_QK_PAD] = r.astype(BF16)


def _attn_proj(x, mod, layer, ctx, nw, wts, rope, with_q, tm):
    bn, s, _ = x.shape
    use_rope = rope is not None
    w_in, qn, kn, mqn, mkvn, w_uq, w_ukv = wts
    row = lambda w: pl.BlockSpec((None, tm, w), lambda b, i: (b, i, 0))
    in_specs = [row(D_MODEL), _mod_spec(layer, 0, ctx), _mod_spec(layer, 1, ctx),
                _const_spec((1, D_MODEL)), _const_spec(w_in.shape),
                _const_spec(qn.shape), _const_spec(kn.shape), _const_spec(mqn.shape),
                _const_spec(mkvn.shape), _const_spec(w_uq.shape), _const_spec(w_ukv.shape)]
    args = [x, mod, mod, nw, w_in, qn, kn, mqn, mkvn, w_uq, w_ukv]
    if use_rope:
        in_specs += [pl.BlockSpec((tm, 128), lambda b, i: (i, 0))] * 6
        args += list(rope)
    col = lambda w: pl.BlockSpec((None, w, tm), lambda b, i: (b, 0, i))
    outs_desc = ([(A_HEADS * A_HEAD_DIM, False), (B_HEADS * B_QK_PAD, False)] if with_q else []) + [
        (A_KV_HEADS * A_HEAD_DIM, False), (A_KV_HEADS * A_HEAD_DIM, True),
        (B_HEADS * B_QK_PAD, False), (B_HEADS * B_V_DIM, True)]
    outs = pl.pallas_call(
        functools.partial(_attn_proj_body, use_rope=use_rope, with_q=with_q),
        grid=(bn, s // tm),
        in_specs=in_specs,
        out_specs=[col(w) if t else row(w) for w, t in outs_desc],
        out_shape=[jax.ShapeDtypeStruct((bn, w, s) if t else (bn, s, w), BF16) for w, t in outs_desc],
        compiler_params=_cparams(("parallel", "parallel")),
        name="attn_proj_ctx" if ctx else "attn_proj",
    )(*args)
    return outs if with_q else [None, None] + list(outs)


def _col_reduce(x, op, final):
    while x.shape[0] > 8 and x.shape[0] % 16 == 0:
        h = x.shape[0] // 2
        x = op(x[:h], x[h:])
    return final(x, axis=0, keepdims=True)


def _flash_body(*refs, group, dq, dv, tk, n_lat, depth=3):
    if n_lat:
        q_ref, kc_ref, vct_ref, k_ref, vt_ref, o_ref = refs
    else:
        q_ref, kc_ref, vct_ref, o_ref = refs
    tq = q_ref.shape[0]
    if group > 1:
        assert tq == QGROUP
        q_parts = [(slice(None), slice(g * dq, (g + 1) * dq)) for g in range(group)]
        o_parts = [(slice(None), slice(g * dv, (g + 1) * dv)) for g in range(group)]
    else:
        q_parts = [(slice(g * QGROUP, (g + 1) * QGROUP), slice(None)) for g in range(tq // QGROUP)]
        o_parts = q_parts
    n = len(q_parts)
    chunks = [(kc_ref, vct_ref, 0, kc_ref.shape[0])] + [(k_ref, vt_ref, c * tk, tk) for c in range(n_lat)]
    m, l, acc = [None] * n, [None] * n, [None] * n
    blocks = [(ci, g) for ci in range(len(chunks)) for g in range(n)]

    def scores(ci, g):
        kr, _, off, size = chunks[ci]
        return lax.dot_general(kr[off:off + size, :], q_ref[q_parts[g]], (((1,), (1,)), ((), ())),
                               preferred_element_type=F32)

    s_next = [scores(*blocks[d]) for d in range(min(depth, len(blocks)))]
    for idx, (ci, g) in enumerate(blocks):
        s = s_next.pop(0)
        if idx + depth < len(blocks):
            s_next.append(scores(*blocks[idx + depth]))
        _, vr, off, size = chunks[ci]
        vt = vr[:, off:off + size]
        smax = _col_reduce(s, jnp.maximum, jnp.max)
        if ci == 0:
            m_new = smax
            p = jnp.exp2(s - m_new)
            l[g] = _col_reduce(p, jnp.add, jnp.sum)
            acc[g] = jnp.dot(vt, p.astype(BF16), preferred_element_type=F32)
        else:
            m_new = jnp.maximum(m[g], smax)
            alpha = jnp.exp2(m[g] - m_new)
            p = jnp.exp2(s - m_new)
            l[g] = alpha * l[g] + _col_reduce(p, jnp.add, jnp.sum)
            acc[g] = alpha * acc[g] + jnp.dot(vt, p.astype(BF16), preferred_element_type=F32)
        m[g] = m_new
    for g in range(n):
        o_ref[o_parts[g]] = (acc[g] / l[g]).T.astype(o_ref.dtype)


def _flash(q, kc, vct, k, vt, *, group, dq, dv, tq, tk, name):
    bn, s, qw = q.shape
    kvh = qw // (group * dq)
    c = kc.shape[1]
    n_lat = 0 if k is None else k.shape[1] // tk
    in_specs = [pl.BlockSpec((None, tq, group * dq), lambda b, h, i: (b, i, h)),
                pl.BlockSpec((None, c, dq), lambda b, h, i: (b, 0, h)),
                pl.BlockSpec((None, dv, c), lambda b, h, i: (b, h, 0))]
    args = [q, kc, vct]
    if n_lat:
        t = k.shape[1]
        in_specs += [pl.BlockSpec((None, t, dq), lambda b, h, i: (b, 0, h)),
                     pl.BlockSpec((None, dv, t), lambda b, h, i: (b, h, 0))]
        args += [k, vt]
    return pl.pallas_call(
        functools.partial(_flash_body, group=group, dq=dq, dv=dv, tk=tk, n_lat=n_lat),
        grid=(bn, kvh, s // tq),
        in_specs=in_specs,
        out_specs=pl.BlockSpec((None, tq, group * dv), lambda b, h, i: (b, i, h)),
        out_shape=jax.ShapeDtypeStruct((bn, s, kvh * group * dv), BF16),
        compiler_params=_cparams(("parallel", "parallel", "parallel")),
        name=name,
    )(*args)


def _proj_res_body(*refs, n_lhs, final):
    x_ref, g_ref = refs[0], refs[1]
    lhs = refs[2:2 + n_lhs]
    ws = refs[2 + n_lhs:2 + 2 * n_lhs]
    rest = refs[2 + 2 * n_lhs:]
    y = jnp.dot(lhs[0][...], ws[0][...], preferred_element_type=F32)
    for a, w in zip(lhs[1:], ws[1:]):
        y = y + jnp.dot(a[...], w[...], preferred_element_type=F32)
    out = x_ref[...] + g_ref[...] * y
    if final:
        fw_ref, o_ref = rest
        out = _rms(out) * fw_ref[...]
    else:
        (o_ref,) = rest
    o_ref[...] = out


def _proj_res(x, mod, layer, part, ctx, lhs, ws, tm, final_w=None, name="proj_res"):
    bn, s, _ = x.shape
    row = lambda w: pl.BlockSpec((None, tm, w), lambda b, i: (b, i, 0))
    in_specs = [row(D_MODEL), _mod_spec(layer, part, ctx)]
    in_specs += [row(a.shape[-1]) for a in lhs] + [_const_spec(w.shape) for w in ws]
    args = [x, mod] + list(lhs) + list(ws)
    if final_w is not None:
        in_specs.append(_const_spec((1, D_MODEL)))
        args.append(final_w)
    return pl.pallas_call(
        functools.partial(_proj_res_body, n_lhs=len(lhs), final=final_w is not None),
        grid=(bn, s // tm),
        in_specs=in_specs,
        out_specs=row(D_MODEL),
        out_shape=jax.ShapeDtypeStruct(x.shape, F32),
        compiler_params=_cparams(("parallel", "parallel")),
        name=name + ("_ctx" if ctx else ""),
    )(*args)


def _up_body(*refs, mode, tm, ni, nj, total):
    x_ref, xp_ref, xn_ref, sh_ref, sc_ref, nw_ref = refs[:6]
    t = pl.program_id(0)
    cur = jnp.minimum(t, total - 1)
    i = (cur // nj) % ni
    j = cur % nj
    if mode == "ffn":
        wg_ref, wu_ref, cg_ref, cu_ref, o_ref, h_scr, g0, g1, u0, u1 = refs[6:]
        g_raw, u_raw = (g0, g1), (u0, u1)
        second = u_raw
    else:
        wb_ref, wc_ref, wv_ref, cw_ref, o_ref, h_scr, g0, g1, b0, b1 = refs[6:]
        g_raw, b_raw = (g0, g1), (b0, b1)
        second = b_raw

    @pl.when(t == 0)
    def _():
        for r in (g_raw[1], second[1]):
            r[...] = jnp.zeros(r.shape, F32)

    @pl.when(j == 0)
    def _():
        gain = nw_ref[...] * (1.0 + sc_ref[...])
        sh = sh_ref[...]
        for r in range(0, tm, EPI_ROWS):
            h_scr[HALO + r:HALO + r + EPI_ROWS, :] = (
                _rms(x_ref[r:r + EPI_ROWS, :]) * gain + sh).astype(BF16)
        hp = _rms(xp_ref[...]) * gain + sh
        h_scr[0:HALO, :] = jnp.where(i == 0, 0.0, hp).astype(BF16)
        hn = _rms(xn_ref[...]) * gain + sh
        h_scr[HALO + tm:2 * HALO + tm, :] = jnp.where(i == ni - 1, 0.0, hn).astype(BF16)

    def conv(scr, cw_ref, r):
        return (cw_ref[0:1, :] * scr[HALO - 1 + r:HALO - 1 + r + EPI_ROWS, :]
                + cw_ref[1:2, :] * scr[HALO + r:HALO + r + EPI_ROWS, :]
                + cw_ref[2:3, :] * scr[HALO + 1 + r:HALO + 1 + r + EPI_ROWS, :])

    rows = tm + 2 * HALO
    dot_rows = max(d for d in range(16, 193, 16) if rows % d == 0)

    def step(slot):
        prev = 1 - slot
        rs = lambda r: slice(r, r + EPI_ROWS)
        if mode == "ffn":
            def epi(r):
                g = conv(g_raw[prev], cg_ref, r)
                u = conv(u_raw[prev], cu_ref, r)
                o_ref[rs(r), :] = ((g / (1.0 + jnp.exp(-g))) * u).astype(o_ref.dtype)

            def mm(r):
                h = h_scr[r:r + dot_rows, :]
                g_raw[slot][r:r + dot_rows, :] = jnp.dot(h, wg_ref[...], preferred_element_type=F32)
                u_raw[slot][r:r + dot_rows, :] = jnp.dot(h, wu_ref[...], preferred_element_type=F32)
        else:
            def epi(r):
                o_ref[rs(r), :] = (b_raw[prev][rs(r), :] * conv(g_raw[prev], cw_ref, r)).astype(o_ref.dtype)

            def mm(r):
                h = h_scr[r:r + dot_rows, :]
                g_raw[slot][r:r + dot_rows, :] = (jnp.dot(h, wc_ref[...], preferred_element_type=F32)
                                                  * jnp.dot(h, wv_ref[...], preferred_element_type=F32))
                lo, hi = max(r, HALO), min(r + dot_rows, HALO + tm)
                b_raw[slot][lo - HALO:hi - HALO, :] = jnp.dot(h_scr[lo:hi, :], wb_ref[...],
                                                              preferred_element_type=F32)
        epis = list(range(0, tm, EPI_ROWS))
        mms = list(range(0, rows, dot_rows))
        done = 0
        for k, r in enumerate(mms):
            upto = -(-len(epis) * (k + 1) // len(mms))
            for e in epis[done:upto]:
                epi(e)
            done = upto
            mm(r)

    for parity in (0, 1):
        pl.when(t % 2 == parity)(functools.partial(step, parity))


def _up(x, mod, layer, ctx, nw, w, cw, mode, tm, tn):
    bn, s, _ = x.shape
    parts = 2 if mode == "ffn" else 3
    width = w.shape[1] // parts
    nj = width // tn
    ni = s // tm
    total = bn * ni * nj
    hb = tm // HALO
    last_hb = s // HALO - 1

    def split(t):
        return t // (ni * nj), (t // nj) % ni, t % nj

    cur = lambda t: split(jnp.minimum(t, total - 1))
    prev = lambda t: split(jnp.maximum(t - 1, 0))
    mod_row = (lambda t: MOD_ROWS // 2) if ctx else (lambda t: cur(t)[0])
    mspec = lambda part: pl.BlockSpec((None, None, None, 1, D_MODEL),
                                      lambda t: (layer, mod_row(t), part, 0, 0))
    in_specs = [
        pl.BlockSpec((None, tm, D_MODEL), lambda t: (cur(t)[0], cur(t)[1], 0)),
        pl.BlockSpec((None, HALO, D_MODEL), lambda t: (cur(t)[0], jnp.maximum(cur(t)[1] * hb - 1, 0), 0)),
        pl.BlockSpec((None, HALO, D_MODEL),
                     lambda t: (cur(t)[0], jnp.minimum((cur(t)[1] + 1) * hb, last_hb), 0)),
        mspec(3 if mode == "ffn" else 0),
        mspec(4 if mode == "ffn" else 1),
        _const_spec((1, D_MODEL)),
    ]
    args = [x, x, x, mod, mod, nw]
    wspec = lambda p: pl.BlockSpec((D_MODEL, tn), lambda t: (0, p * nj + cur(t)[2]))
    cspec = lambda p: pl.BlockSpec((3, tn), lambda t: (0, p * nj + prev(t)[2]))
    rows = tm + 2 * HALO
    scratch = [pltpu.VMEM((rows, D_MODEL), BF16)] + [pltpu.VMEM((rows, tn), F32)] * 2
    if mode == "ffn":
        in_specs += [wspec(0), wspec(1), cspec(0), cspec(1)]
        args += [w, w, cw, cw]
        scratch += [pltpu.VMEM((rows, tn), F32)] * 2
    else:
        in_specs += [wspec(0), wspec(1), wspec(2), cspec(0)]
        args += [w, w, w, cw]
        scratch += [pltpu.VMEM((tm, tn), F32)] * 2
    return pl.pallas_call(
        functools.partial(_up_body, mode=mode, tm=tm, ni=ni, nj=nj, total=total),
        grid=(total + 1,),
        in_specs=in_specs,
        out_specs=pl.BlockSpec((None, tm, tn), lambda t: prev(t)),
        out_shape=jax.ShapeDtypeStruct((bn, s, width), BF16),
        scratch_shapes=scratch,
        compiler_params=_cparams(("arbitrary",)),
        name=mode + "_up" + ("_ctx" if ctx else ""),
    )(*args)


def _rope_tables(seq, rot_dim):
    rows = seq // GRID_W
    r = jnp.repeat(jnp.arange(rows, dtype=F32), GRID_W)
    col = jnp.tile(jnp.arange(GRID_W, dtype=F32), rows)
    quarter = rot_dim // 4
    inv = ROPE_THETA ** (-jnp.arange(quarter, dtype=F32) / quarter)
    ar = r[:, None] * inv
    ac = col[:, None] * inv
    ang = jnp.concatenate([ar, ar, ac, ac], axis=-1)
    reps = 128 // rot_dim
    cos = jnp.tile(jnp.cos(ang), (1, reps))
    sin = jnp.tile(jnp.sin(ang), (1, reps))
    first = ((jnp.arange(128) // quarter) % 2 == 0)[None, :]
    return cos, jnp.where(first, -sin, 0.0), jnp.where(first, 0.0, sin)


def _attn_weights(w_in, qn, kn, mqn, mkvn, w_uq, w_ukv):
    w_in = jnp.pad(w_in, ((0, 0), (0, W_IN_PAD - w_in.shape[1]))).astype(BF16)
    uq = w_uq.reshape(B_Q_RANK, B_HEADS, B_NOPE_DIM + B_ROPE_DIM)
    uq_rope = jnp.pad(uq[:, :, B_NOPE_DIM:], ((0, 0), (0, 0), (0, 128 - B_ROPE_DIM)))
    w_uq = jnp.concatenate([uq[:, :, :B_NOPE_DIM].reshape(B_Q_RANK, -1),
                            uq_rope.reshape(B_Q_RANK, -1)], axis=1).astype(BF16)
    ukv = w_ukv.reshape(B_KV_RANK, B_HEADS, B_NOPE_DIM + B_V_DIM)
    w_ukv = jnp.concatenate([ukv[:, :, :B_NOPE_DIM].reshape(B_KV_RANK, -1),
                             ukv[:, :, B_NOPE_DIM:].reshape(B_KV_RANK, -1)], axis=1).astype(BF16)
    return (w_in, qn.reshape(1, -1), kn.reshape(1, -1), mqn.reshape(1, -1), mkvn.reshape(1, -1),
            w_uq, w_ukv)


def kernel(x, c, ctx, c_ctx, w_ada, b_ada, norm_mix, norm_ffn, attn_w_in, attn_q_norm, attn_k_norm,
           mla_q_norm, mla_kv_norm, mla_w_uq, mla_w_ukv, attn_w_o, sc_w_in, sc_conv, sc_w_out,
           ffn_w_up, ffn_conv, ffn_w_down, final_norm):
    bn, s, _ = x.shape
    cl = ctx.shape[1]
    assert bn <= MOD_ROWS // 2
    cvec = jnp.zeros((MOD_ROWS, D_MODEL), F32).at[:bn].set(c).at[MOD_ROWS // 2].set(c_ctx)
    mod = _adaln(cvec, w_ada, b_ada)
    rope = _rope_tables(s, A_HEAD_DIM) + _rope_tables(s, B_ROPE_DIM)

    xc = ctx
    for l in range(DEPTH):
        later_attn = any(j % 2 == 0 for j in range(l + 1, DEPTH))
        i = l // 2
        nw = norm_mix[l].reshape(1, -1)
        if l % 2 == 0:
            wts = _attn_weights(attn_w_in[i], attn_q_norm[i], attn_k_norm[i], mla_q_norm[i],
                                mla_kv_norm[i], mla_w_uq[i], mla_w_ukv[i])
            w_o = attn_w_o[i].astype(BF16)
            w_oa, w_ob = w_o[:A_HEADS * A_HEAD_DIM], w_o[A_HEADS * A_HEAD_DIM:]
            qa_c, qb_c, ka_c, va_c, kb_c, vb_c = _attn_proj(xc, mod, l, True, nw, wts, None, later_attn, cl)
            qa, qb, ka, va, kb, vb = _attn_proj(x, mod, l, False, nw, wts, rope, True, 256)
            oa = _flash(qa, ka_c, va_c, ka, va, group=A_GROUP, dq=A_HEAD_DIM, dv=A_HEAD_DIM,
                        tq=256, tk=512, name="gqa")
            ob = _flash(qb, kb_c, vb_c, kb, vb, group=1, dq=B_QK_PAD, dv=B_V_DIM,
                        tq=1024, tk=512, name="mla")
            x = _proj_res(x, mod, l, 2, False, [oa, ob], [w_oa, w_ob], 512, name="attn_out")
            if later_attn:
                oa_c = _flash(qa_c, ka_c, va_c, None, None, group=A_GROUP, dq=A_HEAD_DIM, dv=A_HEAD_DIM,
                              tq=cl, tk=0, name="gqa_ctx")
                ob_c = _flash(qb_c, kb_c, vb_c, None, None, group=1, dq=B_QK_PAD, dv=B_V_DIM,
                              tq=cl, tk=0, name="mla_ctx")
                xc = _proj_res(xc, mod, l, 2, True, [oa_c, ob_c], [w_oa, w_ob], cl, name="attn_out")
        else:
            w_in = sc_w_in[i].astype(BF16)
            w_out = sc_w_out[i].astype(BF16)
            z = _up(x, mod, l, False, nw, w_in, sc_conv[i], "sc", 1024, 512)
            x = _proj_res(x, mod, l, 2, False, [z], [w_out], 512, name="sc_out")
            if later_attn:
                zc = _up(xc, mod, l, True, nw, w_in, sc_conv[i], "sc", cl, 512)
                xc = _proj_res(xc, mod, l, 2, True, [zc], [w_out], cl, name="sc_out")
        nwf = norm_ffn[l].reshape(1, -1)
        w_up = ffn_w_up[l].astype(BF16)
        w_down = ffn_w_down[l].astype(BF16)
        hid = _up(x, mod, l, False, nwf, w_up, ffn_conv[l], "ffn", 1024, 512)
        fw = final_norm.reshape(1, -1) if l == DEPTH - 1 else None
        x = _proj_res(x, mod, l, 5, False, [hid], [w_down], 256, final_w=fw, name="ffn_down")
        if later_attn:
            hid_c = _up(xc, mod, l, True, nwf, w_up, ffn_conv[l], "ffn", cl, 512)
            xc = _proj_res(xc, mod, l, 5, True, [hid_c], [w_down], cl, name="ffn_down")
    return x
```

```python
import functools

import jax
import jax.numpy as jnp
from jax import lax
from jax.experimental import pallas as pl
from jax.experimental.pallas import tpu as pltpu

F32 = jnp.float32
BF16 = jnp.bfloat16

D_MODEL = 2048
DEPTH = 4
GRID_W = 64
ROPE_THETA = 10000.0
EPS = 1e-6
A_HEADS = 8
A_KV_HEADS = 2
A_GROUP = A_HEADS // A_KV_HEADS
A_HEAD_DIM = 128
B_HEADS = 8
B_Q_RANK = 512
B_KV_RANK = 256
B_NOPE_DIM = 128
B_ROPE_DIM = 64
B_V_DIM = 128
B_QK_PAD = 256
LOG2E = 1.4426950408889634
A_SCALE = A_HEAD_DIM ** -0.5 * LOG2E
B_SCALE = (B_NOPE_DIM + B_ROPE_DIM) ** -0.5 * LOG2E
QGROUP = 256
SC_WIDTH = D_MODEL
D_FF = 256 * ((8 * D_MODEL // 3 + 255) // 256)
W_IN_PAD = 2432
MOD_ROWS = 16
HALO = 16
EPI_ROWS = 32
VMEM_LIMIT = 56 * 1024 * 1024


def _cparams(sem):
    return pltpu.CompilerParams(dimension_semantics=sem, vmem_limit_bytes=VMEM_LIMIT)


def _const_spec(shape):
    nd = len(shape)
    return pl.BlockSpec(shape, lambda *g: (0,) * nd, pipeline_mode=pl.Buffered(1))


def _mod_spec(layer, part, ctx):
    if ctx:
        idx = lambda b, *g: (layer, MOD_ROWS // 2, part, 0, 0)
    else:
        idx = lambda b, *g: (layer, b, part, 0, 0)
    return pl.BlockSpec((None, None, None, 1, D_MODEL), idx)


def _rms(x):
    return x * lax.rsqrt(jnp.mean(x * x, axis=-1, keepdims=True) + EPS)


def _norm_mod(x, nw, sc, sh):
    return (_rms(x) * nw) * (1.0 + sc) + sh


def _rope(x, cos, sin_up, sin_dn, quarter):
    n = x.shape[-1]
    return x * cos + pltpu.roll(x, n - quarter, 1) * sin_up + pltpu.roll(x, quarter, 1) * sin_dn


def _adaln_body(c_ref, w_ref, b_ref, o_ref):
    c = c_ref[...]
    s = (c / (1.0 + jnp.exp(-c))).astype(BF16)
    o_ref[...] = jnp.dot(s, w_ref[...].astype(BF16), preferred_element_type=F32) + b_ref[...]


def _adaln(cvec, w_ada, b_ada):
    tn = 1024
    n = 6 * D_MODEL
    out = pl.pallas_call(
        _adaln_body,
        grid=(DEPTH, n // tn),
        in_specs=[
            pl.BlockSpec((MOD_ROWS, D_MODEL), lambda l, j: (0, 0)),
            pl.BlockSpec((None, D_MODEL, tn), lambda l, j: (l, 0, j)),
            pl.BlockSpec((None, 1, tn), lambda l, j: (l, 0, j)),
        ],
        out_specs=pl.BlockSpec((None, MOD_ROWS, tn), lambda l, j: (l, 0, j)),
        out_shape=jax.ShapeDtypeStruct((DEPTH, MOD_ROWS, n), F32),
        compiler_params=_cparams(("parallel", "parallel")),
        name="adaln",
    )(cvec, w_ada, b_ada.reshape(DEPTH, 1, n))
    return out.reshape(DEPTH, MOD_ROWS, 6, 1, D_MODEL)


def _attn_proj_body(*refs, use_rope, with_q):
    it = iter(refs)
    x_ref, sh_ref, sc_ref, nw_ref, win_ref = (next(it) for _ in range(5))
    qn_ref, kn_ref, mqn_ref, mkvn_ref, wuq_ref, wukv_ref = (next(it) for _ in range(6))
    if use_rope:
        ca, sau, sad, cb, sbu, sbd = (next(it)[...] for _ in range(6))
    if with_q:
        qa_ref, qb_ref = next(it), next(it)
    ka_ref, va_ref, kb_ref, vb_ref = (next(it) for _ in range(4))

    def rope_a(v):
        return _rope(v, ca, sau, sad, A_HEAD_DIM // 4) if use_rope else v

    def rope_b(v):
        return _rope(v, cb, sbu, sbd, B_ROPE_DIM // 4) if use_rope else v

    h = _norm_mod(x_ref[...], nw_ref[...], sc_ref[...], sh_ref[...]).astype(BF16)
    proj = jnp.dot(h, win_ref[...], preferred_element_type=F32)
    q_end = A_HEADS * A_HEAD_DIM
    k_end = q_end + A_KV_HEADS * A_HEAD_DIM
    v_end = k_end + A_KV_HEADS * A_HEAD_DIM
    cq_end = v_end + B_Q_RANK
    ckv_end = cq_end + B_KV_RANK

    kn = kn_ref[...]
    for kh in range(A_KV_HEADS):
        lo = q_end + kh * A_HEAD_DIM
        k = rope_a(_rms(proj[:, lo:lo + A_HEAD_DIM]) * kn)
        ka_ref[:, kh * A_HEAD_DIM:(kh + 1) * A_HEAD_DIM] = k.astype(BF16)
    va_ref[...] = proj[:, k_end:v_end].T.astype(BF16)

    ckv = (_rms(proj[:, cq_end:ckv_end]) * mkvn_ref[...]).astype(BF16)
    kv = jnp.dot(ckv, wukv_ref[...], preferred_element_type=F32)
    nope_w = B_HEADS * B_NOPE_DIM
    vb_ref[...] = kv[:, nope_w:].T.astype(BF16)
    kr = rope_b(proj[:, ckv_end:W_IN_PAD]).astype(BF16)
    for hh in range(B_HEADS):
        lo = hh * B_QK_PAD
        kb_ref[:, lo:lo + B_NOPE_DIM] = kv[:, hh * B_NOPE_DIM:(hh + 1) * B_NOPE_DIM].astype(BF16)
        kb_ref[:, lo + B_NOPE_DIM:lo + B_QK_PAD] = kr

    if with_q:
        qn = qn_ref[...]
        for hh in range(A_HEADS):
            lo = hh * A_HEAD_DIM
            q = rope_a(_rms(proj[:, lo:lo + A_HEAD_DIM]) * qn) * A_SCALE
            qa_ref[:, lo:lo + A_HEAD_DIM] = q.astype(BF16)
        cq = (_rms(proj[:, v_end:cq_end]) * mqn_ref[...]).astype(BF16)
        qb = jnp.dot(cq, wuq_ref[...], preferred_element_type=F32)
        for hh in range(B_HEADS):
            lo = hh * B_QK_PAD
            qb_ref[:, lo:lo + B_NOPE_DIM] = (qb[:, hh * B_NOPE_DIM:(hh + 1) * B_NOPE_DIM] * B_SCALE).astype(BF16)
            r = rope_b(qb[:, nope_w + hh * 128:nope_w + (hh + 1) * 128]) * B_SCALE
            qb_ref[:, lo + B_NOPE_DIM:lo + B_QK_PAD] = r.astype(BF16)


def _attn_proj(x, mod, layer, ctx, nw, wts, rope, with_q, tm):
    bn, s, _ = x.shape
    use_rope = rope is not None
    w_in, qn, kn, mqn, mkvn, w_uq, w_ukv = wts
    row = lambda w: pl.BlockSpec((None, tm, w), lambda b, i: (b, i, 0))
    in_specs = [row(D_MODEL), _mod_spec(layer, 0, ctx), _mod_spec(layer, 1, ctx),
                _const_spec((1, D_MODEL)), _const_spec(w_in.shape),
                _const_spec(qn.shape), _const_spec(kn.shape), _const_spec(mqn.shape),
                _const_spec(mkvn.shape), _const_spec(w_uq.shape), _const_spec(w_ukv.shape)]
    args = [x, mod, mod, nw, w_in, qn, kn, mqn, mkvn, w_uq, w_ukv]
    if use_rope:
        in_specs += [pl.BlockSpec((tm, 128), lambda b, i: (i, 0))] * 6
        args += list(rope)
    col = lambda w: pl.BlockSpec((None, w, tm), lambda b, i: (b, 0, i))
    outs_desc = ([(A_HEADS * A_HEAD_DIM, False), (B_HEADS * B_QK_PAD, False)] if with_q else []) + [
        (A_KV_HEADS * A_HEAD_DIM, False), (A_KV_HEADS * A_HEAD_DIM, True),
        (B_HEADS * B_QK_PAD, False), (B_HEADS * B_V_DIM, True)]
    outs = pl.pallas_call(
        functools.partial(_attn_proj_body, use_rope=use_rope, with_q=with_q),
        grid=(bn, s // tm),
        in_specs=in_specs,
        out_specs=[col(w) if t else row(w) for w, t in outs_desc],
        out_shape=[jax.ShapeDtypeStruct((bn, w, s) if t else (bn, s, w), BF16) for w, t in outs_desc],
        compiler_params=_cparams(("parallel", "parallel")),
        name="attn_proj_ctx" if ctx else "attn_proj",
    )(*args)
    return outs if with_q else [None, None] + list(outs)


def _col_reduce(x, op, final):
    while x.shape[0] > 8 and x.shape[0] % 16 == 0:
        h = x.shape[0] // 2
        x = op(x[:h], x[h:])
    return final(x, axis=0, keepdims=True)


def _flash_body(*refs, group, dq, dv, tk, n_lat, depth=4):
    if n_lat:
        q_ref, kc_ref, vct_ref, k_ref, vt_ref, o_ref = refs
    else:
        q_ref, kc_ref, vct_ref, o_ref = refs
    tq = q_ref.shape[0]
    if group > 1:
        assert tq == QGROUP
        q_parts = [(slice(None), slice(g * dq, (g + 1) * dq)) for g in range(group)]
        o_parts = [(slice(None), slice(g * dv, (g + 1) * dv)) for g in range(group)]
    else:
        q_parts = [(slice(g * QGROUP, (g + 1) * QGROUP), slice(None)) for g in range(tq // QGROUP)]
        o_parts = q_parts
    n = len(q_parts)
    chunks = [(kc_ref, vct_ref, 0, kc_ref.shape[0])] + [(k_ref, vt_ref, c * tk, tk) for c in range(n_lat)]
    m, l, acc = [None] * n, [None] * n, [None] * n
    blocks = [(ci, g) for ci in range(len(chunks)) for g in range(n)]

    def scores(ci, g):
        kr, _, off, size = chunks[ci]
        return lax.dot_general(kr[off:off + size, :], q_ref[q_parts[g]], (((1,), (1,)), ((), ())),
                               preferred_element_type=F32)

    s_next = [scores(*blocks[d]) for d in range(min(depth, len(blocks)))]
    for idx, (ci, g) in enumerate(blocks):
        s = s_next.pop(0)
        if idx + depth < len(blocks):
            s_next.append(scores(*blocks[idx + depth]))
        _, vr, off, size = chunks[ci]
        vt = vr[:, off:off + size]
        smax = _col_reduce(s, jnp.maximum, jnp.max)
        if ci == 0:
            m_new = smax
            p = jnp.exp2(s - m_new)
            l[g] = _col_reduce(p, jnp.add, jnp.sum)
            acc[g] = jnp.dot(vt, p.astype(BF16), preferred_element_type=F32)
        else:
            m_new = jnp.maximum(m[g], smax)
            alpha = jnp.exp2(m[g] - m_new)
            p = jnp.exp2(s - m_new)
            l[g] = alpha * l[g] + _col_reduce(p, jnp.add, jnp.sum)
            acc[g] = alpha * acc[g] + jnp.dot(vt, p.astype(BF16), preferred_element_type=F32)
        m[g] = m_new
    for g in range(n):
        o_ref[o_parts[g]] = (acc[g] / l[g]).T.astype(o_ref.dtype)


def _flash(q, kc, vct, k, vt, *, group, dq, dv, tq, tk, name, depth=4):
    bn, s, qw = q.shape
    kvh = qw // (group * dq)
    c = kc.shape[1]
    n_lat = 0 if k is None else k.shape[1] // tk
    in_specs = [pl.BlockSpec((None, tq, group * dq), lambda b, h, i: (b, i, h)),
                pl.BlockSpec((None, c, dq), lambda b, h, i: (b, 0, h)),
                pl.BlockSpec((None, dv, c), lambda b, h, i: (b, h, 0))]
    args = [q, kc, vct]
    if n_lat:
        t = k.shape[1]
        in_specs += [pl.BlockSpec((None, t, dq), lambda b, h, i: (b, 0, h)),
                     pl.BlockSpec((None, dv, t), lambda b, h, i: (b, h, 0))]
        args += [k, vt]
    return pl.pallas_call(
        functools.partial(_flash_body, group=group, dq=dq, dv=dv, tk=tk, n_lat=n_lat, depth=depth),
        grid=(bn, kvh, s // tq),
        in_specs=in_specs,
        out_specs=pl.BlockSpec((None, tq, group * dv), lambda b, h, i: (b, i, h)),
        out_shape=jax.ShapeDtypeStruct((bn, s, kvh * group * dv), BF16),
        compiler_params=_cparams(("parallel", "parallel", "parallel")),
        name=name,
    )(*args)


def _proj_res_body(*refs, n_lhs, final):
    x_ref, g_ref = refs[0], refs[1]
    lhs = refs[2:2 + n_lhs]
    ws = refs[2 + n_lhs:2 + 2 * n_lhs]
    rest = refs[2 + 2 * n_lhs:]
    y = jnp.dot(lhs[0][...], ws[0][...], preferred_element_type=F32)
    for a, w in zip(lhs[1:], ws[1:]):
        y = y + jnp.dot(a[...], w[...], preferred_element_type=F32)
    out = x_ref[...] + g_ref[...] * y
    if final:
        fw_ref, o_ref = rest
        out = _rms(out) * fw_ref[...]
    else:
        (o_ref,) = rest
    o_ref[...] = out


def _proj_res(x, mod, layer, part, ctx, lhs, ws, tm, final_w=None, name="proj_res"):
    bn, s, _ = x.shape
    row = lambda w: pl.BlockSpec((None, tm, w), lambda b, i: (b, i, 0))
    in_specs = [row(D_MODEL), _mod_spec(layer, part, ctx)]
    in_specs += [row(a.shape[-1]) for a in lhs] + [_const_spec(w.shape) for w in ws]
    args = [x, mod] + list(lhs) + list(ws)
    if final_w is not None:
        in_specs.append(_const_spec((1, D_MODEL)))
        args.append(final_w)
    return pl.pallas_call(
        functools.partial(_proj_res_body, n_lhs=len(lhs), final=final_w is not None),
        grid=(bn, s // tm),
        in_specs=in_specs,
        out_specs=row(D_MODEL),
        out_shape=jax.ShapeDtypeStruct(x.shape, F32),
        compiler_params=_cparams(("parallel", "parallel")),
        name=name + ("_ctx" if ctx else ""),
    )(*args)


def _up_body(*refs, mode, tm, ni, nj, total, max_dot_rows):
    x_ref, xp_ref, xn_ref, sh_ref, sc_ref, nw_ref = refs[:6]
    t = pl.program_id(0)
    cur = jnp.minimum(t, total - 1)
    i = (cur // nj) % ni
    j = cur % nj
    if mode == "ffn":
        wg_ref, wu_ref, cg_ref, cu_ref, o_ref, h_scr, g0, g1, u0, u1 = refs[6:]
        g_raw, u_raw = (g0, g1), (u0, u1)
        second = u_raw
    else:
        wb_ref, wc_ref, wv_ref, cw_ref, o_ref, h_scr, g0, g1, b0, b1 = refs[6:]
        g_raw, b_raw = (g0, g1), (b0, b1)
        second = b_raw

    @pl.when(t == 0)
    def _():
        for r in (g_raw[1], second[1]):
            r[...] = jnp.zeros(r.shape, F32)

    @pl.when(j == 0)
    def _():
        gain = nw_ref[...] * (1.0 + sc_ref[...])
        sh = sh_ref[...]
        for r in range(0, tm, EPI_ROWS):
            h_scr[HALO + r:HALO + r + EPI_ROWS, :] = (
                _rms(x_ref[r:r + EPI_ROWS, :]) * gain + sh).astype(BF16)
        hp = _rms(xp_ref[...]) * gain + sh
        h_scr[0:HALO, :] = jnp.where(i == 0, 0.0, hp).astype(BF16)
        hn = _rms(xn_ref[...]) * gain + sh
        h_scr[HALO + tm:2 * HALO + tm, :] = jnp.where(i == ni - 1, 0.0, hn).astype(BF16)

    def conv(scr, cw_ref, r):
        return (cw_ref[0:1, :] * scr[HALO - 1 + r:HALO - 1 + r + EPI_ROWS, :]
                + cw_ref[1:2, :] * scr[HALO + r:HALO + r + EPI_ROWS, :]
                + cw_ref[2:3, :] * scr[HALO + 1 + r:HALO + 1 + r + EPI_ROWS, :])

    rows = tm + 2 * HALO
    dot_rows = max(d for d in range(16, max_dot_rows + 1, 16) if rows % d == 0)

    def step(slot):
        prev = 1 - slot
        rs = lambda r: slice(r, r + EPI_ROWS)
        if mode == "ffn":
            def epi(r):
                g = conv(g_raw[prev], cg_ref, r)
                u = conv(u_raw[prev], cu_ref, r)
                o_ref[rs(r), :] = ((g / (1.0 + jnp.exp(-g))) * u).astype(o_ref.dtype)

            def mm(r):
                h = h_scr[r:r + dot_rows, :]
                g_raw[slot][r:r + dot_rows, :] = jnp.dot(h, wg_ref[...], preferred_element_type=F32)
                u_raw[slot][r:r + dot_rows, :] = jnp.dot(h, wu_ref[...], preferred_element_type=F32)
        else:
            def epi(r):
                o_ref[rs(r), :] = (b_raw[prev][rs(r), :] * conv(g_raw[prev], cw_ref, r)).astype(o_ref.dtype)

            def mm(r):
                h = h_scr[r:r + dot_rows, :]
                g_raw[slot][r:r + dot_rows, :] = (jnp.dot(h, wc_ref[...], preferred_element_type=F32)
                                                  * jnp.dot(h, wv_ref[...], preferred_element_type=F32))
                lo, hi = max(r, HALO), min(r + dot_rows, HALO + tm)
                b_raw[slot][lo - HALO:hi - HALO, :] = jnp.dot(h_scr[lo:hi, :], wb_ref[...],
                                                              preferred_element_type=F32)
        epis = list(range(0, tm, EPI_ROWS))
        mms = list(range(0, rows, dot_rows))
        done = 0
        for k, r in enumerate(mms):
            upto = -(-len(epis) * (k + 1) // len(mms))
            for e in epis[done:upto]:
                epi(e)
            done = upto
            mm(r)

    for parity in (0, 1):
        pl.when(t % 2 == parity)(functools.partial(step, parity))


def _up(x, mod, layer, ctx, nw, w, cw, mode, tm, tn, max_dot_rows=528):
    bn, s, _ = x.shape
    parts = 2 if mode == "ffn" else 3
    width = w.shape[1] // parts
    nj = width // tn
    ni = s // tm
    total = bn * ni * nj
    hb = tm // HALO
    last_hb = s // HALO - 1

    def split(t):
        return t // (ni * nj), (t // nj) % ni, t % nj

    cur = lambda t: split(jnp.minimum(t, total - 1))
    prev = lambda t: split(jnp.maximum(t - 1, 0))
    mod_row = (lambda t: MOD_ROWS // 2) if ctx else (lambda t: cur(t)[0])
    mspec = lambda part: pl.BlockSpec((None, None, None, 1, D_MODEL),
                                      lambda t: (layer, mod_row(t), part, 0, 0))
    in_specs = [
        pl.BlockSpec((None, tm, D_MODEL), lambda t: (cur(t)[0], cur(t)[1], 0)),
        pl.BlockSpec((None, HALO, D_MODEL), lambda t: (cur(t)[0], jnp.maximum(cur(t)[1] * hb - 1, 0), 0)),
        pl.BlockSpec((None, HALO, D_MODEL),
                     lambda t: (cur(t)[0], jnp.minimum((cur(t)[1] + 1) * hb, last_hb), 0)),
        mspec(3 if mode == "ffn" else 0),
        mspec(4 if mode == "ffn" else 1),
        _const_spec((1, D_MODEL)),
    ]
    args = [x, x, x, mod, mod, nw]
    wspec = lambda p: pl.BlockSpec((D_MODEL, tn), lambda t: (0, p * nj + cur(t)[2]))
    cspec = lambda p: pl.BlockSpec((3, tn), lambda t: (0, p * nj + prev(t)[2]))
    rows = tm + 2 * HALO
    scratch = [pltpu.VMEM((rows, D_MODEL), BF16)] + [pltpu.VMEM((rows, tn), F32)] * 2
    if mode == "ffn":
        in_specs += [wspec(0), wspec(1), cspec(0), cspec(1)]
        args += [w, w, cw, cw]
        scratch += [pltpu.VMEM((rows, tn), F32)] * 2
    else:
        in_specs += [wspec(0), wspec(1), wspec(2), cspec(0)]
        args += [w, w, w, cw]
        scratch += [pltpu.VMEM((tm, tn), F32)] * 2
    return pl.pallas_call(
        functools.partial(_up_body, mode=mode, tm=tm, ni=ni, nj=nj, total=total,
                          max_dot_rows=max_dot_rows),
        grid=(total + 1,),
        in_specs=in_specs,
        out_specs=pl.BlockSpec((None, tm, tn), lambda t: prev(t)),
        out_shape=jax.ShapeDtypeStruct((bn, s, width), BF16),
        scratch_shapes=scratch,
        compiler_params=_cparams(("arbitrary",)),
        name=mode + "_up" + ("_ctx" if ctx else ""),
    )(*args)


def _rope_tables(seq, rot_dim):
    rows = seq // GRID_W
    r = jnp.repeat(jnp.arange(rows, dtype=F32), GRID_W)
    col = jnp.tile(jnp.arange(GRID_W, dtype=F32), rows)
    quarter = rot_dim // 4
    inv = ROPE_THETA ** (-jnp.arange(quarter, dtype=F32) / quarter)
    ar = r[:, None] * inv
    ac = col[:, None] * inv
    ang = jnp.concatenate([ar, ar, ac, ac], axis=-1)
    reps = 128 // rot_dim
    cos = jnp.tile(jnp.cos(ang), (1, reps))
    sin = jnp.tile(jnp.sin(ang), (1, reps))
    first = ((jnp.arange(128) // quarter) % 2 == 0)[None, :]
    return cos, jnp.where(first, -sin, 0.0), jnp.where(first, 0.0, sin)


def _attn_weights(w_in, qn, kn, mqn, mkvn, w_uq, w_ukv):
    w_in = jnp.pad(w_in, ((0, 0), (0, W_IN_PAD - w_in.shape[1]))).astype(BF16)
    uq = w_uq.reshape(B_Q_RANK, B_HEADS, B_NOPE_DIM + B_ROPE_DIM)
    uq_rope = jnp.pad(uq[:, :, B_NOPE_DIM:], ((0, 0), (0, 0), (0, 128 - B_ROPE_DIM)))
    w_uq = jnp.concatenate([uq[:, :, :B_NOPE_DIM].reshape(B_Q_RANK, -1),
                            uq_rope.reshape(B_Q_RANK, -1)], axis=1).astype(BF16)
    ukv = w_ukv.reshape(B_KV_RANK, B_HEADS, B_NOPE_DIM + B_V_DIM)
    w_ukv = jnp.concatenate([ukv[:, :, :B_NOPE_DIM].reshape(B_KV_RANK, -1),
                             ukv[:, :, B_NOPE_DIM:].reshape(B_KV_RANK, -1)], axis=1).astype(BF16)
    return (w_in, qn.reshape(1, -1), kn.reshape(1, -1), mqn.reshape(1, -1), mkvn.reshape(1, -1),
            w_uq, w_ukv)


def kernel(x, c, ctx, c_ctx, w_ada, b_ada, norm_mix, norm_ffn, attn_w_in, attn_q_norm, attn_k_norm,
           mla_q_norm, mla_kv_norm, mla_w_uq, mla_w_ukv, attn_w_o, sc_w_in, sc_conv, sc_w_out,
           ffn_w_up, ffn_conv, ffn_w_down, final_norm):
    bn, s, _ = x.shape
    cl = ctx.shape[1]
    assert bn <= MOD_ROWS // 2
    cvec = jnp.zeros((MOD_ROWS, D_MODEL), F32).at[:bn].set(c).at[MOD_ROWS // 2].set(c_ctx)
    mod = _adaln(cvec, w_ada, b_ada)
    rope = _rope_tables(s, A_HEAD_DIM) + _rope_tables(s, B_ROPE_DIM)

    xc = ctx
    for l in range(DEPTH):
        later_attn = any(j % 2 == 0 for j in range(l + 1, DEPTH))
        i = l // 2
        nw = norm_mix[l].reshape(1, -1)
        if l % 2 == 0:
            wts = _attn_weights(attn_w_in[i], attn_q_norm[i], attn_k_norm[i], mla_q_norm[i],
                                mla_kv_norm[i], mla_w_uq[i], mla_w_ukv[i])
            w_o = attn_w_o[i].astype(BF16)
            w_oa, w_ob = w_o[:A_HEADS * A_HEAD_DIM], w_o[A_HEADS * A_HEAD_DIM:]
            qa_c, qb_c, ka_c, va_c, kb_c, vb_c = _attn_proj(xc, mod, l, True, nw, wts, None, later_attn, cl)
            qa, qb, ka, va, kb, vb = _attn_proj(x, mod, l, False, nw, wts, rope, True, 256)
            oa = _flash(qa, ka_c, va_c, ka, va, group=A_GROUP, dq=A_HEAD_DIM, dv=A_HEAD_DIM,
                        tq=256, tk=512, name="gqa", depth=(4, 6)[i])
            ob = _flash(qb, kb_c, vb_c, kb, vb, group=1, dq=B_QK_PAD, dv=B_V_DIM,
                        tq=1024, tk=(512, 1024)[i], name="mla")
            x = _proj_res(x, mod, l, 2, False, [oa, ob], [w_oa, w_ob], 512, name="attn_out")
            if later_attn:
                oa_c = _flash(qa_c, ka_c, va_c, None, None, group=A_GROUP, dq=A_HEAD_DIM, dv=A_HEAD_DIM,
                              tq=cl, tk=0, name="gqa_ctx")
                ob_c = _flash(qb_c, kb_c, vb_c, None, None, group=1, dq=B_QK_PAD, dv=B_V_DIM,
                              tq=cl, tk=0, name="mla_ctx")
                xc = _proj_res(xc, mod, l, 2, True, [oa_c, ob_c], [w_oa, w_ob], cl, name="attn_out")
        else:
            w_in = sc_w_in[i].astype(BF16)
            w_out = sc_w_out[i].astype(BF16)
            z = _up(x, mod, l, False, nw, w_in, sc_conv[i], "sc", 1024, 512, max_dot_rows=(528, 1056)[i])
            x = _proj_res(x, mod, l, 2, False, [z], [w_out], 512, name="sc_out")
            if later_attn:
                zc = _up(xc, mod, l, True, nw, w_in, sc_conv[i], "sc", cl, 512)
                xc = _proj_res(xc, mod, l, 2, True, [zc], [w_out], cl, name="sc_out")
        nwf = norm_ffn[l].reshape(1, -1)
        w_up = ffn_w_up[l].astype(BF16)
        w_down = ffn_w_down[l].astype(BF16)
        hid = _up(x, mod, l, False, nwf, w_up, ffn_conv[l], "ffn", 1024, 512,
                  max_dot_rows=(176, 352, 528, 1056)[l])
        fw = final_norm.reshape(1, -1) if l == DEPTH - 1 else None
        x = _proj_res(x, mod, l, 5, False, [hid], [w_down], 256, final_w=fw, name="ffn_down")
        if later_attn:
            hid_c = _up(xc, mod, l, True, nwf, w_up, ffn_conv[l], "ffn", cl, 512)
            xc = _proj_res(xc, mod, l, 5, True, [hid_c], [w_down], cl, name="ffn_down")
    return x
```

```python
import functools

import jax
import jax.numpy as jnp
from jax import lax
from jax.experimental import pallas as pl
from jax.experimental.pallas import tpu as pltpu

F32 = jnp.float32
BF16 = jnp.bfloat16

D_MODEL = 2048
DEPTH = 4
GRID_W = 64
ROPE_THETA = 10000.0
EPS = 1e-6
A_HEADS = 8
A_KV_HEADS = 2
A_GROUP = A_HEADS // A_KV_HEADS
A_HEAD_DIM = 128
B_HEADS = 8
B_Q_RANK = 512
B_KV_RANK = 256
B_NOPE_DIM = 128
B_ROPE_DIM = 64
B_V_DIM = 128
B_QK_PAD = 256
LOG2E = 1.4426950408889634
A_SCALE = A_HEAD_DIM ** -0.5 * LOG2E
B_SCALE = (B_NOPE_DIM + B_ROPE_DIM) ** -0.5 * LOG2E
QGROUP = 256
SC_WIDTH = D_MODEL
D_FF = 256 * ((8 * D_MODEL // 3 + 255) // 256)
W_IN_PAD = 2432
MOD_ROWS = 16
HALO = 16
EPI_ROWS = 32
VMEM_LIMIT = 56 * 1024 * 1024


def _cparams(sem):
    return pltpu.CompilerParams(dimension_semantics=sem, vmem_limit_bytes=VMEM_LIMIT)


def _const_spec(shape):
    nd = len(shape)
    return pl.BlockSpec(shape, lambda *g: (0,) * nd, pipeline_mode=pl.Buffered(1))


def _mod_spec(layer, part, ctx):
    if ctx:
        idx = lambda b, *g: (layer, MOD_ROWS // 2, part, 0, 0)
    else:
        idx = lambda b, *g: (layer, b, part, 0, 0)
    return pl.BlockSpec((None, None, None, 1, D_MODEL), idx)


def _rms(x):
    return x * lax.rsqrt(jnp.mean(x * x, axis=-1, keepdims=True) + EPS)


def _norm_mod(x, nw, sc, sh):
    return (_rms(x) * nw) * (1.0 + sc) + sh


def _rope(x, cos, sin_up, sin_dn, quarter):
    n = x.shape[-1]
    return x * cos + pltpu.roll(x, n - quarter, 1) * sin_up + pltpu.roll(x, quarter, 1) * sin_dn


def _adaln_body(c_ref, w_ref, b_ref, o_ref):
    c = c_ref[...]
    s = (c / (1.0 + jnp.exp(-c))).astype(BF16)
    o_ref[...] = jnp.dot(s, w_ref[...].astype(BF16), preferred_element_type=F32) + b_ref[...]


def _adaln(cvec, w_ada, b_ada):
    tn = 1024
    n = 6 * D_MODEL
    out = pl.pallas_call(
        _adaln_body,
        grid=(DEPTH, n // tn),
        in_specs=[
            pl.BlockSpec((MOD_ROWS, D_MODEL), lambda l, j: (0, 0)),
            pl.BlockSpec((None, D_MODEL, tn), lambda l, j: (l, 0, j)),
            pl.BlockSpec((None, 1, tn), lambda l, j: (l, 0, j)),
        ],
        out_specs=pl.BlockSpec((None, MOD_ROWS, tn), lambda l, j: (l, 0, j)),
        out_shape=jax.ShapeDtypeStruct((DEPTH, MOD_ROWS, n), F32),
        compiler_params=_cparams(("parallel", "parallel")),
        name="adaln",
    )(cvec, w_ada, b_ada.reshape(DEPTH, 1, n))
    return out.reshape(DEPTH, MOD_ROWS, 6, 1, D_MODEL)


def _attn_proj_body(*refs, use_rope, with_q):
    it = iter(refs)
    x_ref, sh_ref, sc_ref, nw_ref, win_ref = (next(it) for _ in range(5))
    qn_ref, kn_ref, mqn_ref, mkvn_ref, wuq_ref, wukv_ref = (next(it) for _ in range(6))
    if use_rope:
        ca, sau, sad, cb, sbu, sbd = (next(it)[...] for _ in range(6))
    if with_q:
        qa_ref, qb_ref = next(it), next(it)
    ka_ref, va_ref, kb_ref, vb_ref = (next(it) for _ in range(4))

    def rope_a(v):
        return _rope(v, ca, sau, sad, A_HEAD_DIM // 4) if use_rope else v

    def rope_b(v):
        return _rope(v, cb, sbu, sbd, B_ROPE_DIM // 4) if use_rope else v

    h = _norm_mod(x_ref[...], nw_ref[...], sc_ref[...], sh_ref[...]).astype(BF16)
    proj = jnp.dot(h, win_ref[...], preferred_element_type=F32)
    q_end = A_HEADS * A_HEAD_DIM
    k_end = q_end + A_KV_HEADS * A_HEAD_DIM
    v_end = k_end + A_KV_HEADS * A_HEAD_DIM
    cq_end = v_end + B_Q_RANK
    ckv_end = cq_end + B_KV_RANK

    kn = kn_ref[...]
    for kh in range(A_KV_HEADS):
        lo = q_end + kh * A_HEAD_DIM
        k = rope_a(_rms(proj[:, lo:lo + A_HEAD_DIM]) * kn)
        ka_ref[:, kh * A_HEAD_DIM:(kh + 1) * A_HEAD_DIM] = k.astype(BF16)
    va_ref[...] = proj[:, k_end:v_end].T.astype(BF16)

    ckv = (_rms(proj[:, cq_end:ckv_end]) * mkvn_ref[...]).astype(BF16)
    kv = jnp.dot(ckv, wukv_ref[...], preferred_element_type=F32)
    nope_w = B_HEADS * B_NOPE_DIM
    vb_ref[...] = kv[:, nope_w:].T.astype(BF16)
    kr = rope_b(proj[:, ckv_end:W_IN_PAD]).astype(BF16)
    for hh in range(B_HEADS):
        lo = hh * B_QK_PAD
        kb_ref[:, lo:lo + B_NOPE_DIM] = kv[:, hh * B_NOPE_DIM:(hh + 1) * B_NOPE_DIM].astype(BF16)
        kb_ref[:, lo + B_NOPE_DIM:lo + B_QK_PAD] = kr

    if with_q:
        qn = qn_ref[...]
        for hh in range(A_HEADS):
            lo = hh * A_HEAD_DIM
            q = rope_a(_rms(proj[:, lo:lo + A_HEAD_DIM]) * qn) * A_SCALE
            qa_ref[:, lo:lo + A_HEAD_DIM] = q.astype(BF16)
        cq = (_rms(proj[:, v_end:cq_end]) * mqn_ref[...]).astype(BF16)
        qb = jnp.dot(cq, wuq_ref[...], preferred_element_type=F32)
        for hh in range(B_HEADS):
            lo = hh * B_QK_PAD
            qb_ref[:, lo:lo + B_NOPE_DIM] = (qb[:, hh * B_NOPE_DIM:(hh + 1) * B_NOPE_DIM] * B_SCALE).astype(BF16)
            r = rope_b(qb[:, nope_w + hh * 128:nope_w + (hh + 1) * 128]) * B_SCALE
            qb_ref[:, lo + B_NOPE_DIM:lo + B_QK_PAD] = r.astype(BF16)


def _attn_proj(x, mod, layer, ctx, nw, wts, rope, with_q, tm):
    bn, s, _ = x.shape
    use_rope = rope is not None
    w_in, qn, kn, mqn, mkvn, w_uq, w_ukv = wts
    row = lambda w: pl.BlockSpec((None, tm, w), lambda b, i: (b, i, 0))
    in_specs = [row(D_MODEL), _mod_spec(layer, 0, ctx), _mod_spec(layer, 1, ctx),
                _const_spec((1, D_MODEL)), _const_spec(w_in.shape),
                _const_spec(qn.shape), _const_spec(kn.shape), _const_spec(mqn.shape),
                _const_spec(mkvn.shape), _const_spec(w_uq.shape), _const_spec(w_ukv.shape)]
    args = [x, mod, mod, nw, w_in, qn, kn, mqn, mkvn, w_uq, w_ukv]
    if use_rope:
        in_specs += [pl.BlockSpec((tm, 128), lambda b, i: (i, 0))] * 6
        args += list(rope)
    col = lambda w: pl.BlockSpec((None, w, tm), lambda b, i: (b, 0, i))
    outs_desc = ([(A_HEADS * A_HEAD_DIM, False), (B_HEADS * B_QK_PAD, False)] if with_q else []) + [
        (A_KV_HEADS * A_HEAD_DIM, False), (A_KV_HEADS * A_HEAD_DIM, True),
        (B_HEADS * B_QK_PAD, False), (B_HEADS * B_V_DIM, True)]
    outs = pl.pallas_call(
        functools.partial(_attn_proj_body, use_rope=use_rope, with_q=with_q),
        grid=(bn, s // tm),
        in_specs=in_specs,
        out_specs=[col(w) if t else row(w) for w, t in outs_desc],
        out_shape=[jax.ShapeDtypeStruct((bn, w, s) if t else (bn, s, w), BF16) for w, t in outs_desc],
        compiler_params=_cparams(("parallel", "parallel")),
        name="attn_proj_ctx" if ctx else "attn_proj",
    )(*args)
    return outs if with_q else [None, None] + list(outs)


def _col_reduce(x, op, final):
    while x.shape[0] > 8 and x.shape[0] % 16 == 0:
        h = x.shape[0] // 2
        x = op(x[:h], x[h:])
    return final(x, axis=0, keepdims=True)


def _flash_body(*refs, group, dq, dv, tk, n_lat, depth=4):
    if n_lat:
        q_ref, kc_ref, vct_ref, k_ref, vt_ref, o_ref = refs
    else:
        q_ref, kc_ref, vct_ref, o_ref = refs
    tq = q_ref.shape[0]
    if group > 1:
        assert tq == QGROUP
        q_parts = [(slice(None), slice(g * dq, (g + 1) * dq)) for g in range(group)]
        o_parts = [(slice(None), slice(g * dv, (g + 1) * dv)) for g in range(group)]
    else:
        q_parts = [(slice(g * QGROUP, (g + 1) * QGROUP), slice(None)) for g in range(tq // QGROUP)]
        o_parts = q_parts
    n = len(q_parts)
    chunks = [(kc_ref, vct_ref, 0, kc_ref.shape[0])] + [(k_ref, vt_ref, c * tk, tk) for c in range(n_lat)]
    m, l, acc = [None] * n, [None] * n, [None] * n
    blocks = [(ci, g) for ci in range(len(chunks)) for g in range(n)]

    def scores(ci, g):
        kr, _, off, size = chunks[ci]
        return lax.dot_general(kr[off:off + size, :], q_ref[q_parts[g]], (((1,), (1,)), ((), ())),
                               preferred_element_type=F32)

    s_next = [scores(*blocks[d]) for d in range(min(depth, len(blocks)))]
    for idx, (ci, g) in enumerate(blocks):
        s = s_next.pop(0)
        if idx + depth < len(blocks):
            s_next.append(scores(*blocks[idx + depth]))
        _, vr, off, size = chunks[ci]
        vt = vr[:, off:off + size]
        smax = _col_reduce(s, jnp.maximum, jnp.max)
        if ci == 0:
            m_new = smax
            p = jnp.exp2(s - m_new)
            l[g] = _col_reduce(p, jnp.add, jnp.sum)
            acc[g] = jnp.dot(vt, p.astype(BF16), preferred_element_type=F32)
        else:
            m_new = jnp.maximum(m[g], smax)
            alpha = jnp.exp2(m[g] - m_new)
            p = jnp.exp2(s - m_new)
            l[g] = alpha * l[g] + _col_reduce(p, jnp.add, jnp.sum)
            acc[g] = alpha * acc[g] + jnp.dot(vt, p.astype(BF16), preferred_element_type=F32)
        m[g] = m_new
    for g in range(n):
        o_ref[o_parts[g]] = (acc[g] / l[g]).T.astype(o_ref.dtype)


def _flash(q, kc, vct, k, vt, *, group, dq, dv, tq, tk, name, depth=4):
    bn, s, qw = q.shape
    kvh = qw // (group * dq)
    c = kc.shape[1]
    n_lat = 0 if k is None else k.shape[1] // tk
    in_specs = [pl.BlockSpec((None, tq, group * dq), lambda b, h, i: (b, i, h)),
                pl.BlockSpec((None, c, dq), lambda b, h, i: (b, 0, h)),
                pl.BlockSpec((None, dv, c), lambda b, h, i: (b, h, 0))]
    args = [q, kc, vct]
    if n_lat:
        t = k.shape[1]
        in_specs += [pl.BlockSpec((None, t, dq), lambda b, h, i: (b, 0, h)),
                     pl.BlockSpec((None, dv, t), lambda b, h, i: (b, h, 0))]
        args += [k, vt]
    return pl.pallas_call(
        functools.partial(_flash_body, group=group, dq=dq, dv=dv, tk=tk, n_lat=n_lat, depth=depth),
        grid=(bn, kvh, s // tq),
        in_specs=in_specs,
        out_specs=pl.BlockSpec((None, tq, group * dv), lambda b, h, i: (b, i, h)),
        out_shape=jax.ShapeDtypeStruct((bn, s, kvh * group * dv), BF16),
        compiler_params=_cparams(("parallel", "parallel", "parallel")),
        name=name,
    )(*args)


def _proj_res_body(*refs, n_lhs, final):
    x_ref, g_ref = refs[0], refs[1]
    lhs = refs[2:2 + n_lhs]
    ws = refs[2 + n_lhs:2 + 2 * n_lhs]
    rest = refs[2 + 2 * n_lhs:]
    y = jnp.dot(lhs[0][...], ws[0][...], preferred_element_type=F32)
    for a, w in zip(lhs[1:], ws[1:]):
        y = y + jnp.dot(a[...], w[...], preferred_element_type=F32)
    out = x_ref[...] + g_ref[...] * y
    if final:
        fw_ref, o_ref = rest
        out = _rms(out) * fw_ref[...]
    else:
        (o_ref,) = rest
    o_ref[...] = out


def _proj_res(x, mod, layer, part, ctx, lhs, ws, tm, final_w=None, name="proj_res"):
    bn, s, _ = x.shape
    row = lambda w: pl.BlockSpec((None, tm, w), lambda b, i: (b, i, 0))
    in_specs = [row(D_MODEL), _mod_spec(layer, part, ctx)]
    in_specs += [row(a.shape[-1]) for a in lhs] + [_const_spec(w.shape) for w in ws]
    args = [x, mod] + list(lhs) + list(ws)
    if final_w is not None:
        in_specs.append(_const_spec((1, D_MODEL)))
        args.append(final_w)
    return pl.pallas_call(
        functools.partial(_proj_res_body, n_lhs=len(lhs), final=final_w is not None),
        grid=(bn, s // tm),
        in_specs=in_specs,
        out_specs=row(D_MODEL),
        out_shape=jax.ShapeDtypeStruct(x.shape, F32),
        compiler_params=_cparams(("parallel", "parallel")),
        name=name + ("_ctx" if ctx else ""),
    )(*args)


def _up_body(*refs, mode, tm, ni, nj, total, max_dot_rows):
    x_ref, xp_ref, xn_ref, sh_ref, sc_ref, nw_ref = refs[:6]
    t = pl.program_id(0)
    cur = jnp.minimum(t, total - 1)
    i = (cur // nj) % ni
    j = cur % nj
    if mode == "ffn":
        wg_ref, wu_ref, cg_ref, cu_ref, o_ref, h_scr, g0, g1, u0, u1 = refs[6:]
        g_raw, u_raw = (g0, g1), (u0, u1)
        second = u_raw
    else:
        wb_ref, wc_ref, wv_ref, cw_ref, o_ref, h_scr, g0, g1, b0, b1 = refs[6:]
        g_raw, b_raw = (g0, g1), (b0, b1)
        second = b_raw

    @pl.when(t == 0)
    def _():
        for r in (g_raw[1], second[1]):
            r[...] = jnp.zeros(r.shape, F32)

    @pl.when(j == 0)
    def _():
        gain = nw_ref[...] * (1.0 + sc_ref[...])
        sh = sh_ref[...]
        for r in range(0, tm, EPI_ROWS):
            h_scr[HALO + r:HALO + r + EPI_ROWS, :] = (
                _rms(x_ref[r:r + EPI_ROWS, :]) * gain + sh).astype(BF16)
        hp = _rms(xp_ref[...]) * gain + sh
        h_scr[0:HALO, :] = jnp.where(i == 0, 0.0, hp).astype(BF16)
        hn = _rms(xn_ref[...]) * gain + sh
        h_scr[HALO + tm:2 * HALO + tm, :] = jnp.where(i == ni - 1, 0.0, hn).astype(BF16)

    def conv(scr, cw_ref, r):
        return (cw_ref[0:1, :] * scr[HALO - 1 + r:HALO - 1 + r + EPI_ROWS, :]
                + cw_ref[1:2, :] * scr[HALO + r:HALO + r + EPI_ROWS, :]
                + cw_ref[2:3, :] * scr[HALO + 1 + r:HALO + 1 + r + EPI_ROWS, :])

    rows = tm + 2 * HALO
    dot_rows = max(d for d in range(16, max_dot_rows + 1, 16) if rows % d == 0)

    def step(slot):
        prev = 1 - slot
        rs = lambda r: slice(r, r + EPI_ROWS)
        if mode == "ffn":
            def epi(r):
                g = conv(g_raw[prev], cg_ref, r)
                u = conv(u_raw[prev], cu_ref, r)
                o_ref[rs(r), :] = ((g / (1.0 + jnp.exp(-g))) * u).astype(o_ref.dtype)

            def mm(r):
                h = h_scr[r:r + dot_rows, :]
                g_raw[slot][r:r + dot_rows, :] = jnp.dot(h, wg_ref[...], preferred_element_type=F32)
                u_raw[slot][r:r + dot_rows, :] = jnp.dot(h, wu_ref[...], preferred_element_type=F32)
        else:
            def epi(r):
                o_ref[rs(r), :] = (b_raw[prev][rs(r), :] * conv(g_raw[prev], cw_ref, r)).astype(o_ref.dtype)

            def mm(r):
                h = h_scr[r:r + dot_rows, :]
                g_raw[slot][r:r + dot_rows, :] = (jnp.dot(h, wc_ref[...], preferred_element_type=F32)
                                                  * jnp.dot(h, wv_ref[...], preferred_element_type=F32))
                lo, hi = max(r, HALO), min(r + dot_rows, HALO + tm)
                b_raw[slot][lo - HALO:hi - HALO, :] = jnp.dot(h_scr[lo:hi, :], wb_ref[...],
                                                              preferred_element_type=F32)
        epis = list(range(0, tm, EPI_ROWS))
        mms = list(range(0, rows, dot_rows))
        done = 0
        for k, r in enumerate(mms):
            upto = -(-len(epis) * (k + 1) // len(mms))
            for e in epis[done:upto]:
                epi(e)
            done = upto
            mm(r)

    for parity in (0, 1):
        pl.when(t % 2 == parity)(functools.partial(step, parity))


def _up(x, mod, layer, ctx, nw, w, cw, mode, tm, tn, max_dot_rows=528):
    bn, s, _ = x.shape
    parts = 2 if mode == "ffn" else 3
    width = w.shape[1] // parts
    nj = width // tn
    ni = s // tm
    total = bn * ni * nj
    hb = tm // HALO
    last_hb = s // HALO - 1

    def split(t):
        return t // (ni * nj), (t // nj) % ni, t % nj

    cur = lambda t: split(jnp.minimum(t, total - 1))
    prev = lambda t: split(jnp.maximum(t - 1, 0))
    mod_row = (lambda t: MOD_ROWS // 2) if ctx else (lambda t: cur(t)[0])
    mspec = lambda part: pl.BlockSpec((None, None, None, 1, D_MODEL),
                                      lambda t: (layer, mod_row(t), part, 0, 0))
    in_specs = [
        pl.BlockSpec((None, tm, D_MODEL), lambda t: (cur(t)[0], cur(t)[1], 0)),
        pl.BlockSpec((None, HALO, D_MODEL), lambda t: (cur(t)[0], jnp.maximum(cur(t)[1] * hb - 1, 0), 0)),
        pl.BlockSpec((None, HALO, D_MODEL),
                     lambda t: (cur(t)[0], jnp.minimum((cur(t)[1] + 1) * hb, last_hb), 0)),
        mspec(3 if mode == "ffn" else 0),
        mspec(4 if mode == "ffn" else 1),
        _const_spec((1, D_MODEL)),
    ]
    args = [x, x, x, mod, mod, nw]
    wspec = lambda p: pl.BlockSpec((D_MODEL, tn), lambda t: (0, p * nj + cur(t)[2]))
    cspec = lambda p: pl.BlockSpec((3, tn), lambda t: (0, p * nj + prev(t)[2]))
    rows = tm + 2 * HALO
    scratch = [pltpu.VMEM((rows, D_MODEL), BF16)] + [pltpu.VMEM((rows, tn), F32)] * 2
    if mode == "ffn":
        in_specs += [wspec(0), wspec(1), cspec(0), cspec(1)]
        args += [w, w, cw, cw]
        scratch += [pltpu.VMEM((rows, tn), F32)] * 2
    else:
        in_specs += [wspec(0), wspec(1), wspec(2), cspec(0)]
        args += [w, w, w, cw]
        scratch += [pltpu.VMEM((tm, tn), F32)] * 2
    return pl.pallas_call(
        functools.partial(_up_body, mode=mode, tm=tm, ni=ni, nj=nj, total=total,
                          max_dot_rows=max_dot_rows),
        grid=(total + 1,),
        in_specs=in_specs,
        out_specs=pl.BlockSpec((None, tm, tn), lambda t: prev(t)),
        out_shape=jax.ShapeDtypeStruct((bn, s, width), BF16),
        scratch_shapes=scratch,
        compiler_params=_cparams(("arbitrary",)),
        name=mode + "_up" + ("_ctx" if ctx else ""),
    )(*args)


def _up2_body(*refs, mode, tm, epi_rows):
    x_ref, xp_ref, xn_ref, sh_ref, sc_ref, nw_ref = refs[:6]
    i = pl.program_id(1)
    j = pl.program_id(2)
    wg_ref, wu_ref, cg_ref, cu_ref, o_ref, h_scr, g_scr, u_scr = refs[6:]

    @pl.when(j == 0)
    def _():
        gain = nw_ref[...] * (1.0 + sc_ref[...])
        sh = sh_ref[...]
        for r in range(0, tm, epi_rows):
            h_scr[HALO + r:HALO + r + epi_rows, :] = (
                _rms(x_ref[r:r + epi_rows, :]) * gain + sh).astype(BF16)
        hp = _rms(xp_ref[...]) * gain + sh
        h_scr[0:HALO, :] = jnp.where(i == 0, 0.0, hp).astype(BF16)
        hn = _rms(xn_ref[...]) * gain + sh
        h_scr[HALO + tm:2 * HALO + tm, :] = jnp.where(i == pl.num_programs(1) - 1, 0.0, hn).astype(BF16)

    def conv(scr, cw_ref, r):
        return (cw_ref[0:1, :] * scr[HALO - 1 + r:HALO - 1 + r + epi_rows, :]
                + cw_ref[1:2, :] * scr[HALO + r:HALO + r + epi_rows, :]
                + cw_ref[2:3, :] * scr[HALO + 1 + r:HALO + 1 + r + epi_rows, :])

    h = h_scr[...]
    g_scr[...] = jnp.dot(h, wg_ref[...], preferred_element_type=F32)
    u_scr[...] = jnp.dot(h, wu_ref[...], preferred_element_type=F32)
    for r in range(0, tm, epi_rows):
        g = conv(g_scr, cg_ref, r)
        u = conv(u_scr, cu_ref, r)
        o_ref[r:r + epi_rows, :] = ((g / (1.0 + jnp.exp(-g))) * u).astype(o_ref.dtype)


def _up2(x, mod, layer, ctx, nw, w, cw, mode, tm, tn, epi_rows):
    bn, s, _ = x.shape
    width = w.shape[1] // 2
    nj = width // tn
    hb = tm // HALO
    last_hb = s // HALO - 1
    in_specs = [
        pl.BlockSpec((None, tm, D_MODEL), lambda b, i, j: (b, i, 0)),
        pl.BlockSpec((None, HALO, D_MODEL), lambda b, i, j: (b, jnp.maximum(i * hb - 1, 0), 0)),
        pl.BlockSpec((None, HALO, D_MODEL), lambda b, i, j: (b, jnp.minimum((i + 1) * hb, last_hb), 0)),
        _mod_spec(layer, 3, ctx), _mod_spec(layer, 4, ctx), _const_spec((1, D_MODEL)),
    ]
    wspec = lambda p: pl.BlockSpec((D_MODEL, tn), lambda b, i, j: (0, p * nj + j))
    cspec = lambda p: pl.BlockSpec((3, tn), lambda b, i, j: (0, p * nj + j))
    in_specs += [wspec(0), wspec(1), cspec(0), cspec(1)]
    rows = tm + 2 * HALO
    return pl.pallas_call(
        functools.partial(_up2_body, mode=mode, tm=tm, epi_rows=epi_rows),
        grid=(bn, s // tm, nj),
        in_specs=in_specs,
        out_specs=pl.BlockSpec((None, tm, tn), lambda b, i, j: (b, i, j)),
        out_shape=jax.ShapeDtypeStruct((bn, s, width), BF16),
        scratch_shapes=[pltpu.VMEM((rows, D_MODEL), BF16)] + [pltpu.VMEM((rows, tn), F32)] * 2,
        compiler_params=_cparams(("parallel", "parallel", "arbitrary")),
        name="ffn_up2",
    )(x, x, x, mod, mod, nw, w, w, cw, cw)


def _rope_tables(seq, rot_dim):
    rows = seq // GRID_W
    r = jnp.repeat(jnp.arange(rows, dtype=F32), GRID_W)
    col = jnp.tile(jnp.arange(GRID_W, dtype=F32), rows)
    quarter = rot_dim // 4
    inv = ROPE_THETA ** (-jnp.arange(quarter, dtype=F32) / quarter)
    ar = r[:, None] * inv
    ac = col[:, None] * inv
    ang = jnp.concatenate([ar, ar, ac, ac], axis=-1)
    reps = 128 // rot_dim
    cos = jnp.tile(jnp.cos(ang), (1, reps))
    sin = jnp.tile(jnp.sin(ang), (1, reps))
    first = ((jnp.arange(128) // quarter) % 2 == 0)[None, :]
    return cos, jnp.where(first, -sin, 0.0), jnp.where(first, 0.0, sin)


def _attn_weights(w_in, qn, kn, mqn, mkvn, w_uq, w_ukv):
    w_in = jnp.pad(w_in, ((0, 0), (0, W_IN_PAD - w_in.shape[1]))).astype(BF16)
    uq = w_uq.reshape(B_Q_RANK, B_HEADS, B_NOPE_DIM + B_ROPE_DIM)
    uq_rope = jnp.pad(uq[:, :, B_NOPE_DIM:], ((0, 0), (0, 0), (0, 128 - B_ROPE_DIM)))
    w_uq = jnp.concatenate([uq[:, :, :B_NOPE_DIM].reshape(B_Q_RANK, -1),
                            uq_rope.reshape(B_Q_RANK, -1)], axis=1).astype(BF16)
    ukv = w_ukv.reshape(B_KV_RANK, B_HEADS, B_NOPE_DIM + B_V_DIM)
    w_ukv = jnp.concatenate([ukv[:, :, :B_NOPE_DIM].reshape(B_KV_RANK, -1),
                             ukv[:, :, B_NOPE_DIM:].reshape(B_KV_RANK, -1)], axis=1).astype(BF16)
    return (w_in, qn.reshape(1, -1), kn.reshape(1, -1), mqn.reshape(1, -1), mkvn.reshape(1, -1),
            w_uq, w_ukv)


def kernel(x, c, ctx, c_ctx, w_ada, b_ada, norm_mix, norm_ffn, attn_w_in, attn_q_norm, attn_k_norm,
           mla_q_norm, mla_kv_norm, mla_w_uq, mla_w_ukv, attn_w_o, sc_w_in, sc_conv, sc_w_out,
           ffn_w_up, ffn_conv, ffn_w_down, final_norm):
    bn, s, _ = x.shape
    cl = ctx.shape[1]
    assert bn <= MOD_ROWS // 2
    cvec = jnp.zeros((MOD_ROWS, D_MODEL), F32).at[:bn].set(c).at[MOD_ROWS // 2].set(c_ctx)
    mod = _adaln(cvec, w_ada, b_ada)
    rope = _rope_tables(s, A_HEAD_DIM) + _rope_tables(s, B_ROPE_DIM)

    xc = ctx
    for l in range(DEPTH):
        later_attn = any(j % 2 == 0 for j in range(l + 1, DEPTH))
        i = l // 2
        nw = norm_mix[l].reshape(1, -1)
        if l % 2 == 0:
            wts = _attn_weights(attn_w_in[i], attn_q_norm[i], attn_k_norm[i], mla_q_norm[i],
                                mla_kv_norm[i], mla_w_uq[i], mla_w_ukv[i])
            w_o = attn_w_o[i].astype(BF16)
            w_oa, w_ob = w_o[:A_HEADS * A_HEAD_DIM], w_o[A_HEADS * A_HEAD_DIM:]
            qa_c, qb_c, ka_c, va_c, kb_c, vb_c = _attn_proj(xc, mod, l, True, nw, wts, None, later_attn, cl)
            qa, qb, ka, va, kb, vb = _attn_proj(x, mod, l, False, nw, wts, rope, True, 256)
            oa = _flash(qa, ka_c, va_c, ka, va, group=A_GROUP, dq=A_HEAD_DIM, dv=A_HEAD_DIM,
                        tq=256, tk=512, name="gqa", depth=(6, 8)[i])
            ob = _flash(qb, kb_c, vb_c, kb, vb, group=1, dq=B_QK_PAD, dv=B_V_DIM,
                        tq=1024, tk=512, name="mla", depth=(4, 6)[i])
            x = _proj_res(x, mod, l, 2, False, [oa, ob], [w_oa, w_ob], 512, name="attn_out")
            if later_attn:
                oa_c = _flash(qa_c, ka_c, va_c, None, None, group=A_GROUP, dq=A_HEAD_DIM, dv=A_HEAD_DIM,
                              tq=cl, tk=0, name="gqa_ctx")
                ob_c = _flash(qb_c, kb_c, vb_c, None, None, group=1, dq=B_QK_PAD, dv=B_V_DIM,
                              tq=cl, tk=0, name="mla_ctx")
                xc = _proj_res(xc, mod, l, 2, True, [oa_c, ob_c], [w_oa, w_ob], cl, name="attn_out")
        else:
            w_in = sc_w_in[i].astype(BF16)
            w_out = sc_w_out[i].astype(BF16)
            z = _up(x, mod, l, False, nw, w_in, sc_conv[i], "sc", 1024, 512, max_dot_rows=(528, 1056)[i])
            x = _proj_res(x, mod, l, 2, False, [z], [w_out], 512, name="sc_out")
            if later_attn:
                zc = _up(xc, mod, l, True, nw, w_in, sc_conv[i], "sc", cl, 512)
                xc = _proj_res(xc, mod, l, 2, True, [zc], [w_out], cl, name="sc_out")
        nwf = norm_ffn[l].reshape(1, -1)
        w_up = ffn_w_up[l].astype(BF16)
        w_down = ffn_w_down[l].astype(BF16)
        if l < 3:
            hid = _up2(x, mod, l, False, nwf, w_up, ffn_conv[l], "ffn", 1024, 512, (1024, 32, 128)[l])
        else:
            hid = _up(x, mod, l, False, nwf, w_up, ffn_conv[l], "ffn", 1024, 512, max_dot_rows=352)
        fw = final_norm.reshape(1, -1) if l == DEPTH - 1 else None
        x = _proj_res(x, mod, l, 5, False, [hid], [w_down], 256, final_w=fw, name="ffn_down")
        if later_attn:
            hid_c = _up(xc, mod, l, True, nwf, w_up, ffn_conv[l], "ffn", cl, 512)
            xc = _proj_res(xc, mod, l, 5, True, [hid_c], [w_down], cl, name="ffn_down")
    return x
```

```python
import functools

import jax
import jax.numpy as jnp
from jax import lax
from jax.experimental import pallas as pl
from jax.experimental.pallas import tpu as pltpu

F32 = jnp.float32
BF16 = jnp.bfloat16

D_MODEL = 2048
DEPTH = 4
GRID_W = 64
ROPE_THETA = 10000.0
EPS = 1e-6
A_HEADS = 8
A_KV_HEADS = 2
A_GROUP = A_HEADS // A_KV_HEADS
A_HEAD_DIM = 128
B_HEADS = 8
B_Q_RANK = 512
B_KV_RANK = 256
B_NOPE_DIM = 128
B_ROPE_DIM = 64
B_V_DIM = 128
B_QK_PAD = 256
LOG2E = 1.4426950408889634
A_SCALE = A_HEAD_DIM ** -0.5 * LOG2E
B_SCALE = (B_NOPE_DIM + B_ROPE_DIM) ** -0.5 * LOG2E
QGROUP = 256
V_ROWS = 128 + 16
SC_WIDTH = D_MODEL
D_FF = 256 * ((8 * D_MODEL // 3 + 255) // 256)
W_IN_PAD = 2432
MOD_ROWS = 16
HALO = 16
NORM_ROWS = 32
VMEM_LIMIT = 56 * 1024 * 1024


def _cparams(sem):
    return pltpu.CompilerParams(dimension_semantics=sem, vmem_limit_bytes=VMEM_LIMIT)


def _const_spec(shape):
    nd = len(shape)
    return pl.BlockSpec(shape, lambda *g: (0,) * nd, pipeline_mode=pl.Buffered(1))


def _mod_spec(layer, part, ctx):
    if ctx:
        idx = lambda b, *g: (layer, MOD_ROWS // 2, part, 0, 0)
    else:
        idx = lambda b, *g: (layer, b, part, 0, 0)
    return pl.BlockSpec((None, None, None, 1, D_MODEL), idx)


def _rms(x):
    return x * lax.rsqrt(jnp.mean(x * x, axis=-1, keepdims=True) + EPS)


def _rope(x, cos, sin_up, sin_dn, quarter):
    n = x.shape[-1]
    return x * cos + pltpu.roll(x, n - quarter, 1) * sin_up + pltpu.roll(x, quarter, 1) * sin_dn


def _adaln_body(c_ref, w_ref, b_ref, o_ref):
    c = c_ref[...]
    s = (c / (1.0 + jnp.exp(-c))).astype(BF16)
    o_ref[...] = jnp.dot(s, w_ref[...].astype(BF16), preferred_element_type=F32) + b_ref[...]


def _adaln(cvec, w_ada, b_ada):
    tn = 1024
    n = 6 * D_MODEL
    out = pl.pallas_call(
        _adaln_body,
        grid=(DEPTH, n // tn),
        in_specs=[
            pl.BlockSpec((MOD_ROWS, D_MODEL), lambda l, j: (0, 0)),
            pl.BlockSpec((None, D_MODEL, tn), lambda l, j: (l, 0, j)),
            pl.BlockSpec((None, 1, tn), lambda l, j: (l, 0, j)),
        ],
        out_specs=pl.BlockSpec((None, MOD_ROWS, tn), lambda l, j: (l, 0, j)),
        out_shape=jax.ShapeDtypeStruct((DEPTH, MOD_ROWS, n), F32),
        compiler_params=_cparams(("parallel", "parallel")),
        name="adaln",
    )(cvec, w_ada, b_ada.reshape(DEPTH, 1, n))
    return out.reshape(DEPTH, MOD_ROWS, 6, 1, D_MODEL)


def _attn_proj_body(*refs, use_rope, with_q):
    it = iter(refs)
    x_ref, sh_ref, sc_ref, nw_ref, win_ref = (next(it) for _ in range(5))
    qn_ref, kn_ref, mqn_ref, mkvn_ref, wuq_ref, wukv_ref = (next(it) for _ in range(6))
    if use_rope:
        ca, sau, sad, cb, sbu, sbd = (next(it)[...] for _ in range(6))
    if with_q:
        qa_ref, qb_ref = next(it), next(it)
    ka_ref, va_ref, kb_ref, vb_ref, h_scr = (next(it) for _ in range(5))

    def rope_a(v):
        return _rope(v, ca, sau, sad, A_HEAD_DIM // 4) if use_rope else v

    def rope_b(v):
        return _rope(v, cb, sbu, sbd, B_ROPE_DIM // 4) if use_rope else v

    gain = nw_ref[...] * (1.0 + sc_ref[...])
    sh = sh_ref[...]
    tm = x_ref.shape[0]
    for r in range(0, tm, NORM_ROWS):
        h_scr[r:r + NORM_ROWS, :] = (_rms(x_ref[r:r + NORM_ROWS, :]) * gain + sh).astype(BF16)
    proj = jnp.dot(h_scr[...], win_ref[...], preferred_element_type=F32)
    q_end = A_HEADS * A_HEAD_DIM
    k_end = q_end + A_KV_HEADS * A_HEAD_DIM
    v_end = k_end + A_KV_HEADS * A_HEAD_DIM
    cq_end = v_end + B_Q_RANK
    ckv_end = cq_end + B_KV_RANK
    nope_w = B_HEADS * B_NOPE_DIM

    ckv = (_rms(proj[:, cq_end:ckv_end]) * mkvn_ref[...]).astype(BF16)
    kv = jnp.dot(ckv, wukv_ref[...], preferred_element_type=F32)
    if with_q:
        cq = (_rms(proj[:, v_end:cq_end]) * mqn_ref[...]).astype(BF16)
        qb = jnp.dot(cq, wuq_ref[...], preferred_element_type=F32)

    def put_vt(ref, hh, v):
        lo = hh * V_ROWS
        ref[lo:lo + v.shape[1], :] = v.T.astype(BF16)
        ref[lo + v.shape[1]:lo + V_ROWS, :] = jnp.ones((V_ROWS - v.shape[1], v.shape[0]), BF16)

    kn = kn_ref[...]
    for kh in range(A_KV_HEADS):
        lo = q_end + kh * A_HEAD_DIM
        k = rope_a(_rms(proj[:, lo:lo + A_HEAD_DIM]) * kn)
        ka_ref[:, kh * A_HEAD_DIM:(kh + 1) * A_HEAD_DIM] = k.astype(BF16)
        lo = k_end + kh * A_HEAD_DIM
        put_vt(va_ref, kh, proj[:, lo:lo + A_HEAD_DIM])
    if with_q:
        qn = qn_ref[...]
        for hh in range(A_HEADS):
            lo = hh * A_HEAD_DIM
            q = rope_a(_rms(proj[:, lo:lo + A_HEAD_DIM]) * qn) * A_SCALE
            qa_ref[:, lo:lo + A_HEAD_DIM] = q.astype(BF16)

    kr = rope_b(proj[:, ckv_end:W_IN_PAD]).astype(BF16)
    for hh in range(B_HEADS):
        put_vt(vb_ref, hh, kv[:, nope_w + hh * B_V_DIM:nope_w + (hh + 1) * B_V_DIM])
        lo = hh * B_QK_PAD
        kb_ref[:, lo:lo + B_NOPE_DIM] = kv[:, hh * B_NOPE_DIM:(hh + 1) * B_NOPE_DIM].astype(BF16)
        kb_ref[:, lo + B_NOPE_DIM:lo + B_QK_PAD] = kr
    if with_q:
        for hh in range(B_HEADS):
            lo = hh * B_QK_PAD
            qb_ref[:, lo:lo + B_NOPE_DIM] = (qb[:, hh * B_NOPE_DIM:(hh + 1) * B_NOPE_DIM] * B_SCALE).astype(BF16)
            r = rope_b(qb[:, nope_w + hh * 128:nope_w + (hh + 1) * 128]) * B_SCALE
            qb_ref[:, lo + B_NOPE_DIM:lo + B_QK_PAD] = r.astype(BF16)


def _attn_proj(x, mod, layer, ctx, nw, wts, rope, with_q, tm):
    bn, s, _ = x.shape
    use_rope = rope is not None
    w_in, qn, kn, mqn, mkvn, w_uq, w_ukv = wts
    row = lambda w: pl.BlockSpec((None, tm, w), lambda b, i: (b, i, 0))
    in_specs = [row(D_MODEL), _mod_spec(layer, 0, ctx), _mod_spec(layer, 1, ctx),
                _const_spec((1, D_MODEL)), _const_spec(w_in.shape),
                _const_spec(qn.shape), _const_spec(kn.shape), _const_spec(mqn.shape),
                _const_spec(mkvn.shape), _const_spec(w_uq.shape), _const_spec(w_ukv.shape)]
    args = [x, mod, mod, nw, w_in, qn, kn, mqn, mkvn, w_uq, w_ukv]
    if use_rope:
        in_specs += [pl.BlockSpec((tm, 128), lambda b, i: (i, 0))] * 6
        args += list(rope)
    col = lambda w: pl.BlockSpec((None, w, tm), lambda b, i: (b, 0, i))
    outs_desc = ([(A_HEADS * A_HEAD_DIM, False), (B_HEADS * B_QK_PAD, False)] if with_q else []) + [
        (A_KV_HEADS * A_HEAD_DIM, False), (A_KV_HEADS * V_ROWS, True),
        (B_HEADS * B_QK_PAD, False), (B_HEADS * V_ROWS, True)]
    outs = pl.pallas_call(
        functools.partial(_attn_proj_body, use_rope=use_rope, with_q=with_q),
        grid=(bn, s // tm),
        in_specs=in_specs,
        out_specs=[col(w) if t else row(w) for w, t in outs_desc],
        out_shape=[jax.ShapeDtypeStruct((bn, w, s) if t else (bn, s, w), BF16) for w, t in outs_desc],
        scratch_shapes=[pltpu.VMEM((tm, D_MODEL), BF16)],
        compiler_params=_cparams(("parallel", "parallel")),
        name="attn_proj_ctx" if ctx else "attn_proj",
    )(*args)
    return outs if with_q else [None, None] + list(outs)


def _col_reduce(x, op, final):
    while x.shape[0] > 8 and x.shape[0] % 16 == 0:
        h = x.shape[0] // 2
        x = op(x[:h], x[h:])
    return final(x, axis=0, keepdims=True)


def _flash_body(*refs, group, dq, dv, tk, n_lat, depth):
    if n_lat:
        q_ref, kc_ref, vct_ref, k_ref, vt_ref, o_ref = refs
    else:
        q_ref, kc_ref, vct_ref, o_ref = refs
    tq = q_ref.shape[0]
    rows = [slice(r, r + QGROUP) for r in range(0, tq, QGROUP)]
    q_parts = [(r, slice(g * dq, (g + 1) * dq)) for r in rows for g in range(group)]
    o_parts = [(r, slice(g * dv, (g + 1) * dv)) for r in rows for g in range(group)]
    n = len(q_parts)
    chunks = [(kc_ref, vct_ref, 0, kc_ref.shape[0])] + [(k_ref, vt_ref, c * tk, tk) for c in range(n_lat)]
    m, acc = [None] * n, [None] * n
    blocks = [(ci, g) for ci in range(len(chunks)) for g in range(n)]

    def scores(ci, g):
        kr, _, off, size = chunks[ci]
        return lax.dot_general(kr[off:off + size, :], q_ref[q_parts[g]], (((1,), (1,)), ((), ())),
                               preferred_element_type=F32)

    s_next = [scores(*blocks[d]) for d in range(min(depth, len(blocks)))]
    for idx, (ci, g) in enumerate(blocks):
        s = s_next.pop(0)
        if idx + depth < len(blocks):
            s_next.append(scores(*blocks[idx + depth]))
        _, vr, off, size = chunks[ci]
        vt = vr[:, off:off + size]
        smax = _col_reduce(s, jnp.maximum, jnp.max)
        if ci == 0:
            m_new = smax
            p = jnp.exp2(s - m_new)
            acc[g] = jnp.dot(vt, p.astype(BF16), preferred_element_type=F32)
        else:
            m_new = jnp.maximum(m[g], smax)
            alpha = jnp.exp2(m[g] - m_new)
            p = jnp.exp2(s - m_new)
            acc[g] = alpha * acc[g] + jnp.dot(vt, p.astype(BF16), preferred_element_type=F32)
        m[g] = m_new
    for g in range(n):
        o_ref[o_parts[g]] = (acc[g][:dv] / acc[g][dv:dv + 1]).T.astype(o_ref.dtype)


def _flash(q, kc, vct, k, vt, *, group, dq, dv, tq, tk, name, depth=4):
    bn, s, qw = q.shape
    kvh = qw // (group * dq)
    c = kc.shape[1]
    n_lat = 0 if k is None else k.shape[1] // tk
    in_specs = [pl.BlockSpec((None, tq, group * dq), lambda b, h, i: (b, i, h)),
                pl.BlockSpec((None, c, dq), lambda b, h, i: (b, 0, h)),
                pl.BlockSpec((None, V_ROWS, c), lambda b, h, i: (b, h, 0))]
    args = [q, kc, vct]
    if n_lat:
        t = k.shape[1]
        in_specs += [pl.BlockSpec((None, t, dq), lambda b, h, i: (b, 0, h)),
                     pl.BlockSpec((None, V_ROWS, t), lambda b, h, i: (b, h, 0))]
        args += [k, vt]
    return pl.pallas_call(
        functools.partial(_flash_body, group=group, dq=dq, dv=dv, tk=tk, n_lat=n_lat, depth=depth),
        grid=(bn, kvh, s // tq),
        in_specs=in_specs,
        out_specs=pl.BlockSpec((None, tq, group * dv), lambda b, h, i: (b, i, h)),
        out_shape=jax.ShapeDtypeStruct((bn, s, kvh * group * dv), BF16),
        compiler_params=_cparams(("parallel", "parallel", "parallel")),
        name=name,
    )(*args)


def _proj_res_body(*refs, n_lhs, final):
    x_ref, g_ref = refs[0], refs[1]
    lhs = refs[2:2 + n_lhs]
    ws = refs[2 + n_lhs:2 + 2 * n_lhs]
    rest = refs[2 + 2 * n_lhs:]
    y = jnp.dot(lhs[0][...], ws[0][...], preferred_element_type=F32)
    for a, w in zip(lhs[1:], ws[1:]):
        y = y + jnp.dot(a[...], w[...], preferred_element_type=F32)
    out = x_ref[...] + g_ref[...] * y
    if final:
        fw_ref, o_ref = rest
        out = _rms(out) * fw_ref[...]
    else:
        (o_ref,) = rest
    o_ref[...] = out


def _proj_res(x, mod, layer, part, ctx, lhs, ws, tm, final_w=None, name="proj_res"):
    bn, s, _ = x.shape
    row = lambda w: pl.BlockSpec((None, tm, w), lambda b, i: (b, i, 0))
    in_specs = [row(D_MODEL), _mod_spec(layer, part, ctx)]
    in_specs += [row(a.shape[-1]) for a in lhs] + [_const_spec(w.shape) for w in ws]
    args = [x, mod] + list(lhs) + list(ws)
    if final_w is not None:
        in_specs.append(_const_spec((1, D_MODEL)))
        args.append(final_w)
    return pl.pallas_call(
        functools.partial(_proj_res_body, n_lhs=len(lhs), final=final_w is not None),
        grid=(bn, s // tm),
        in_specs=in_specs,
        out_specs=row(D_MODEL),
        out_shape=jax.ShapeDtypeStruct(x.shape, F32),
        compiler_params=_cparams(("parallel", "parallel")),
        name=name + ("_ctx" if ctx else ""),
    )(*args)


def _up_body(*refs, mode, tm):
    x_ref, xp_ref, xn_ref, sh_ref, sc_ref, nw_ref = refs[:6]
    i = pl.program_id(1)
    j = pl.program_id(2)
    if mode == "ffn":
        wg_ref, wu_ref, cg_ref, cu_ref, o_ref, h_scr, g_scr, u_scr = refs[6:]
    else:
        wb_ref, wc_ref, wv_ref, cw_ref, o_ref, h_scr, g_scr = refs[6:]

    @pl.when(j == 0)
    def _():
        gain = nw_ref[...] * (1.0 + sc_ref[...])
        sh = sh_ref[...]
        for r in range(0, tm, NORM_ROWS):
            h_scr[HALO + r:HALO + r + NORM_ROWS, :] = (
                _rms(x_ref[r:r + NORM_ROWS, :]) * gain + sh).astype(BF16)
        hp = _rms(xp_ref[...]) * gain + sh
        h_scr[0:HALO, :] = jnp.where(i == 0, 0.0, hp).astype(BF16)
        hn = _rms(xn_ref[...]) * gain + sh
        h_scr[HALO + tm:2 * HALO + tm, :] = jnp.where(i == pl.num_programs(1) - 1, 0.0, hn).astype(BF16)

    def conv(scr, cw_ref):
        return (cw_ref[0:1, :] * scr[HALO - 1:HALO - 1 + tm, :]
                + cw_ref[1:2, :] * scr[HALO:HALO + tm, :]
                + cw_ref[2:3, :] * scr[HALO + 1:HALO + 1 + tm, :])

    h = h_scr[...]
    if mode == "ffn":
        g_scr[...] = jnp.dot(h, wg_ref[...], preferred_element_type=F32)
        u_scr[...] = jnp.dot(h, wu_ref[...], preferred_element_type=F32)
        g = conv(g_scr, cg_ref)
        u = conv(u_scr, cu_ref)
        o_ref[...] = ((g / (1.0 + jnp.exp(-g))) * u).astype(o_ref.dtype)
    else:
        g_scr[...] = (jnp.dot(h, wc_ref[...], preferred_element_type=F32)
                      * jnp.dot(h, wv_ref[...], preferred_element_type=F32))
        b = jnp.dot(h_scr[HALO:HALO + tm, :], wb_ref[...], preferred_element_type=F32)
        o_ref[...] = (b * conv(g_scr, cw_ref)).astype(o_ref.dtype)


def _up(x, mod, layer, ctx, nw, w, cw, mode, tm, tn):
    bn, s, _ = x.shape
    parts = 2 if mode == "ffn" else 3
    width = w.shape[1] // parts
    nj = width // tn
    hb = tm // HALO
    last_hb = s // HALO - 1
    in_specs = [
        pl.BlockSpec((None, tm, D_MODEL), lambda b, i, j: (b, i, 0)),
        pl.BlockSpec((None, HALO, D_MODEL), lambda b, i, j: (b, jnp.maximum(i * hb - 1, 0), 0)),
        pl.BlockSpec((None, HALO, D_MODEL), lambda b, i, j: (b, jnp.minimum((i + 1) * hb, last_hb), 0)),
        _mod_spec(layer, 3 if mode == "ffn" else 0, ctx),
        _mod_spec(layer, 4 if mode == "ffn" else 1, ctx),
        _const_spec((1, D_MODEL)),
    ]
    args = [x, x, x, mod, mod, nw]
    wspec = lambda p: pl.BlockSpec((D_MODEL, tn), lambda b, i, j: (0, p * nj + j))
    cspec = lambda p: pl.BlockSpec((3, tn), lambda b, i, j: (0, p * nj + j))
    rows = tm + 2 * HALO
    scratch = [pltpu.VMEM((rows, D_MODEL), BF16), pltpu.VMEM((rows, tn), F32)]
    if mode == "ffn":
        in_specs += [wspec(0), wspec(1), cspec(0), cspec(1)]
        args += [w, w, cw, cw]
        scratch.append(pltpu.VMEM((rows, tn), F32))
    else:
        in_specs += [wspec(0), wspec(1), wspec(2), cspec(0)]
        args += [w, w, w, cw]
    return pl.pallas_call(
        functools.partial(_up_body, mode=mode, tm=tm),
        grid=(bn, s // tm, nj),
        in_specs=in_specs,
        out_specs=pl.BlockSpec((None, tm, tn), lambda b, i, j: (b, i, j)),
        out_shape=jax.ShapeDtypeStruct((bn, s, width), BF16),
        scratch_shapes=scratch,
        compiler_params=_cparams(("parallel", "parallel", "arbitrary")),
        name=mode + "_up" + ("_ctx" if ctx else ""),
    )(*args)


def _rope_tables(seq, rot_dim):
    rows = seq // GRID_W
    r = jnp.repeat(jnp.arange(rows, dtype=F32), GRID_W)
    col = jnp.tile(jnp.arange(GRID_W, dtype=F32), rows)
    quarter = rot_dim // 4
    inv = ROPE_THETA ** (-jnp.arange(quarter, dtype=F32) / quarter)
    ar = r[:, None] * inv
    ac = col[:, None] * inv
    ang = jnp.concatenate([ar, ar, ac, ac], axis=-1)
    reps = 128 // rot_dim
    cos = jnp.tile(jnp.cos(ang), (1, reps))
    sin = jnp.tile(jnp.sin(ang), (1, reps))
    first = ((jnp.arange(128) // quarter) % 2 == 0)[None, :]
    return cos, jnp.where(first, -sin, 0.0), jnp.where(first, 0.0, sin)


def _attn_weights(w_in, qn, kn, mqn, mkvn, w_uq, w_ukv):
    w_in = jnp.pad(w_in, ((0, 0), (0, W_IN_PAD - w_in.shape[1]))).astype(BF16)
    uq = w_uq.reshape(B_Q_RANK, B_HEADS, B_NOPE_DIM + B_ROPE_DIM)
    uq_rope = jnp.pad(uq[:, :, B_NOPE_DIM:], ((0, 0), (0, 0), (0, 128 - B_ROPE_DIM)))
    w_uq = jnp.concatenate([uq[:, :, :B_NOPE_DIM].reshape(B_Q_RANK, -1),
                            uq_rope.reshape(B_Q_RANK, -1)], axis=1).astype(BF16)
    ukv = w_ukv.reshape(B_KV_RANK, B_HEADS, B_NOPE_DIM + B_V_DIM)
    w_ukv = jnp.concatenate([ukv[:, :, :B_NOPE_DIM].reshape(B_KV_RANK, -1),
                             ukv[:, :, B_NOPE_DIM:].reshape(B_KV_RANK, -1)], axis=1).astype(BF16)
    return (w_in, qn.reshape(1, -1), kn.reshape(1, -1), mqn.reshape(1, -1), mkvn.reshape(1, -1),
            w_uq, w_ukv)


def kernel(x, c, ctx, c_ctx, w_ada, b_ada, norm_mix, norm_ffn, attn_w_in, attn_q_norm, attn_k_norm,
           mla_q_norm, mla_kv_norm, mla_w_uq, mla_w_ukv, attn_w_o, sc_w_in, sc_conv, sc_w_out,
           ffn_w_up, ffn_conv, ffn_w_down, final_norm):
    bn, s, _ = x.shape
    cl = ctx.shape[1]
    assert bn <= MOD_ROWS // 2
    cvec = jnp.zeros((MOD_ROWS, D_MODEL), F32).at[:bn].set(c).at[MOD_ROWS // 2].set(c_ctx)
    mod = _adaln(cvec, w_ada, b_ada)
    rope = _rope_tables(s, A_HEAD_DIM) + _rope_tables(s, B_ROPE_DIM)

    xc = ctx
    for l in range(DEPTH):
        later_attn = any(j % 2 == 0 for j in range(l + 1, DEPTH))
        i = l // 2
        nw = norm_mix[l].reshape(1, -1)
        if l % 2 == 0:
            wts = _attn_weights(attn_w_in[i], attn_q_norm[i], attn_k_norm[i], mla_q_norm[i],
                                mla_kv_norm[i], mla_w_uq[i], mla_w_ukv[i])
            w_o = attn_w_o[i].astype(BF16)
            w_oa, w_ob = w_o[:A_HEADS * A_HEAD_DIM], w_o[A_HEADS * A_HEAD_DIM:]
            qa_c, qb_c, ka_c, va_c, kb_c, vb_c = _attn_proj(xc, mod, l, True, nw, wts, None, later_attn, cl)
            qa, qb, ka, va, kb, vb = _attn_proj(x, mod, l, False, nw, wts, rope, True, 256)
            oa = _flash(qa, ka_c, va_c, ka, va, group=A_GROUP, dq=A_HEAD_DIM, dv=A_HEAD_DIM,
                        tq=512, tk=512, name="gqa", depth=6)
            ob = _flash(qb, kb_c, vb_c, kb, vb, group=1, dq=B_QK_PAD, dv=B_V_DIM,
                        tq=2048, tk=512, name="mla", depth=6)
            x = _proj_res(x, mod, l, 2, False, [oa, ob], [w_oa, w_ob], 512, name="attn_out")
            if later_attn:
                oa_c = _flash(qa_c, ka_c, va_c, None, None, group=A_GROUP, dq=A_HEAD_DIM, dv=A_HEAD_DIM,
                              tq=cl, tk=0, name="gqa_ctx")
                ob_c = _flash(qb_c, kb_c, vb_c, None, None, group=1, dq=B_QK_PAD, dv=B_V_DIM,
                              tq=cl, tk=0, name="mla_ctx")
                xc = _proj_res(xc, mod, l, 2, True, [oa_c, ob_c], [w_oa, w_ob], cl, name="attn_out")
        else:
            w_in = sc_w_in[i].astype(BF16)
            w_out = sc_w_out[i].astype(BF16)
            z = _up(x, mod, l, False, nw, w_in, sc_conv[i], "sc", 1024, 512)
            x = _proj_res(x, mod, l, 2, False, [z], [w_out], 512, name="sc_out")
            if later_attn:
                zc = _up(xc, mod, l, True, nw, w_in, sc_conv[i], "sc", cl, 512)
                xc = _proj_res(xc, mod, l, 2, True, [zc], [w_out], cl, name="sc_out")
        nwf = norm_ffn[l].reshape(1, -1)
        w_up = ffn_w_up[l].astype(BF16)
        w_down = ffn_w_down[l].astype(BF16)
        hid = _up(x, mod, l, False, nwf, w_up, ffn_conv[l], "ffn", 1024, 512)
        fw = final_norm.reshape(1, -1) if l == DEPTH - 1 else None
        x = _proj_res(x, mod, l, 5, False, [hid], [w_down], 256, final_w=fw, name="ffn_down")
        if later_attn:
            hid_c = _up(xc, mod, l, True, nwf, w_up, ffn_conv[l], "ffn", cl, 512)
            xc = _proj_res(xc, mod, l, 5, True, [hid_c], [w_down], cl, name="ffn_down")
    return x
```

```python
import functools

import jax
import jax.numpy as jnp
from jax import lax
from jax.experimental import pallas as pl
from jax.experimental.pallas import tpu as pltpu

F32 = jnp.float32
BF16 = jnp.bfloat16

D_MODEL = 2048
DEPTH = 4
GRID_W = 64
ROPE_THETA = 10000.0
EPS = 1e-6
A_HEADS = 8
A_KV_HEADS = 2
A_GROUP = A_HEADS // A_KV_HEADS
A_HEAD_DIM = 128
B_HEADS = 8
B_Q_RANK = 512
B_KV_RANK = 256
B_NOPE_DIM = 128
B_ROPE_DIM = 64
B_V_DIM = 128
B_QK_PAD = 256
LOG2E = 1.4426950408889634
A_SCALE = A_HEAD_DIM ** -0.5 * LOG2E
B_SCALE = (B_NOPE_DIM + B_ROPE_DIM) ** -0.5 * LOG2E
QGROUP = 256
V_ROWS = 128 + 16
SC_WIDTH = D_MODEL
D_FF = 256 * ((8 * D_MODEL // 3 + 255) // 256)
W_IN_PAD = 2432
MOD_ROWS = 16
HALO = 16
NORM_ROWS = 32
VMEM_LIMIT = 56 * 1024 * 1024


def _cparams(sem):
    return pltpu.CompilerParams(dimension_semantics=sem, vmem_limit_bytes=VMEM_LIMIT)


def _const_spec(shape):
    nd = len(shape)
    return pl.BlockSpec(shape, lambda *g: (0,) * nd, pipeline_mode=pl.Buffered(1))


def _mod_spec(layer, part, ctx):
    if ctx:
        idx = lambda b, *g: (layer, MOD_ROWS // 2, part, 0, 0)
    else:
        idx = lambda b, *g: (layer, b, part, 0, 0)
    return pl.BlockSpec((None, None, None, 1, D_MODEL), idx)


def _rms(x):
    return x * lax.rsqrt(jnp.mean(x * x, axis=-1, keepdims=True) + EPS)


def _rope(x, cos, sin_up, sin_dn, quarter):
    n = x.shape[-1]
    return x * cos + pltpu.roll(x, n - quarter, 1) * sin_up + pltpu.roll(x, quarter, 1) * sin_dn


def _adaln_body(c_ref, w_ref, b_ref, o_ref):
    c = c_ref[...]
    s = (c / (1.0 + jnp.exp(-c))).astype(BF16)
    o_ref[...] = jnp.dot(s, w_ref[...].astype(BF16), preferred_element_type=F32) + b_ref[...]


def _adaln(cvec, w_ada, b_ada):
    tn = 1024
    n = 6 * D_MODEL
    out = pl.pallas_call(
        _adaln_body,
        grid=(DEPTH, n // tn),
        in_specs=[
            pl.BlockSpec((MOD_ROWS, D_MODEL), lambda l, j: (0, 0)),
            pl.BlockSpec((None, D_MODEL, tn), lambda l, j: (l, 0, j)),
            pl.BlockSpec((None, 1, tn), lambda l, j: (l, 0, j)),
        ],
        out_specs=pl.BlockSpec((None, MOD_ROWS, tn), lambda l, j: (l, 0, j)),
        out_shape=jax.ShapeDtypeStruct((DEPTH, MOD_ROWS, n), F32),
        compiler_params=_cparams(("parallel", "parallel")),
        name="adaln",
    )(cvec, w_ada, b_ada.reshape(DEPTH, 1, n))
    return out.reshape(DEPTH, MOD_ROWS, 6, 1, D_MODEL)


def _attn_proj_body(*refs, use_rope, with_q):
    it = iter(refs)
    x_ref, sh_ref, sc_ref, nw_ref, win_ref = (next(it) for _ in range(5))
    qn_ref, kn_ref, mqn_ref, mkvn_ref, wuq_ref, wukv_ref = (next(it) for _ in range(6))
    if use_rope:
        ca, sau, sad, cb, sbu, sbd = (next(it)[...] for _ in range(6))
    if with_q:
        qa_ref, qb_ref = next(it), next(it)
    ka_ref, va_ref, kb_ref, vb_ref, h_scr = (next(it) for _ in range(5))

    def rope_a(v):
        return _rope(v, ca, sau, sad, A_HEAD_DIM // 4) if use_rope else v

    def rope_b(v):
        return _rope(v, cb, sbu, sbd, B_ROPE_DIM // 4) if use_rope else v

    gain = nw_ref[...] * (1.0 + sc_ref[...])
    sh = sh_ref[...]
    tm = x_ref.shape[0]
    for r in range(0, tm, NORM_ROWS):
        h_scr[r:r + NORM_ROWS, :] = (_rms(x_ref[r:r + NORM_ROWS, :]) * gain + sh).astype(BF16)
    proj = jnp.dot(h_scr[...], win_ref[...], preferred_element_type=F32)
    q_end = A_HEADS * A_HEAD_DIM
    k_end = q_end + A_KV_HEADS * A_HEAD_DIM
    v_end = k_end + A_KV_HEADS * A_HEAD_DIM
    cq_end = v_end + B_Q_RANK
    ckv_end = cq_end + B_KV_RANK
    nope_w = B_HEADS * B_NOPE_DIM

    ckv = (_rms(proj[:, cq_end:ckv_end]) * mkvn_ref[...]).astype(BF16)
    kv = jnp.dot(ckv, wukv_ref[...], preferred_element_type=F32)
    if with_q:
        cq = (_rms(proj[:, v_end:cq_end]) * mqn_ref[...]).astype(BF16)
        qb = jnp.dot(cq, wuq_ref[...], preferred_element_type=F32)

    def put_vt(ref, hh, v):
        lo = hh * V_ROWS
        ref[lo:lo + v.shape[1], :] = v.T.astype(BF16)
        ref[lo + v.shape[1]:lo + V_ROWS, :] = jnp.ones((V_ROWS - v.shape[1], v.shape[0]), BF16)

    kn = kn_ref[...]
    for kh in range(A_KV_HEADS):
        lo = q_end + kh * A_HEAD_DIM
        k = rope_a(_rms(proj[:, lo:lo + A_HEAD_DIM]) * kn)
        ka_ref[:, kh * A_HEAD_DIM:(kh + 1) * A_HEAD_DIM] = k.astype(BF16)
        lo = k_end + kh * A_HEAD_DIM
        put_vt(va_ref, kh, proj[:, lo:lo + A_HEAD_DIM])
    if with_q:
        qn = qn_ref[...]
        for hh in range(A_HEADS):
            lo = hh * A_HEAD_DIM
            q = rope_a(_rms(proj[:, lo:lo + A_HEAD_DIM]) * qn) * A_SCALE
            qa_ref[:, lo:lo + A_HEAD_DIM] = q.astype(BF16)

    kr = rope_b(proj[:, ckv_end:W_IN_PAD]).astype(BF16)
    for hh in range(B_HEADS):
        put_vt(vb_ref, hh, kv[:, nope_w + hh * B_V_DIM:nope_w + (hh + 1) * B_V_DIM])
        lo = hh * B_QK_PAD
        kb_ref[:, lo:lo + B_NOPE_DIM] = kv[:, hh * B_NOPE_DIM:(hh + 1) * B_NOPE_DIM].astype(BF16)
        kb_ref[:, lo + B_NOPE_DIM:lo + B_QK_PAD] = kr
    if with_q:
        for hh in range(B_HEADS):
            lo = hh * B_QK_PAD
            qb_ref[:, lo:lo + B_NOPE_DIM] = (qb[:, hh * B_NOPE_DIM:(hh + 1) * B_NOPE_DIM] * B_SCALE).astype(BF16)
            r = rope_b(qb[:, nope_w + hh * 128:nope_w + (hh + 1) * 128]) * B_SCALE
            qb_ref[:, lo + B_NOPE_DIM:lo + B_QK_PAD] = r.astype(BF16)


def _attn_proj(x, mod, layer, ctx, nw, wts, rope, with_q, tm):
    bn, s, _ = x.shape
    use_rope = rope is not None
    w_in, qn, kn, mqn, mkvn, w_uq, w_ukv = wts
    row = lambda w: pl.BlockSpec((None, tm, w), lambda b, i: (b, i, 0))
    in_specs = [row(D_MODEL), _mod_spec(layer, 0, ctx), _mod_spec(layer, 1, ctx),
                _const_spec((1, D_MODEL)), _const_spec(w_in.shape),
                _const_spec(qn.shape), _const_spec(kn.shape), _const_spec(mqn.shape),
                _const_spec(mkvn.shape), _const_spec(w_uq.shape), _const_spec(w_ukv.shape)]
    args = [x, mod, mod, nw, w_in, qn, kn, mqn, mkvn, w_uq, w_ukv]
    if use_rope:
        in_specs += [pl.BlockSpec((tm, 128), lambda b, i: (i, 0))] * 6
        args += list(rope)
    col = lambda w: pl.BlockSpec((None, w, tm), lambda b, i: (b, 0, i))
    outs_desc = ([(A_HEADS * A_HEAD_DIM, False), (B_HEADS * B_QK_PAD, False)] if with_q else []) + [
        (A_KV_HEADS * A_HEAD_DIM, False), (A_KV_HEADS * V_ROWS, True),
        (B_HEADS * B_QK_PAD, False), (B_HEADS * V_ROWS, True)]
    outs = pl.pallas_call(
        functools.partial(_attn_proj_body, use_rope=use_rope, with_q=with_q),
        grid=(bn, s // tm),
        in_specs=in_specs,
        out_specs=[col(w) if t else row(w) for w, t in outs_desc],
        out_shape=[jax.ShapeDtypeStruct((bn, w, s) if t else (bn, s, w), BF16) for w, t in outs_desc],
        scratch_shapes=[pltpu.VMEM((tm, D_MODEL), BF16)],
        compiler_params=_cparams(("parallel", "parallel")),
        name="attn_proj_ctx" if ctx else "attn_proj",
    )(*args)
    return outs if with_q else [None, None] + list(outs)


def _col_reduce(x, op, final):
    while x.shape[0] > 8 and x.shape[0] % 16 == 0:
        h = x.shape[0] // 2
        x = op(x[:h], x[h:])
    return final(x, axis=0, keepdims=True)


def _flash_body(*refs, group, dq, dv, tk, n_lat, depth):
    if n_lat:
        q_ref, kc_ref, vct_ref, k_ref, vt_ref, o_ref = refs
    else:
        q_ref, kc_ref, vct_ref, o_ref = refs
    tq = q_ref.shape[0]
    rows = [slice(r, r + QGROUP) for r in range(0, tq, QGROUP)]
    q_parts = [(r, slice(g * dq, (g + 1) * dq)) for r in rows for g in range(group)]
    o_parts = [(r, slice(g * dv, (g + 1) * dv)) for r in rows for g in range(group)]
    n = len(q_parts)
    chunks = [(kc_ref, vct_ref, 0, kc_ref.shape[0])] + [(k_ref, vt_ref, c * tk, tk) for c in range(n_lat)]
    m, acc = [None] * n, [None] * n
    blocks = [(ci, g) for ci in range(len(chunks)) for g in range(n)]

    def scores(ci, g):
        kr, _, off, size = chunks[ci]
        return lax.dot_general(kr[off:off + size, :], q_ref[q_parts[g]], (((1,), (1,)), ((), ())),
                               preferred_element_type=F32)

    s_next = [scores(*blocks[d]) for d in range(min(depth, len(blocks)))]
    for idx, (ci, g) in enumerate(blocks):
        s = s_next.pop(0)
        if idx + depth < len(blocks):
            s_next.append(scores(*blocks[idx + depth]))
        _, vr, off, size = chunks[ci]
        vt = vr[:, off:off + size]
        smax = _col_reduce(s, jnp.maximum, jnp.max)
        if ci == 0:
            m_new = smax
            p = jnp.exp2(s - m_new)
            acc[g] = jnp.dot(vt, p.astype(BF16), preferred_element_type=F32)
        else:
            m_new = jnp.maximum(m[g], smax)
            alpha = jnp.exp2(m[g] - m_new)
            p = jnp.exp2(s - m_new)
            acc[g] = alpha * acc[g] + jnp.dot(vt, p.astype(BF16), preferred_element_type=F32)
        m[g] = m_new
    for g in range(n):
        o_ref[o_parts[g]] = (acc[g][:dv] / acc[g][dv:dv + 1]).T.astype(o_ref.dtype)


def _flash(q, kc, vct, k, vt, *, group, dq, dv, tq, tk, name, depth=4):
    bn, s, qw = q.shape
    kvh = qw // (group * dq)
    c = kc.shape[1]
    n_lat = 0 if k is None else k.shape[1] // tk
    in_specs = [pl.BlockSpec((None, tq, group * dq), lambda b, h, i: (b, i, h)),
                pl.BlockSpec((None, c, dq), lambda b, h, i: (b, 0, h)),
                pl.BlockSpec((None, V_ROWS, c), lambda b, h, i: (b, h, 0))]
    args = [q, kc, vct]
    if n_lat:
        t = k.shape[1]
        in_specs += [pl.BlockSpec((None, t, dq), lambda b, h, i: (b, 0, h)),
                     pl.BlockSpec((None, V_ROWS, t), lambda b, h, i: (b, h, 0))]
        args += [k, vt]
    return pl.pallas_call(
        functools.partial(_flash_body, group=group, dq=dq, dv=dv, tk=tk, n_lat=n_lat, depth=depth),
        grid=(bn, kvh, s // tq),
        in_specs=in_specs,
        out_specs=pl.BlockSpec((None, tq, group * dv), lambda b, h, i: (b, i, h)),
        out_shape=jax.ShapeDtypeStruct((bn, s, kvh * group * dv), BF16),
        compiler_params=_cparams(("parallel", "parallel", "parallel")),
        name=name,
    )(*args)


def _proj_res_body(*refs, n_lhs, final):
    x_ref, g_ref = refs[0], refs[1]
    lhs = refs[2:2 + n_lhs]
    ws = refs[2 + n_lhs:2 + 2 * n_lhs]
    rest = refs[2 + 2 * n_lhs:]
    y = jnp.dot(lhs[0][...], ws[0][...], preferred_element_type=F32)
    for a, w in zip(lhs[1:], ws[1:]):
        y = y + jnp.dot(a[...], w[...], preferred_element_type=F32)
    out = x_ref[...] + g_ref[...] * y
    if final:
        fw_ref, o_ref = rest
        out = _rms(out) * fw_ref[...]
    else:
        (o_ref,) = rest
    o_ref[...] = out


def _proj_res(x, mod, layer, part, ctx, lhs, ws, tm, final_w=None, name="proj_res"):
    bn, s, _ = x.shape
    row = lambda w: pl.BlockSpec((None, tm, w), lambda b, i: (b, i, 0))
    in_specs = [row(D_MODEL), _mod_spec(layer, part, ctx)]
    in_specs += [row(a.shape[-1]) for a in lhs] + [_const_spec(w.shape) for w in ws]
    args = [x, mod] + list(lhs) + list(ws)
    if final_w is not None:
        in_specs.append(_const_spec((1, D_MODEL)))
        args.append(final_w)
    return pl.pallas_call(
        functools.partial(_proj_res_body, n_lhs=len(lhs), final=final_w is not None),
        grid=(bn, s // tm),
        in_specs=in_specs,
        out_specs=row(D_MODEL),
        out_shape=jax.ShapeDtypeStruct(x.shape, F32),
        compiler_params=_cparams(("parallel", "parallel")),
        name=name + ("_ctx" if ctx else ""),
    )(*args)


def _up_body(*refs, mode, tm, probe=False):
    x_ref, xp_ref, xn_ref, sh_ref, sc_ref, nw_ref = refs[:6]
    i = pl.program_id(1)
    j = pl.program_id(2)
    if mode == "ffn":
        wg_ref, wu_ref, cg_ref, cu_ref, o_ref, h_scr, g_scr, u_scr = refs[6:]
    else:
        wb_ref, wc_ref, wv_ref, cw_ref, o_ref, h_scr, g_scr = refs[6:]

    @pl.when(j == 0)
    def _():
        gain = nw_ref[...] * (1.0 + sc_ref[...])
        sh = sh_ref[...]
        for r in range(0, tm, NORM_ROWS):
            h_scr[HALO + r:HALO + r + NORM_ROWS, :] = (
                _rms(x_ref[r:r + NORM_ROWS, :]) * gain + sh).astype(BF16)
        hp = _rms(xp_ref[...]) * gain + sh
        h_scr[0:HALO, :] = jnp.where(i == 0, 0.0, hp).astype(BF16)
        hn = _rms(xn_ref[...]) * gain + sh
        h_scr[HALO + tm:2 * HALO + tm, :] = jnp.where(i == pl.num_programs(1) - 1, 0.0, hn).astype(BF16)

    def conv(scr, cw_ref):
        return (cw_ref[0:1, :] * scr[HALO - 1:HALO - 1 + tm, :]
                + cw_ref[1:2, :] * scr[HALO:HALO + tm, :]
                + cw_ref[2:3, :] * scr[HALO + 1:HALO + 1 + tm, :])

    h = h_scr[...]
    if mode == "ffn":
        def main():
            g_scr[...] = jnp.dot(h_scr[...], wg_ref[...], preferred_element_type=F32)
            u_scr[...] = jnp.dot(h_scr[...], wu_ref[...], preferred_element_type=F32)
            g = conv(g_scr, cg_ref)
            u = conv(u_scr, cu_ref)
            o_ref[...] = ((g / (1.0 + jnp.exp(-g))) * u).astype(o_ref.dtype)
        if probe:
            for parity in (0, 1):
                pl.when(j % 2 == parity)(main)
        else:
            main()
    else:
        g_scr[...] = (jnp.dot(h, wc_ref[...], preferred_element_type=F32)
                      * jnp.dot(h, wv_ref[...], preferred_element_type=F32))
        b = jnp.dot(h_scr[HALO:HALO + tm, :], wb_ref[...], preferred_element_type=F32)
        o_ref[...] = (b * conv(g_scr, cw_ref)).astype(o_ref.dtype)


def _up(x, mod, layer, ctx, nw, w, cw, mode, tm, tn, probe=False):
    bn, s, _ = x.shape
    parts = 2 if mode == "ffn" else 3
    width = w.shape[1] // parts
    nj = width // tn
    hb = tm // HALO
    last_hb = s // HALO - 1
    in_specs = [
        pl.BlockSpec((None, tm, D_MODEL), lambda b, i, j: (b, i, 0)),
        pl.BlockSpec((None, HALO, D_MODEL), lambda b, i, j: (b, jnp.maximum(i * hb - 1, 0), 0)),
        pl.BlockSpec((None, HALO, D_MODEL), lambda b, i, j: (b, jnp.minimum((i + 1) * hb, last_hb), 0)),
        _mod_spec(layer, 3 if mode == "ffn" else 0, ctx),
        _mod_spec(layer, 4 if mode == "ffn" else 1, ctx),
        _const_spec((1, D_MODEL)),
    ]
    args = [x, x, x, mod, mod, nw]
    wspec = lambda p: pl.BlockSpec((D_MODEL, tn), lambda b, i, j: (0, p * nj + j))
    cspec = lambda p: pl.BlockSpec((3, tn), lambda b, i, j: (0, p * nj + j))
    rows = tm + 2 * HALO
    scratch = [pltpu.VMEM((rows, D_MODEL), BF16), pltpu.VMEM((rows, tn), F32)]
    if mode == "ffn":
        in_specs += [wspec(0), wspec(1), cspec(0), cspec(1)]
        args += [w, w, cw, cw]
        scratch.append(pltpu.VMEM((rows, tn), F32))
    else:
        in_specs += [wspec(0), wspec(1), wspec(2), cspec(0)]
        args += [w, w, w, cw]
    return pl.pallas_call(
        functools.partial(_up_body, mode=mode, tm=tm, probe=probe),
        grid=(bn, s // tm, nj),
        in_specs=in_specs,
        out_specs=pl.BlockSpec((None, tm, tn), lambda b, i, j: (b, i, j)),
        out_shape=jax.ShapeDtypeStruct((bn, s, width), BF16),
        scratch_shapes=scratch,
        compiler_params=_cparams(("parallel", "parallel", "arbitrary")),
        name=mode + "_up" + ("_ctx" if ctx else ""),
    )(*args)


def _rope_tables(seq, rot_dim):
    rows = seq // GRID_W
    r = jnp.repeat(jnp.arange(rows, dtype=F32), GRID_W)
    col = jnp.tile(jnp.arange(GRID_W, dtype=F32), rows)
    quarter = rot_dim // 4
    inv = ROPE_THETA ** (-jnp.arange(quarter, dtype=F32) / quarter)
    ar = r[:, None] * inv
    ac = col[:, None] * inv
    ang = jnp.concatenate([ar, ar, ac, ac], axis=-1)
    reps = 128 // rot_dim
    cos = jnp.tile(jnp.cos(ang), (1, reps))
    sin = jnp.tile(jnp.sin(ang), (1, reps))
    first = ((jnp.arange(128) // quarter) % 2 == 0)[None, :]
    return cos, jnp.where(first, -sin, 0.0), jnp.where(first, 0.0, sin)


def _attn_weights(w_in, qn, kn, mqn, mkvn, w_uq, w_ukv):
    w_in = jnp.pad(w_in, ((0, 0), (0, W_IN_PAD - w_in.shape[1]))).astype(BF16)
    uq = w_uq.reshape(B_Q_RANK, B_HEADS, B_NOPE_DIM + B_ROPE_DIM)
    uq_rope = jnp.pad(uq[:, :, B_NOPE_DIM:], ((0, 0), (0, 0), (0, 128 - B_ROPE_DIM)))
    w_uq = jnp.concatenate([uq[:, :, :B_NOPE_DIM].reshape(B_Q_RANK, -1),
                            uq_rope.reshape(B_Q_RANK, -1)], axis=1).astype(BF16)
    ukv = w_ukv.reshape(B_KV_RANK, B_HEADS, B_NOPE_DIM + B_V_DIM)
    w_ukv = jnp.concatenate([ukv[:, :, :B_NOPE_DIM].reshape(B_KV_RANK, -1),
                             ukv[:, :, B_NOPE_DIM:].reshape(B_KV_RANK, -1)], axis=1).astype(BF16)
    return (w_in, qn.reshape(1, -1), kn.reshape(1, -1), mqn.reshape(1, -1), mkvn.reshape(1, -1),
            w_uq, w_ukv)


def kernel(x, c, ctx, c_ctx, w_ada, b_ada, norm_mix, norm_ffn, attn_w_in, attn_q_norm, attn_k_norm,
           mla_q_norm, mla_kv_norm, mla_w_uq, mla_w_ukv, attn_w_o, sc_w_in, sc_conv, sc_w_out,
           ffn_w_up, ffn_conv, ffn_w_down, final_norm):
    bn, s, _ = x.shape
    cl = ctx.shape[1]
    assert bn <= MOD_ROWS // 2
    cvec = jnp.zeros((MOD_ROWS, D_MODEL), F32).at[:bn].set(c).at[MOD_ROWS // 2].set(c_ctx)
    mod = _adaln(cvec, w_ada, b_ada)
    rope = _rope_tables(s, A_HEAD_DIM) + _rope_tables(s, B_ROPE_DIM)

    xc = ctx
    for l in range(DEPTH):
        later_attn = any(j % 2 == 0 for j in range(l + 1, DEPTH))
        i = l // 2
        nw = norm_mix[l].reshape(1, -1)
        if l % 2 == 0:
            wts = _attn_weights(attn_w_in[i], attn_q_norm[i], attn_k_norm[i], mla_q_norm[i],
                                mla_kv_norm[i], mla_w_uq[i], mla_w_ukv[i])
            w_o = attn_w_o[i].astype(BF16)
            w_oa, w_ob = w_o[:A_HEADS * A_HEAD_DIM], w_o[A_HEADS * A_HEAD_DIM:]
            qa_c, qb_c, ka_c, va_c, kb_c, vb_c = _attn_proj(xc, mod, l, True, nw, wts, None, later_attn, cl)
            qa, qb, ka, va, kb, vb = _attn_proj(x, mod, l, False, nw, wts, rope, True, (256, 512)[i])
            oa = _flash(qa, ka_c, va_c, ka, va, group=A_GROUP, dq=A_HEAD_DIM, dv=A_HEAD_DIM,
                        tq=512, tk=512, name="gqa", depth=6)
            ob = _flash(qb, kb_c, vb_c, kb, vb, group=1, dq=B_QK_PAD, dv=B_V_DIM,
                        tq=2048, tk=512, name="mla", depth=6)
            x = _proj_res(x, mod, l, 2, False, [oa, ob], [w_oa, w_ob], 512, name="attn_out")
            if later_attn:
                oa_c = _flash(qa_c, ka_c, va_c, None, None, group=A_GROUP, dq=A_HEAD_DIM, dv=A_HEAD_DIM,
                              tq=cl, tk=0, name="gqa_ctx")
                ob_c = _flash(qb_c, kb_c, vb_c, None, None, group=1, dq=B_QK_PAD, dv=B_V_DIM,
                              tq=cl, tk=0, name="mla_ctx")
                xc = _proj_res(xc, mod, l, 2, True, [oa_c, ob_c], [w_oa, w_ob], cl, name="attn_out")
        else:
            w_in = sc_w_in[i].astype(BF16)
            w_out = sc_w_out[i].astype(BF16)
            z = _up(x, mod, l, False, nw, w_in, sc_conv[i], "sc", 1024, 512)
            x = _proj_res(x, mod, l, 2, False, [z], [w_out], 512, name="sc_out")
            if later_attn:
                zc = _up(xc, mod, l, True, nw, w_in, sc_conv[i], "sc", cl, 512)
                xc = _proj_res(xc, mod, l, 2, True, [zc], [w_out], cl, name="sc_out")
        nwf = norm_ffn[l].reshape(1, -1)
        w_up = ffn_w_up[l].astype(BF16)
        w_down = ffn_w_down[l].astype(BF16)
        hid = _up(x, mod, l, False, nwf, w_up, ffn_conv[l], "ffn", 1024, 512, probe=(l == 0))
        fw = final_norm.reshape(1, -1) if l == DEPTH - 1 else None
        x = _proj_res(x, mod, l, 5, False, [hid], [w_down], 256, final_w=fw, name="ffn_down")
        if later_attn:
            hid_c = _up(xc, mod, l, True, nwf, w_up, ffn_conv[l], "ffn", cl, 512)
            xc = _proj_res(xc, mod, l, 5, True, [hid_c], [w_down], cl, name="ffn_down")
    return x
```

```python
import functools

import jax
import jax.numpy as jnp
from jax import lax
from jax.experimental import pallas as pl
from jax.experimental.pallas import tpu as pltpu

F32 = jnp.float32
BF16 = jnp.bfloat16

D_MODEL = 2048
DEPTH = 4
GRID_W = 64
ROPE_THETA = 10000.0
EPS = 1e-6
A_HEADS = 8
A_KV_HEADS = 2
A_GROUP = A_HEADS // A_KV_HEADS
A_HEAD_DIM = 128
B_HEADS = 8
B_Q_RANK = 512
B_KV_RANK = 256
B_NOPE_DIM = 128
B_ROPE_DIM = 64
B_V_DIM = 128
B_QK_PAD = 256
LOG2E = 1.4426950408889634
A_SCALE = A_HEAD_DIM ** -0.5 * LOG2E
B_SCALE = (B_NOPE_DIM + B_ROPE_DIM) ** -0.5 * LOG2E
QGROUP = 256
V_ROWS = 128 + 16
SC_WIDTH = D_MODEL
D_FF = 256 * ((8 * D_MODEL // 3 + 255) // 256)
W_IN_PAD = 2432
MOD_ROWS = 16
HALO = 16
NORM_ROWS = 32
VMEM_LIMIT = 56 * 1024 * 1024


def _cparams(sem):
    return pltpu.CompilerParams(dimension_semantics=sem, vmem_limit_bytes=VMEM_LIMIT)


def _const_spec(shape):
    nd = len(shape)
    return pl.BlockSpec(shape, lambda *g: (0,) * nd, pipeline_mode=pl.Buffered(1))


def _mod_spec(layer, part, ctx):
    if ctx:
        idx = lambda b, *g: (layer, MOD_ROWS // 2, part, 0, 0)
    else:
        idx = lambda b, *g: (layer, b, part, 0, 0)
    return pl.BlockSpec((None, None, None, 1, D_MODEL), idx)


def _rms(x):
    return x * lax.rsqrt(jnp.mean(x * x, axis=-1, keepdims=True) + EPS)


def _rope(x, cos, sin_up, sin_dn, quarter):
    n = x.shape[-1]
    return x * cos + pltpu.roll(x, n - quarter, 1) * sin_up + pltpu.roll(x, quarter, 1) * sin_dn


def _adaln_body(c_ref, w_ref, b_ref, o_ref):
    c = c_ref[...]
    s = (c / (1.0 + jnp.exp(-c))).astype(BF16)
    o_ref[...] = jnp.dot(s, w_ref[...].astype(BF16), preferred_element_type=F32) + b_ref[...]


def _adaln(cvec, w_ada, b_ada):
    tn = 1024
    n = 6 * D_MODEL
    out = pl.pallas_call(
        _adaln_body,
        grid=(DEPTH, n // tn),
        in_specs=[
            pl.BlockSpec((MOD_ROWS, D_MODEL), lambda l, j: (0, 0)),
            pl.BlockSpec((None, D_MODEL, tn), lambda l, j: (l, 0, j)),
            pl.BlockSpec((None, 1, tn), lambda l, j: (l, 0, j)),
        ],
        out_specs=pl.BlockSpec((None, MOD_ROWS, tn), lambda l, j: (l, 0, j)),
        out_shape=jax.ShapeDtypeStruct((DEPTH, MOD_ROWS, n), F32),
        compiler_params=_cparams(("parallel", "parallel")),
        name="adaln",
    )(cvec, w_ada, b_ada.reshape(DEPTH, 1, n))
    return out.reshape(DEPTH, MOD_ROWS, 6, 1, D_MODEL)


def _attn_proj_body(*refs, use_rope, with_q):
    it = iter(refs)
    x_ref, sh_ref, sc_ref, nw_ref, win_ref = (next(it) for _ in range(5))
    qn_ref, kn_ref, mqn_ref, mkvn_ref, wuq_ref, wukv_ref = (next(it) for _ in range(6))
    if use_rope:
        ca, sau, sad, cb, sbu, sbd = (next(it)[...] for _ in range(6))
    if with_q:
        qa_ref, qb_ref = next(it), next(it)
    ka_ref, va_ref, kb_ref, vb_ref, h_scr = (next(it) for _ in range(5))

    def rope_a(v):
        return _rope(v, ca, sau, sad, A_HEAD_DIM // 4) if use_rope else v

    def rope_b(v):
        return _rope(v, cb, sbu, sbd, B_ROPE_DIM // 4) if use_rope else v

    gain = nw_ref[...] * (1.0 + sc_ref[...])
    sh = sh_ref[...]
    tm = x_ref.shape[0]
    for r in range(0, tm, NORM_ROWS):
        h_scr[r:r + NORM_ROWS, :] = (_rms(x_ref[r:r + NORM_ROWS, :]) * gain + sh).astype(BF16)
    proj = jnp.dot(h_scr[...], win_ref[...], preferred_element_type=F32)
    q_end = A_HEADS * A_HEAD_DIM
    k_end = q_end + A_KV_HEADS * A_HEAD_DIM
    v_end = k_end + A_KV_HEADS * A_HEAD_DIM
    cq_end = v_end + B_Q_RANK
    ckv_end = cq_end + B_KV_RANK
    nope_w = B_HEADS * B_NOPE_DIM

    ckv = (_rms(proj[:, cq_end:ckv_end]) * mkvn_ref[...]).astype(BF16)
    kv = jnp.dot(ckv, wukv_ref[...], preferred_element_type=F32)
    if with_q:
        cq = (_rms(proj[:, v_end:cq_end]) * mqn_ref[...]).astype(BF16)
        qb = jnp.dot(cq, wuq_ref[...], preferred_element_type=F32)

    def put_vt(ref, hh, v):
        lo = hh * V_ROWS
        ref[lo:lo + v.shape[1], :] = v.T.astype(BF16)
        ref[lo + v.shape[1]:lo + V_ROWS, :] = jnp.ones((V_ROWS - v.shape[1], v.shape[0]), BF16)

    kn = kn_ref[...]
    for kh in range(A_KV_HEADS):
        lo = q_end + kh * A_HEAD_DIM
        k = rope_a(_rms(proj[:, lo:lo + A_HEAD_DIM]) * kn)
        ka_ref[:, kh * A_HEAD_DIM:(kh + 1) * A_HEAD_DIM] = k.astype(BF16)
        lo = k_end + kh * A_HEAD_DIM
        put_vt(va_ref, kh, proj[:, lo:lo + A_HEAD_DIM])
    if with_q:
        qn = qn_ref[...]
        for hh in range(A_HEADS):
            lo = hh * A_HEAD_DIM
            q = rope_a(_rms(proj[:, lo:lo + A_HEAD_DIM]) * qn) * A_SCALE
            qa_ref[:, lo:lo + A_HEAD_DIM] = q.astype(BF16)

    kr = rope_b(proj[:, ckv_end:W_IN_PAD]).astype(BF16)
    for hh in range(B_HEADS):
        put_vt(vb_ref, hh, kv[:, nope_w + hh * B_V_DIM:nope_w + (hh + 1) * B_V_DIM])
        lo = hh * B_QK_PAD
        kb_ref[:, lo:lo + B_NOPE_DIM] = kv[:, hh * B_NOPE_DIM:(hh + 1) * B_NOPE_DIM].astype(BF16)
        kb_ref[:, lo + B_NOPE_DIM:lo + B_QK_PAD] = kr
    if with_q:
        for hh in range(B_HEADS):
            lo = hh * B_QK_PAD
            qb_ref[:, lo:lo + B_NOPE_DIM] = (qb[:, hh * B_NOPE_DIM:(hh + 1) * B_NOPE_DIM] * B_SCALE).astype(BF16)
            r = rope_b(qb[:, nope_w + hh * 128:nope_w + (hh + 1) * 128]) * B_SCALE
            qb_ref[:, lo + B_NOPE_DIM:lo + B_QK_PAD] = r.astype(BF16)


def _attn_proj(x, mod, layer, ctx, nw, wts, rope, with_q, tm):
    bn, s, _ = x.shape
    use_rope = rope is not None
    w_in, qn, kn, mqn, mkvn, w_uq, w_ukv = wts
    row = lambda w: pl.BlockSpec((None, tm, w), lambda b, i: (b, i, 0))
    in_specs = [row(D_MODEL), _mod_spec(layer, 0, ctx), _mod_spec(layer, 1, ctx),
                _const_spec((1, D_MODEL)), _const_spec(w_in.shape),
                _const_spec(qn.shape), _const_spec(kn.shape), _const_spec(mqn.shape),
                _const_spec(mkvn.shape), _const_spec(w_uq.shape), _const_spec(w_ukv.shape)]
    args = [x, mod, mod, nw, w_in, qn, kn, mqn, mkvn, w_uq, w_ukv]
    if use_rope:
        in_specs += [pl.BlockSpec((tm, 128), lambda b, i: (i, 0))] * 6
        args += list(rope)
    col = lambda w: pl.BlockSpec((None, w, tm), lambda b, i: (b, 0, i))
    outs_desc = ([(A_HEADS * A_HEAD_DIM, False), (B_HEADS * B_QK_PAD, False)] if with_q else []) + [
        (A_KV_HEADS * A_HEAD_DIM, False), (A_KV_HEADS * V_ROWS, True),
        (B_HEADS * B_QK_PAD, False), (B_HEADS * V_ROWS, True)]
    outs = pl.pallas_call(
        functools.partial(_attn_proj_body, use_rope=use_rope, with_q=with_q),
        grid=(bn, s // tm),
        in_specs=in_specs,
        out_specs=[col(w) if t else row(w) for w, t in outs_desc],
        out_shape=[jax.ShapeDtypeStruct((bn, w, s) if t else (bn, s, w), BF16) for w, t in outs_desc],
        scratch_shapes=[pltpu.VMEM((tm, D_MODEL), BF16)],
        compiler_params=_cparams(("parallel", "parallel")),
        name="attn_proj_ctx" if ctx else "attn_proj",
    )(*args)
    return outs if with_q else [None, None] + list(outs)


def _col_reduce(x, op, final):
    while x.shape[0] > 8 and x.shape[0] % 16 == 0:
        h = x.shape[0] // 2
        x = op(x[:h], x[h:])
    return final(x, axis=0, keepdims=True)


def _flash_body(*refs, group, dq, dv, tk, n_lat, depth):
    if n_lat:
        q_ref, kc_ref, vct_ref, k_ref, vt_ref, o_ref = refs
    else:
        q_ref, kc_ref, vct_ref, o_ref = refs
    tq = q_ref.shape[0]
    rows = [slice(r, r + QGROUP) for r in range(0, tq, QGROUP)]
    q_parts = [(r, slice(g * dq, (g + 1) * dq)) for r in rows for g in range(group)]
    o_parts = [(r, slice(g * dv, (g + 1) * dv)) for r in rows for g in range(group)]
    n = len(q_parts)
    chunks = [(kc_ref, vct_ref, 0, kc_ref.shape[0])] + [(k_ref, vt_ref, c * tk, tk) for c in range(n_lat)]
    m, acc = [None] * n, [None] * n
    blocks = [(ci, g) for ci in range(len(chunks)) for g in range(n)]

    def scores(ci, g):
        kr, _, off, size = chunks[ci]
        return lax.dot_general(kr[off:off + size, :], q_ref[q_parts[g]], (((1,), (1,)), ((), ())),
                               preferred_element_type=F32)

    s_next = [scores(*blocks[d]) for d in range(min(depth, len(blocks)))]
    for idx, (ci, g) in enumerate(blocks):
        s = s_next.pop(0)
        if idx + depth < len(blocks):
            s_next.append(scores(*blocks[idx + depth]))
        _, vr, off, size = chunks[ci]
        vt = vr[:, off:off + size]
        smax = _col_reduce(s, jnp.maximum, jnp.max)
        if ci == 0:
            m_new = smax
            p = jnp.exp2(s - m_new)
            acc[g] = jnp.dot(vt, p.astype(BF16), preferred_element_type=F32)
        else:
            m_new = jnp.maximum(m[g], smax)
            alpha = jnp.exp2(m[g] - m_new)
            p = jnp.exp2(s - m_new)
            acc[g] = alpha * acc[g] + jnp.dot(vt, p.astype(BF16), preferred_element_type=F32)
        m[g] = m_new
    for g in range(n):
        o_ref[o_parts[g]] = (acc[g][:dv] / acc[g][dv:dv + 1]).T.astype(o_ref.dtype)


def _flash(q, kc, vct, k, vt, *, group, dq, dv, tq, tk, name, depth=4):
    bn, s, qw = q.shape
    kvh = qw // (group * dq)
    c = kc.shape[1]
    n_lat = 0 if k is None else k.shape[1] // tk
    in_specs = [pl.BlockSpec((None, tq, group * dq), lambda b, h, i: (b, i, h)),
                pl.BlockSpec((None, c, dq), lambda b, h, i: (b, 0, h)),
                pl.BlockSpec((None, V_ROWS, c), lambda b, h, i: (b, h, 0))]
    args = [q, kc, vct]
    if n_lat:
        t = k.shape[1]
        in_specs += [pl.BlockSpec((None, t, dq), lambda b, h, i: (b, 0, h)),
                     pl.BlockSpec((None, V_ROWS, t), lambda b, h, i: (b, h, 0))]
        args += [k, vt]
    return pl.pallas_call(
        functools.partial(_flash_body, group=group, dq=dq, dv=dv, tk=tk, n_lat=n_lat, depth=depth),
        grid=(bn, kvh, s // tq),
        in_specs=in_specs,
        out_specs=pl.BlockSpec((None, tq, group * dv), lambda b, h, i: (b, i, h)),
        out_shape=jax.ShapeDtypeStruct((bn, s, kvh * group * dv), BF16),
        compiler_params=_cparams(("parallel", "parallel", "parallel")),
        name=name,
    )(*args)


def _proj_res_body(*refs, n_lhs, final):
    x_ref, g_ref = refs[0], refs[1]
    lhs = refs[2:2 + n_lhs]
    ws = refs[2 + n_lhs:2 + 2 * n_lhs]
    rest = refs[2 + 2 * n_lhs:]
    y = jnp.dot(lhs[0][...], ws[0][...], preferred_element_type=F32)
    for a, w in zip(lhs[1:], ws[1:]):
        y = y + jnp.dot(a[...], w[...], preferred_element_type=F32)
    out = x_ref[...] + g_ref[...] * y
    if final:
        fw_ref, o_ref = rest
        out = _rms(out) * fw_ref[...]
    else:
        (o_ref,) = rest
    o_ref[...] = out


def _proj_res(x, mod, layer, part, ctx, lhs, ws, tm, final_w=None, name="proj_res"):
    bn, s, _ = x.shape
    row = lambda w: pl.BlockSpec((None, tm, w), lambda b, i: (b, i, 0))
    in_specs = [row(D_MODEL), _mod_spec(layer, part, ctx)]
    in_specs += [row(a.shape[-1]) for a in lhs] + [_const_spec(w.shape) for w in ws]
    args = [x, mod] + list(lhs) + list(ws)
    if final_w is not None:
        in_specs.append(_const_spec((1, D_MODEL)))
        args.append(final_w)
    return pl.pallas_call(
        functools.partial(_proj_res_body, n_lhs=len(lhs), final=final_w is not None),
        grid=(bn, s // tm),
        in_specs=in_specs,
        out_specs=row(D_MODEL),
        out_shape=jax.ShapeDtypeStruct(x.shape, F32),
        compiler_params=_cparams(("parallel", "parallel")),
        name=name + ("_ctx" if ctx else ""),
    )(*args)


def _up_body(*refs, mode, tm, seg):
    x_ref, xp_ref, xn_ref, sh_ref, sc_ref, nw_ref = refs[:6]
    i = pl.program_id(1)
    j = pl.program_id(2)
    if mode == "ffn":
        wg_ref, wu_ref, cg_ref, cu_ref, o_ref, h_scr, g_scr, u_scr = refs[6:]
    else:
        wb_ref, wc_ref, wv_ref, cw_ref, o_ref, h_scr, g_scr = refs[6:]
    whole = seg >= tm

    @pl.when(j == 0)
    def _():
        gain = nw_ref[...] * (1.0 + sc_ref[...])
        sh = sh_ref[...]
        for r in range(0, tm, NORM_ROWS):
            h_scr[HALO + r:HALO + r + NORM_ROWS, :] = (
                _rms(x_ref[r:r + NORM_ROWS, :]) * gain + sh).astype(BF16)
        first = (i * tm) % seg == 0 if whole else True
        last = ((i + 1) * tm) % seg == 0 if whole else True
        hp = _rms(xp_ref[...]) * gain + sh
        h_scr[0:HALO, :] = jnp.where(first, 0.0, hp).astype(BF16)
        hn = _rms(xn_ref[...]) * gain + sh
        h_scr[HALO + tm:2 * HALO + tm, :] = jnp.where(last, 0.0, hn).astype(BF16)

    def conv(scr, cw_ref):
        prev = scr[HALO - 1:HALO - 1 + tm, :]
        nxt = scr[HALO + 1:HALO + 1 + tm, :]
        if not whole:
            pos = lax.broadcasted_iota(jnp.int32, (tm, 1), 0) % seg
            prev = jnp.where(pos == 0, 0.0, prev)
            nxt = jnp.where(pos == seg - 1, 0.0, nxt)
        return cw_ref[0:1, :] * prev + cw_ref[1:2, :] * scr[HALO:HALO + tm, :] + cw_ref[2:3, :] * nxt

    h = h_scr[...]
    if mode == "ffn":
        g_scr[...] = jnp.dot(h, wg_ref[...], preferred_element_type=F32)
        u_scr[...] = jnp.dot(h, wu_ref[...], preferred_element_type=F32)
        g = conv(g_scr, cg_ref)
        u = conv(u_scr, cu_ref)
        o_ref[...] = ((g / (1.0 + jnp.exp(-g))) * u).astype(o_ref.dtype)
    else:
        g_scr[...] = (jnp.dot(h, wc_ref[...], preferred_element_type=F32)
                      * jnp.dot(h, wv_ref[...], preferred_element_type=F32))
        b = jnp.dot(h_scr[HALO:HALO + tm, :], wb_ref[...], preferred_element_type=F32)
        o_ref[...] = (b * conv(g_scr, cw_ref)).astype(o_ref.dtype)


def _up(x, mod, layer, ctx, nw, w, cw, mode, tm, tn):
    seg = x.shape[1]
    if ctx:
        x = x.reshape(1, -1, D_MODEL)
    bn, s, _ = x.shape
    parts = 2 if mode == "ffn" else 3
    width = w.shape[1] // parts
    nj = width // tn
    hb = tm // HALO
    last_hb = s // HALO - 1
    in_specs = [
        pl.BlockSpec((None, tm, D_MODEL), lambda b, i, j: (b, i, 0)),
        pl.BlockSpec((None, HALO, D_MODEL), lambda b, i, j: (b, jnp.maximum(i * hb - 1, 0), 0)),
        pl.BlockSpec((None, HALO, D_MODEL), lambda b, i, j: (b, jnp.minimum((i + 1) * hb, last_hb), 0)),
        _mod_spec(layer, 3 if mode == "ffn" else 0, ctx),
        _mod_spec(layer, 4 if mode == "ffn" else 1, ctx),
        _const_spec((1, D_MODEL)),
    ]
    args = [x, x, x, mod, mod, nw]
    wspec = lambda p: pl.BlockSpec((D_MODEL, tn), lambda b, i, j: (0, p * nj + j))
    cspec = lambda p: pl.BlockSpec((3, tn), lambda b, i, j: (0, p * nj + j))
    rows = tm + 2 * HALO
    scratch = [pltpu.VMEM((rows, D_MODEL), BF16), pltpu.VMEM((rows, tn), F32)]
    if mode == "ffn":
        in_specs += [wspec(0), wspec(1), cspec(0), cspec(1)]
        args += [w, w, cw, cw]
        scratch.append(pltpu.VMEM((rows, tn), F32))
    else:
        in_specs += [wspec(0), wspec(1), wspec(2), cspec(0)]
        args += [w, w, w, cw]
    out = pl.pallas_call(
        functools.partial(_up_body, mode=mode, tm=tm, seg=seg),
        grid=(bn, s // tm, nj),
        in_specs=in_specs,
        out_specs=pl.BlockSpec((None, tm, tn), lambda b, i, j: (b, i, j)),
        out_shape=jax.ShapeDtypeStruct((bn, s, width), BF16),
        scratch_shapes=scratch,
        compiler_params=_cparams(("parallel", "parallel", "arbitrary")),
        name=mode + "_up" + ("_ctx" if ctx else ""),
    )(*args)
    return out.reshape(-1, seg, width)


def _rope_tables(seq, rot_dim):
    rows = seq // GRID_W
    r = jnp.repeat(jnp.arange(rows, dtype=F32), GRID_W)
    col = jnp.tile(jnp.arange(GRID_W, dtype=F32), rows)
    quarter = rot_dim // 4
    inv = ROPE_THETA ** (-jnp.arange(quarter, dtype=F32) / quarter)
    ar = r[:, None] * inv
    ac = col[:, None] * inv
    ang = jnp.concatenate([ar, ar, ac, ac], axis=-1)
    reps = 128 // rot_dim
    cos = jnp.tile(jnp.cos(ang), (1, reps))
    sin = jnp.tile(jnp.sin(ang), (1, reps))
    first = ((jnp.arange(128) // quarter) % 2 == 0)[None, :]
    return cos, jnp.where(first, -sin, 0.0), jnp.where(first, 0.0, sin)


def _attn_weights(w_in, qn, kn, mqn, mkvn, w_uq, w_ukv):
    w_in = jnp.pad(w_in, ((0, 0), (0, W_IN_PAD - w_in.shape[1]))).astype(BF16)
    uq = w_uq.reshape(B_Q_RANK, B_HEADS, B_NOPE_DIM + B_ROPE_DIM)
    uq_rope = jnp.pad(uq[:, :, B_NOPE_DIM:], ((0, 0), (0, 0), (0, 128 - B_ROPE_DIM)))
    w_uq = jnp.concatenate([uq[:, :, :B_NOPE_DIM].reshape(B_Q_RANK, -1),
                            uq_rope.reshape(B_Q_RANK, -1)], axis=1).astype(BF16)
    ukv = w_ukv.reshape(B_KV_RANK, B_HEADS, B_NOPE_DIM + B_V_DIM)
    w_ukv = jnp.concatenate([ukv[:, :, :B_NOPE_DIM].reshape(B_KV_RANK, -1),
                             ukv[:, :, B_NOPE_DIM:].reshape(B_KV_RANK, -1)], axis=1).astype(BF16)
    return (w_in, qn.reshape(1, -1), kn.reshape(1, -1), mqn.reshape(1, -1), mkvn.reshape(1, -1),
            w_uq, w_ukv)


def kernel(x, c, ctx, c_ctx, w_ada, b_ada, norm_mix, norm_ffn, attn_w_in, attn_q_norm, attn_k_norm,
           mla_q_norm, mla_kv_norm, mla_w_uq, mla_w_ukv, attn_w_o, sc_w_in, sc_conv, sc_w_out,
           ffn_w_up, ffn_conv, ffn_w_down, final_norm):
    bn, s, _ = x.shape
    cl = ctx.shape[1]
    assert bn <= MOD_ROWS // 2
    cvec = jnp.zeros((MOD_ROWS, D_MODEL), F32).at[:bn].set(c).at[MOD_ROWS // 2].set(c_ctx)
    mod = _adaln(cvec, w_ada, b_ada)
    rope = _rope_tables(s, A_HEAD_DIM) + _rope_tables(s, B_ROPE_DIM)

    xc = ctx
    for l in range(DEPTH):
        later_attn = any(j % 2 == 0 for j in range(l + 1, DEPTH))
        i = l // 2
        nw = norm_mix[l].reshape(1, -1)
        if l % 2 == 0:
            wts = _attn_weights(attn_w_in[i], attn_q_norm[i], attn_k_norm[i], mla_q_norm[i],
                                mla_kv_norm[i], mla_w_uq[i], mla_w_ukv[i])
            w_o = attn_w_o[i].astype(BF16)
            w_oa, w_ob = w_o[:A_HEADS * A_HEAD_DIM], w_o[A_HEADS * A_HEAD_DIM:]
            qa_c, qb_c, ka_c, va_c, kb_c, vb_c = _attn_proj(xc, mod, l, True, nw, wts, None, later_attn, cl)
            qa, qb, ka, va, kb, vb = _attn_proj(x, mod, l, False, nw, wts, rope, True, 512)
            oa = _flash(qa, ka_c, va_c, ka, va, group=A_GROUP, dq=A_HEAD_DIM, dv=A_HEAD_DIM,
                        tq=512, tk=512, name="gqa", depth=6)
            ob = _flash(qb, kb_c, vb_c, kb, vb, group=1, dq=B_QK_PAD, dv=B_V_DIM,
                        tq=2048, tk=512, name="mla", depth=6)
            x = _proj_res(x, mod, l, 2, False, [oa, ob], [w_oa, w_ob], 512, name="attn_out")
            if later_attn:
                oa_c = _flash(qa_c, ka_c, va_c, None, None, group=A_GROUP, dq=A_HEAD_DIM, dv=A_HEAD_DIM,
                              tq=cl, tk=0, name="gqa_ctx")
                ob_c = _flash(qb_c, kb_c, vb_c, None, None, group=1, dq=B_QK_PAD, dv=B_V_DIM,
                              tq=cl, tk=0, name="mla_ctx")
                xc = _proj_res(xc, mod, l, 2, True, [oa_c, ob_c], [w_oa, w_ob], cl, name="attn_out")
        else:
            w_in = sc_w_in[i].astype(BF16)
            w_out = sc_w_out[i].astype(BF16)
            z = _up(x, mod, l, False, nw, w_in, sc_conv[i], "sc", 1024, 512)
            x = _proj_res(x, mod, l, 2, False, [z], [w_out], 512, name="sc_out")
            if later_attn:
                zc = _up(xc, mod, l, True, nw, w_in, sc_conv[i], "sc", 1024, 512)
                xc = _proj_res(xc, mod, l, 2, True, [zc], [w_out], cl, name="sc_out")
        nwf = norm_ffn[l].reshape(1, -1)
        w_up = ffn_w_up[l].astype(BF16)
        w_down = ffn_w_down[l].astype(BF16)
        hid = _up(x, mod, l, False, nwf, w_up, ffn_conv[l], "ffn", 1024, 512)
        fw = final_norm.reshape(1, -1) if l == DEPTH - 1 else None
        x = _proj_res(x, mod, l, 5, False, [hid], [w_down], (256, 512, 256, 256)[l], final_w=fw, name="ffn_down")
        if later_attn:
            hid_c = _up(xc, mod, l, True, nwf, w_up, ffn_conv[l], "ffn", 1024, 512)
            xc = _proj_res(xc, mod, l, 5, True, [hid_c], [w_down], cl, name="ffn_down")
    return x
```

```python
import functools

import jax
import jax.numpy as jnp
from jax import lax
from jax.experimental import pallas as pl
from jax.experimental.pallas import tpu as pltpu

F32 = jnp.float32
BF16 = jnp.bfloat16

D_MODEL = 2048
DEPTH = 4
GRID_W = 64
ROPE_THETA = 10000.0
EPS = 1e-6
A_HEADS = 8
A_KV_HEADS = 2
A_GROUP = A_HEADS // A_KV_HEADS
A_HEAD_DIM = 128
B_HEADS = 8
B_Q_RANK = 512
B_KV_RANK = 256
B_NOPE_DIM = 128
B_ROPE_DIM = 64
B_V_DIM = 128
B_QK_PAD = 256
LOG2E = 1.4426950408889634
A_SCALE = A_HEAD_DIM ** -0.5 * LOG2E
B_SCALE = (B_NOPE_DIM + B_ROPE_DIM) ** -0.5 * LOG2E
QGROUP = 256
V_ROWS = 128 + 16
SC_WIDTH = D_MODEL
D_FF = 256 * ((8 * D_MODEL // 3 + 255) // 256)
W_IN_PAD = 2432
MOD_ROWS = 16
HALO = 16
NORM_ROWS = 32
VMEM_LIMIT = 56 * 1024 * 1024


def _cparams(sem):
    return pltpu.CompilerParams(dimension_semantics=sem, vmem_limit_bytes=VMEM_LIMIT)


def _const_spec(shape):
    nd = len(shape)
    return pl.BlockSpec(shape, lambda *g: (0,) * nd, pipeline_mode=pl.Buffered(1))


def _layer_spec(shape, idx):
    return pl.BlockSpec((None,) + tuple(shape), lambda *g: idx, pipeline_mode=pl.Buffered(1))


def _mod_spec(layer, part, ctx):
    if ctx:
        idx = lambda b, *g: (layer, MOD_ROWS // 2, part, 0, 0)
    else:
        idx = lambda b, *g: (layer, b, part, 0, 0)
    return pl.BlockSpec((None, None, None, 1, D_MODEL), idx)


def _rms(x):
    return x * lax.rsqrt(jnp.mean(x * x, axis=-1, keepdims=True) + EPS)


def _rope(x, cos, sin_up, sin_dn, quarter):
    n = x.shape[-1]
    return x * cos + pltpu.roll(x, n - quarter, 1) * sin_up + pltpu.roll(x, quarter, 1) * sin_dn


def _adaln_body(c_ref, w_ref, b_ref, o_ref):
    c = c_ref[...]
    s = (c / (1.0 + jnp.exp(-c))).astype(BF16)
    o_ref[...] = jnp.dot(s, w_ref[...].astype(BF16), preferred_element_type=F32) + b_ref[...]


def _adaln(cvec, w_ada, b_ada):
    tn = 1024
    n = 6 * D_MODEL
    out = pl.pallas_call(
        _adaln_body,
        grid=(DEPTH, n // tn),
        in_specs=[
            pl.BlockSpec((MOD_ROWS, D_MODEL), lambda l, j: (0, 0)),
            pl.BlockSpec((None, D_MODEL, tn), lambda l, j: (l, 0, j)),
            pl.BlockSpec((None, 1, tn), lambda l, j: (l, 0, j)),
        ],
        out_specs=pl.BlockSpec((None, MOD_ROWS, tn), lambda l, j: (l, 0, j)),
        out_shape=jax.ShapeDtypeStruct((DEPTH, MOD_ROWS, n), F32),
        compiler_params=_cparams(("parallel", "parallel")),
        name="adaln",
    )(cvec, w_ada, b_ada.reshape(DEPTH, 1, n))
    return out.reshape(DEPTH, MOD_ROWS, 6, 1, D_MODEL)


def _attn_proj_body(*refs, use_rope, with_q):
    it = iter(refs)
    x_ref, sh_ref, sc_ref, nw_ref, win_ref = (next(it) for _ in range(5))
    qn_ref, kn_ref, mqn_ref, mkvn_ref, wuq_ref, wukv_ref = (next(it) for _ in range(6))
    if use_rope:
        ca, sau, sad, cb, sbu, sbd = (next(it)[...] for _ in range(6))
    if with_q:
        qa_ref, qb_ref = next(it), next(it)
    ka_ref, va_ref, kb_ref, vb_ref, h_scr = (next(it) for _ in range(5))

    def rope_a(v):
        return _rope(v, ca, sau, sad, A_HEAD_DIM // 4) if use_rope else v

    def rope_b(v):
        return _rope(v, cb, sbu, sbd, B_ROPE_DIM // 4) if use_rope else v

    gain = nw_ref[...] * (1.0 + sc_ref[...])
    sh = sh_ref[...]
    tm = x_ref.shape[0]
    for r in range(0, tm, NORM_ROWS):
        h_scr[r:r + NORM_ROWS, :] = (_rms(x_ref[r:r + NORM_ROWS, :]) * gain + sh).astype(BF16)
    proj = jnp.dot(h_scr[...], win_ref[...], preferred_element_type=F32)
    q_end = A_HEADS * A_HEAD_DIM
    k_end = q_end + A_KV_HEADS * A_HEAD_DIM
    v_end = k_end + A_KV_HEADS * A_HEAD_DIM
    cq_end = v_end + B_Q_RANK
    ckv_end = cq_end + B_KV_RANK
    nope_w = B_HEADS * B_NOPE_DIM

    ckv = (_rms(proj[:, cq_end:ckv_end]) * mkvn_ref[...]).astype(BF16)
    kv = jnp.dot(ckv, wukv_ref[...], preferred_element_type=F32)
    if with_q:
        cq = (_rms(proj[:, v_end:cq_end]) * mqn_ref[...]).astype(BF16)
        qb = jnp.dot(cq, wuq_ref[...], preferred_element_type=F32)

    def put_vt(ref, hh, v):
        lo = hh * V_ROWS
        ref[lo:lo + v.shape[1], :] = v.T.astype(BF16)
        ref[lo + v.shape[1]:lo + V_ROWS, :] = jnp.ones((V_ROWS - v.shape[1], v.shape[0]), BF16)

    kn = kn_ref[...]
    for kh in range(A_KV_HEADS):
        lo = q_end + kh * A_HEAD_DIM
        k = rope_a(_rms(proj[:, lo:lo + A_HEAD_DIM]) * kn)
        ka_ref[:, kh * A_HEAD_DIM:(kh + 1) * A_HEAD_DIM] = k.astype(BF16)
        lo = k_end + kh * A_HEAD_DIM
        put_vt(va_ref, kh, proj[:, lo:lo + A_HEAD_DIM])
    if with_q:
        qn = qn_ref[...]
        for hh in range(A_HEADS):
            lo = hh * A_HEAD_DIM
            q = rope_a(_rms(proj[:, lo:lo + A_HEAD_DIM]) * qn) * A_SCALE
            qa_ref[:, lo:lo + A_HEAD_DIM] = q.astype(BF16)

    kr = rope_b(proj[:, ckv_end:W_IN_PAD]).astype(BF16)
    for hh in range(B_HEADS):
        put_vt(vb_ref, hh, kv[:, nope_w + hh * B_V_DIM:nope_w + (hh + 1) * B_V_DIM])
        lo = hh * B_QK_PAD
        kb_ref[:, lo:lo + B_NOPE_DIM] = kv[:, hh * B_NOPE_DIM:(hh + 1) * B_NOPE_DIM].astype(BF16)
        kb_ref[:, lo + B_NOPE_DIM:lo + B_QK_PAD] = kr
    if with_q:
        for hh in range(B_HEADS):
            lo = hh * B_QK_PAD
            qb_ref[:, lo:lo + B_NOPE_DIM] = (qb[:, hh * B_NOPE_DIM:(hh + 1) * B_NOPE_DIM] * B_SCALE).astype(BF16)
            r = rope_b(qb[:, nope_w + hh * 128:nope_w + (hh + 1) * 128]) * B_SCALE
            qb_ref[:, lo + B_NOPE_DIM:lo + B_QK_PAD] = r.astype(BF16)


def _attn_proj(x, mod, layer, ctx, nw, wts, rope, with_q, tm):
    bn, s, _ = x.shape
    use_rope = rope is not None
    w_in, qn, kn, mqn, mkvn, w_uq, w_ukv = wts
    row = lambda w: pl.BlockSpec((None, tm, w), lambda b, i: (b, i, 0))
    in_specs = [row(D_MODEL), _mod_spec(layer, 0, ctx), _mod_spec(layer, 1, ctx),
                _const_spec((1, D_MODEL)), _const_spec(w_in.shape),
                _const_spec(qn.shape), _const_spec(kn.shape), _const_spec(mqn.shape),
                _const_spec(mkvn.shape), _const_spec(w_uq.shape), _const_spec(w_ukv.shape)]
    args = [x, mod, mod, nw, w_in, qn, kn, mqn, mkvn, w_uq, w_ukv]
    if use_rope:
        in_specs += [pl.BlockSpec((tm, 128), lambda b, i: (i, 0))] * 6
        args += list(rope)
    col = lambda w: pl.BlockSpec((None, w, tm), lambda b, i: (b, 0, i))
    outs_desc = ([(A_HEADS * A_HEAD_DIM, False), (B_HEADS * B_QK_PAD, False)] if with_q else []) + [
        (A_KV_HEADS * A_HEAD_DIM, False), (A_KV_HEADS * V_ROWS, True),
        (B_HEADS * B_QK_PAD, False), (B_HEADS * V_ROWS, True)]
    outs = pl.pallas_call(
        functools.partial(_attn_proj_body, use_rope=use_rope, with_q=with_q),
        grid=(bn, s // tm),
        in_specs=in_specs,
        out_specs=[col(w) if t else row(w) for w, t in outs_desc],
        out_shape=[jax.ShapeDtypeStruct((bn, w, s) if t else (bn, s, w), BF16) for w, t in outs_desc],
        scratch_shapes=[pltpu.VMEM((tm, D_MODEL), BF16)],
        compiler_params=_cparams(("parallel", "parallel")),
        name="attn_proj_ctx" if ctx else "attn_proj",
    )(*args)
    return outs if with_q else [None, None] + list(outs)


def _col_reduce(x, op, final):
    while x.shape[0] > 8 and x.shape[0] % 16 == 0:
        h = x.shape[0] // 2
        x = op(x[:h], x[h:])
    return final(x, axis=0, keepdims=True)


def _flash_body(*refs, group, dq, dv, tk, n_lat, depth):
    if n_lat:
        q_ref, kc_ref, vct_ref, k_ref, vt_ref, o_ref = refs
    else:
        q_ref, kc_ref, vct_ref, o_ref = refs
    tq = q_ref.shape[0]
    rows = [slice(r, r + QGROUP) for r in range(0, tq, QGROUP)]
    q_parts = [(r, slice(g * dq, (g + 1) * dq)) for r in rows for g in range(group)]
    o_parts = [(r, slice(g * dv, (g + 1) * dv)) for r in rows for g in range(group)]
    n = len(q_parts)
    chunks = [(kc_ref, vct_ref, 0, kc_ref.shape[0])] + [(k_ref, vt_ref, c * tk, tk) for c in range(n_lat)]
    m, acc = [None] * n, [None] * n
    blocks = [(ci, g) for ci in range(len(chunks)) for g in range(n)]

    def scores(ci, g):
        kr, _, off, size = chunks[ci]
        return lax.dot_general(kr[off:off + size, :], q_ref[q_parts[g]], (((1,), (1,)), ((), ())),
                               preferred_element_type=F32)

    s_next = [scores(*blocks[d]) for d in range(min(depth, len(blocks)))]
    for idx, (ci, g) in enumerate(blocks):
        s = s_next.pop(0)
        if idx + depth < len(blocks):
            s_next.append(scores(*blocks[idx + depth]))
        _, vr, off, size = chunks[ci]
        vt = vr[:, off:off + size]
        smax = _col_reduce(s, jnp.maximum, jnp.max)
        if ci == 0:
            m_new = smax
            p = jnp.exp2(s - m_new)
            acc[g] = jnp.dot(vt, p.astype(BF16), preferred_element_type=F32)
        else:
            m_new = jnp.maximum(m[g], smax)
            alpha = jnp.exp2(m[g] - m_new)
            p = jnp.exp2(s - m_new)
            acc[g] = alpha * acc[g] + jnp.dot(vt, p.astype(BF16), preferred_element_type=F32)
        m[g] = m_new
    for g in range(n):
        o_ref[o_parts[g]] = (acc[g][:dv] / acc[g][dv:dv + 1]).T.astype(o_ref.dtype)


def _flash(q, kc, vct, k, vt, *, group, dq, dv, tq, tk, name, depth=4):
    bn, s, qw = q.shape
    kvh = qw // (group * dq)
    c = kc.shape[1]
    n_lat = 0 if k is None else k.shape[1] // tk
    in_specs = [pl.BlockSpec((None, tq, group * dq), lambda b, h, i: (b, i, h)),
                pl.BlockSpec((None, c, dq), lambda b, h, i: (b, 0, h)),
                pl.BlockSpec((None, V_ROWS, c), lambda b, h, i: (b, h, 0))]
    args = [q, kc, vct]
    if n_lat:
        t = k.shape[1]
        in_specs += [pl.BlockSpec((None, t, dq), lambda b, h, i: (b, 0, h)),
                     pl.BlockSpec((None, V_ROWS, t), lambda b, h, i: (b, h, 0))]
        args += [k, vt]
    return pl.pallas_call(
        functools.partial(_flash_body, group=group, dq=dq, dv=dv, tk=tk, n_lat=n_lat, depth=depth),
        grid=(bn, kvh, s // tq),
        in_specs=in_specs,
        out_specs=pl.BlockSpec((None, tq, group * dv), lambda b, h, i: (b, i, h)),
        out_shape=jax.ShapeDtypeStruct((bn, s, kvh * group * dv), BF16),
        compiler_params=_cparams(("parallel", "parallel", "parallel")),
        name=name,
    )(*args)


def _proj_res_body(*refs, n_lhs, final):
    x_ref, g_ref = refs[0], refs[1]
    lhs = refs[2:2 + n_lhs]
    ws = refs[2 + n_lhs:2 + 2 * n_lhs]
    rest = refs[2 + 2 * n_lhs:]
    y = jnp.dot(lhs[0][...], ws[0][...], preferred_element_type=F32)
    for a, w in zip(lhs[1:], ws[1:]):
        y = y + jnp.dot(a[...], w[...], preferred_element_type=F32)
    out = x_ref[...] + g_ref[...] * y
    if final:
        fw_ref, o_ref = rest
        out = _rms(out) * fw_ref[...]
    else:
        (o_ref,) = rest
    o_ref[...] = out


def _proj_res(x, mod, layer, part, ctx, lhs, ws, tm, final_w=None, name="proj_res"):
    bn, s, _ = x.shape
    row = lambda w: pl.BlockSpec((None, tm, w), lambda b, i: (b, i, 0))
    in_specs = [row(D_MODEL), _mod_spec(layer, part, ctx)]
    in_specs += [row(a.shape[-1]) for a in lhs] + [_layer_spec(shape, idx) for _, shape, idx in ws]
    args = [x, mod] + list(lhs) + [w for w, _, _ in ws]
    if final_w is not None:
        in_specs.append(_const_spec((1, D_MODEL)))
        args.append(final_w)
    return pl.pallas_call(
        functools.partial(_proj_res_body, n_lhs=len(lhs), final=final_w is not None),
        grid=(bn, s // tm),
        in_specs=in_specs,
        out_specs=row(D_MODEL),
        out_shape=jax.ShapeDtypeStruct(x.shape, F32),
        compiler_params=_cparams(("parallel", "parallel")),
        name=name + ("_ctx" if ctx else ""),
    )(*args)


def _up_body(*refs, mode, tm, seg):
    x_ref, xp_ref, xn_ref, sh_ref, sc_ref, nw_ref = refs[:6]
    i = pl.program_id(1)
    j = pl.program_id(2)
    if mode == "ffn":
        wg_ref, wu_ref, cg_ref, cu_ref, o_ref, h_scr, g_scr, u_scr = refs[6:]
    else:
        wb_ref, wc_ref, wv_ref, cw_ref, o_ref, h_scr, g_scr = refs[6:]
    whole = seg >= tm

    @pl.when(j == 0)
    def _():
        gain = nw_ref[...] * (1.0 + sc_ref[...])
        sh = sh_ref[...]
        for r in range(0, tm, NORM_ROWS):
            h_scr[HALO + r:HALO + r + NORM_ROWS, :] = (
                _rms(x_ref[r:r + NORM_ROWS, :]) * gain + sh).astype(BF16)
        first = (i * tm) % seg == 0 if whole else True
        last = ((i + 1) * tm) % seg == 0 if whole else True
        hp = _rms(xp_ref[...]) * gain + sh
        h_scr[0:HALO, :] = jnp.where(first, 0.0, hp).astype(BF16)
        hn = _rms(xn_ref[...]) * gain + sh
        h_scr[HALO + tm:2 * HALO + tm, :] = jnp.where(last, 0.0, hn).astype(BF16)

    def conv(scr, cw_ref):
        prev = scr[HALO - 1:HALO - 1 + tm, :]
        nxt = scr[HALO + 1:HALO + 1 + tm, :]
        if not whole:
            pos = lax.broadcasted_iota(jnp.int32, (tm, 1), 0) % seg
            prev = jnp.where(pos == 0, 0.0, prev)
            nxt = jnp.where(pos == seg - 1, 0.0, nxt)
        return cw_ref[0:1, :] * prev + cw_ref[1:2, :] * scr[HALO:HALO + tm, :] + cw_ref[2:3, :] * nxt

    h = h_scr[...]
    if mode == "ffn":
        g_scr[...] = jnp.dot(h, wg_ref[...], preferred_element_type=F32)
        u_scr[...] = jnp.dot(h, wu_ref[...], preferred_element_type=F32)
        g = conv(g_scr, cg_ref)
        u = conv(u_scr, cu_ref)
        o_ref[...] = ((g / (1.0 + jnp.exp(-g))) * u).astype(o_ref.dtype)
    else:
        g_scr[...] = (jnp.dot(h, wc_ref[...], preferred_element_type=F32)
                      * jnp.dot(h, wv_ref[...], preferred_element_type=F32))
        b = jnp.dot(h_scr[HALO:HALO + tm, :], wb_ref[...], preferred_element_type=F32)
        o_ref[...] = (b * conv(g_scr, cw_ref)).astype(o_ref.dtype)


def _up(x, mod, layer, ctx, nw, w, cw, wl, mode, tm, tn):
    seg = x.shape[1]
    if ctx:
        x = x.reshape(1, -1, D_MODEL)
    bn, s, _ = x.shape
    parts = 2 if mode == "ffn" else 3
    width = w.shape[2] // parts
    nj = width // tn
    hb = tm // HALO
    last_hb = s // HALO - 1
    in_specs = [
        pl.BlockSpec((None, tm, D_MODEL), lambda b, i, j: (b, i, 0)),
        pl.BlockSpec((None, HALO, D_MODEL), lambda b, i, j: (b, jnp.maximum(i * hb - 1, 0), 0)),
        pl.BlockSpec((None, HALO, D_MODEL), lambda b, i, j: (b, jnp.minimum((i + 1) * hb, last_hb), 0)),
        _mod_spec(layer, 3 if mode == "ffn" else 0, ctx),
        _mod_spec(layer, 4 if mode == "ffn" else 1, ctx),
        _const_spec((1, D_MODEL)),
    ]
    args = [x, x, x, mod, mod, nw]
    wspec = lambda p: pl.BlockSpec((None, D_MODEL, tn), lambda b, i, j: (wl, 0, p * nj + j))
    cspec = lambda p: pl.BlockSpec((None, 3, tn), lambda b, i, j: (wl, 0, p * nj + j))
    rows = tm + 2 * HALO
    scratch = [pltpu.VMEM((rows, D_MODEL), BF16), pltpu.VMEM((rows, tn), F32)]
    if mode == "ffn":
        in_specs += [wspec(0), wspec(1), cspec(0), cspec(1)]
        args += [w, w, cw, cw]
        scratch.append(pltpu.VMEM((rows, tn), F32))
    else:
        in_specs += [wspec(0), wspec(1), wspec(2), cspec(0)]
        args += [w, w, w, cw]
    out = pl.pallas_call(
        functools.partial(_up_body, mode=mode, tm=tm, seg=seg),
        grid=(bn, s // tm, nj),
        in_specs=in_specs,
        out_specs=pl.BlockSpec((None, tm, tn), lambda b, i, j: (b, i, j)),
        out_shape=jax.ShapeDtypeStruct((bn, s, width), BF16),
        scratch_shapes=scratch,
        compiler_params=_cparams(("parallel", "parallel", "arbitrary")),
        name=mode + "_up" + ("_ctx" if ctx else ""),
    )(*args)
    return out.reshape(-1, seg, width)


def _rope_tables(seq, rot_dim):
    rows = seq // GRID_W
    r = jnp.repeat(jnp.arange(rows, dtype=F32), GRID_W)
    col = jnp.tile(jnp.arange(GRID_W, dtype=F32), rows)
    quarter = rot_dim // 4
    inv = ROPE_THETA ** (-jnp.arange(quarter, dtype=F32) / quarter)
    ar = r[:, None] * inv
    ac = col[:, None] * inv
    ang = jnp.concatenate([ar, ar, ac, ac], axis=-1)
    reps = 128 // rot_dim
    cos = jnp.tile(jnp.cos(ang), (1, reps))
    sin = jnp.tile(jnp.sin(ang), (1, reps))
    first = ((jnp.arange(128) // quarter) % 2 == 0)[None, :]
    return cos, jnp.where(first, -sin, 0.0), jnp.where(first, 0.0, sin)


def _attn_weights(w_in, qn, kn, mqn, mkvn, w_uq, w_ukv):
    w_in = jnp.pad(w_in, ((0, 0), (0, W_IN_PAD - w_in.shape[1]))).astype(BF16)
    uq = w_uq.reshape(B_Q_RANK, B_HEADS, B_NOPE_DIM + B_ROPE_DIM)
    uq_rope = jnp.pad(uq[:, :, B_NOPE_DIM:], ((0, 0), (0, 0), (0, 128 - B_ROPE_DIM)))
    w_uq = jnp.concatenate([uq[:, :, :B_NOPE_DIM].reshape(B_Q_RANK, -1),
                            uq_rope.reshape(B_Q_RANK, -1)], axis=1).astype(BF16)
    ukv = w_ukv.reshape(B_KV_RANK, B_HEADS, B_NOPE_DIM + B_V_DIM)
    w_ukv = jnp.concatenate([ukv[:, :, :B_NOPE_DIM].reshape(B_KV_RANK, -1),
                             ukv[:, :, B_NOPE_DIM:].reshape(B_KV_RANK, -1)], axis=1).astype(BF16)
    return (w_in, qn.reshape(1, -1), kn.reshape(1, -1), mqn.reshape(1, -1), mkvn.reshape(1, -1),
            w_uq, w_ukv)


def kernel(x, c, ctx, c_ctx, w_ada, b_ada, norm_mix, norm_ffn, attn_w_in, attn_q_norm, attn_k_norm,
           mla_q_norm, mla_kv_norm, mla_w_uq, mla_w_ukv, attn_w_o, sc_w_in, sc_conv, sc_w_out,
           ffn_w_up, ffn_conv, ffn_w_down, final_norm):
    bn, s, _ = x.shape
    cl = ctx.shape[1]
    assert bn <= MOD_ROWS // 2
    cvec = jnp.zeros((MOD_ROWS, D_MODEL), F32).at[:bn].set(c).at[MOD_ROWS // 2].set(c_ctx)
    mod = _adaln(cvec, w_ada, b_ada)
    rope = _rope_tables(s, A_HEAD_DIM) + _rope_tables(s, B_ROPE_DIM)

    w_o_all = attn_w_o.astype(BF16)
    sc_w_in_b, sc_w_out_b = sc_w_in.astype(BF16), sc_w_out.astype(BF16)
    ffn_w_up_b, ffn_w_down_b = ffn_w_up.astype(BF16), ffn_w_down.astype(BF16)
    half = A_HEADS * A_HEAD_DIM
    xc = ctx
    for l in range(DEPTH):
        later_attn = any(j % 2 == 0 for j in range(l + 1, DEPTH))
        i = l // 2
        nw = norm_mix[l].reshape(1, -1)
        if l % 2 == 0:
            wts = _attn_weights(attn_w_in[i], attn_q_norm[i], attn_k_norm[i], mla_q_norm[i],
                                mla_kv_norm[i], mla_w_uq[i], mla_w_ukv[i])
            w_o = [(w_o_all, (half, D_MODEL), (i, 0, 0)), (w_o_all, (half, D_MODEL), (i, 1, 0))]
            qa_c, qb_c, ka_c, va_c, kb_c, vb_c = _attn_proj(xc, mod, l, True, nw, wts, None, later_attn, cl)
            qa, qb, ka, va, kb, vb = _attn_proj(x, mod, l, False, nw, wts, rope, True, 512)
            oa = _flash(qa, ka_c, va_c, ka, va, group=A_GROUP, dq=A_HEAD_DIM, dv=A_HEAD_DIM,
                        tq=512, tk=512, name="gqa", depth=6)
            ob = _flash(qb, kb_c, vb_c, kb, vb, group=1, dq=B_QK_PAD, dv=B_V_DIM,
                        tq=2048, tk=512, name="mla", depth=6)
            x = _proj_res(x, mod, l, 2, False, [oa, ob], w_o, 512, name="attn_out")
            if later_attn:
                oa_c = _flash(qa_c, ka_c, va_c, None, None, group=A_GROUP, dq=A_HEAD_DIM, dv=A_HEAD_DIM,
                              tq=cl, tk=0, name="gqa_ctx")
                ob_c = _flash(qb_c, kb_c, vb_c, None, None, group=1, dq=B_QK_PAD, dv=B_V_DIM,
                              tq=cl, tk=0, name="mla_ctx")
                xc = _proj_res(xc, mod, l, 2, True, [oa_c, ob_c], w_o, cl, name="attn_out")
        else:
            w_out = [(sc_w_out_b, (SC_WIDTH, D_MODEL), (i, 0, 0))]
            z = _up(x, mod, l, False, nw, sc_w_in_b, sc_conv, i, "sc", 1024, 512)
            x = _proj_res(x, mod, l, 2, False, [z], w_out, 512, name="sc_out")
            if later_attn:
                zc = _up(xc, mod, l, True, nw, sc_w_in_b, sc_conv, i, "sc", 1024, 512)
                xc = _proj_res(xc, mod, l, 2, True, [zc], w_out, cl, name="sc_out")
        nwf = norm_ffn[l].reshape(1, -1)
        w_down = [(ffn_w_down_b, (D_FF, D_MODEL), (l, 0, 0))]
        hid = _up(x, mod, l, False, nwf, ffn_w_up_b, ffn_conv, l, "ffn", 1024, 512)
        fw = final_norm.reshape(1, -1) if l == DEPTH - 1 else None
        x = _proj_res(x, mod, l, 5, False, [hid], w_down, 512, final_w=fw, name="ffn_down")
        if later_attn:
            hid_c = _up(xc, mod, l, True, nwf, ffn_w_up_b, ffn_conv, l, "ffn", 1024, 512)
            xc = _proj_res(xc, mod, l, 5, True, [hid_c], w_down, cl, name="ffn_down")
    return x
```

```python
import functools

import jax
import jax.numpy as jnp
from jax import lax
from jax.experimental import pallas as pl
from jax.experimental.pallas import tpu as pltpu

F32 = jnp.float32
BF16 = jnp.bfloat16

D_MODEL = 2048
DEPTH = 4
GRID_W = 64
ROPE_THETA = 10000.0
EPS = 1e-6
A_HEADS = 8
A_KV_HEADS = 2
A_GROUP = A_HEADS // A_KV_HEADS
A_HEAD_DIM = 128
B_HEADS = 8
B_Q_RANK = 512
B_KV_RANK = 256
B_NOPE_DIM = 128
B_ROPE_DIM = 64
B_V_DIM = 128
B_QK_PAD = 256
LOG2E = 1.4426950408889634
A_SCALE = A_HEAD_DIM ** -0.5 * LOG2E
B_SCALE = (B_NOPE_DIM + B_ROPE_DIM) ** -0.5 * LOG2E
QGROUP = 256
V_ROWS = 128 + 16
SC_WIDTH = D_MODEL
D_FF = 256 * ((8 * D_MODEL // 3 + 255) // 256)
W_IN_PAD = 2432
MOD_ROWS = 16
HALO = 16
NORM_ROWS = 32
VMEM_LIMIT = 56 * 1024 * 1024


def _cparams(sem):
    return pltpu.CompilerParams(dimension_semantics=sem, vmem_limit_bytes=VMEM_LIMIT)


def _const_spec(shape):
    nd = len(shape)
    return pl.BlockSpec(shape, lambda *g: (0,) * nd, pipeline_mode=pl.Buffered(1))


def _layer_spec(shape, idx):
    return pl.BlockSpec((None,) + tuple(shape), lambda *g: idx, pipeline_mode=pl.Buffered(1))


def _mod_spec(layer, part, ctx):
    if ctx:
        idx = lambda b, *g: (layer, MOD_ROWS // 2, part, 0, 0)
    else:
        idx = lambda b, *g: (layer, b, part, 0, 0)
    return pl.BlockSpec((None, None, None, 1, D_MODEL), idx)


def _rms(x):
    return x * lax.rsqrt(jnp.mean(x * x, axis=-1, keepdims=True) + EPS)


def _rope(x, cos, sin_up, sin_dn, quarter):
    n = x.shape[-1]
    return x * cos + pltpu.roll(x, n - quarter, 1) * sin_up + pltpu.roll(x, quarter, 1) * sin_dn


def _adaln_body(c_ref, w_ref, b_ref, o_ref):
    c = c_ref[...]
    s = (c / (1.0 + jnp.exp(-c))).astype(BF16)
    o_ref[...] = jnp.dot(s, w_ref[...].astype(BF16), preferred_element_type=F32) + b_ref[...]


def _adaln(cvec, w_ada, b_ada):
    tn = 1024
    n = 6 * D_MODEL
    out = pl.pallas_call(
        _adaln_body,
        grid=(DEPTH, n // tn),
        in_specs=[
            pl.BlockSpec((MOD_ROWS, D_MODEL), lambda l, j: (0, 0)),
            pl.BlockSpec((None, D_MODEL, tn), lambda l, j: (l, 0, j)),
            pl.BlockSpec((None, 1, tn), lambda l, j: (l, 0, j)),
        ],
        out_specs=pl.BlockSpec((None, MOD_ROWS, tn), lambda l, j: (l, 0, j)),
        out_shape=jax.ShapeDtypeStruct((DEPTH, MOD_ROWS, n), F32),
        compiler_params=_cparams(("parallel", "parallel")),
        name="adaln",
    )(cvec, w_ada, b_ada.reshape(DEPTH, 1, n))
    return out.reshape(DEPTH, MOD_ROWS, 6, 1, D_MODEL)


def _attn_proj_body(*refs, use_rope, with_q):
    it = iter(refs)
    x_ref, sh_ref, sc_ref, nw_ref, win_ref = (next(it) for _ in range(5))
    qn_ref, kn_ref, mqn_ref, mkvn_ref, wuq_ref, wukv_ref = (next(it) for _ in range(6))
    if use_rope:
        ca, sau, sad, cb, sbu, sbd = (next(it)[...] for _ in range(6))
    if with_q:
        qa_ref, qb_ref = next(it), next(it)
    ka_ref, va_ref, kb_ref, vb_ref, h_scr = (next(it) for _ in range(5))

    def rope_a(v):
        return _rope(v, ca, sau, sad, A_HEAD_DIM // 4) if use_rope else v

    def rope_b(v):
        return _rope(v, cb, sbu, sbd, B_ROPE_DIM // 4) if use_rope else v

    gain = nw_ref[...] * (1.0 + sc_ref[...])
    sh = sh_ref[...]
    tm = x_ref.shape[0]
    for r in range(0, tm, NORM_ROWS):
        h_scr[r:r + NORM_ROWS, :] = (_rms(x_ref[r:r + NORM_ROWS, :]) * gain + sh).astype(BF16)
    proj = jnp.dot(h_scr[...], win_ref[...], preferred_element_type=F32)
    q_end = A_HEADS * A_HEAD_DIM
    k_end = q_end + A_KV_HEADS * A_HEAD_DIM
    v_end = k_end + A_KV_HEADS * A_HEAD_DIM
    cq_end = v_end + B_Q_RANK
    ckv_end = cq_end + B_KV_RANK
    nope_w = B_HEADS * B_NOPE_DIM

    ckv = (_rms(proj[:, cq_end:ckv_end]) * mkvn_ref[...]).astype(BF16)
    kv = jnp.dot(ckv, wukv_ref[...], preferred_element_type=F32)
    if with_q:
        cq = (_rms(proj[:, v_end:cq_end]) * mqn_ref[...]).astype(BF16)
        qb = jnp.dot(cq, wuq_ref[...], preferred_element_type=F32)

    def put_vt(ref, hh, v):
        lo = hh * V_ROWS
        ref[lo:lo + v.shape[1], :] = v.T.astype(BF16)
        ref[lo + v.shape[1]:lo + V_ROWS, :] = jnp.ones((V_ROWS - v.shape[1], v.shape[0]), BF16)

    kn = kn_ref[...]
    for kh in range(A_KV_HEADS):
        lo = q_end + kh * A_HEAD_DIM
        k = rope_a(_rms(proj[:, lo:lo + A_HEAD_DIM]) * kn)
        ka_ref[:, kh * A_HEAD_DIM:(kh + 1) * A_HEAD_DIM] = k.astype(BF16)
        lo = k_end + kh * A_HEAD_DIM
        put_vt(va_ref, kh, proj[:, lo:lo + A_HEAD_DIM])
    if with_q:
        qn = qn_ref[...]
        for hh in range(A_HEADS):
            lo = hh * A_HEAD_DIM
            q = rope_a(_rms(proj[:, lo:lo + A_HEAD_DIM]) * qn) * A_SCALE
            qa_ref[:, lo:lo + A_HEAD_DIM] = q.astype(BF16)

    kr = rope_b(proj[:, ckv_end:W_IN_PAD]).astype(BF16)
    for hh in range(B_HEADS):
        put_vt(vb_ref, hh, kv[:, nope_w + hh * B_V_DIM:nope_w + (hh + 1) * B_V_DIM])
        lo = hh * B_QK_PAD
        kb_ref[:, lo:lo + B_NOPE_DIM] = kv[:, hh * B_NOPE_DIM:(hh + 1) * B_NOPE_DIM].astype(BF16)
        kb_ref[:, lo + B_NOPE_DIM:lo + B_QK_PAD] = kr
    if with_q:
        for hh in range(B_HEADS):
            lo = hh * B_QK_PAD
            qb_ref[:, lo:lo + B_NOPE_DIM] = (qb[:, hh * B_NOPE_DIM:(hh + 1) * B_NOPE_DIM] * B_SCALE).astype(BF16)
            r = rope_b(qb[:, nope_w + hh * 128:nope_w + (hh + 1) * 128]) * B_SCALE
            qb_ref[:, lo + B_NOPE_DIM:lo + B_QK_PAD] = r.astype(BF16)


def _attn_proj(x, mod, layer, ctx, nw, wts, rope, with_q, tm):
    bn, s, _ = x.shape
    use_rope = rope is not None
    w_in, qn, kn, mqn, mkvn, w_uq, w_ukv = wts
    row = lambda w: pl.BlockSpec((None, tm, w), lambda b, i: (b, i, 0))
    in_specs = [row(D_MODEL), _mod_spec(layer, 0, ctx), _mod_spec(layer, 1, ctx),
                _const_spec((1, D_MODEL)), _const_spec(w_in.shape),
                _const_spec(qn.shape), _const_spec(kn.shape), _const_spec(mqn.shape),
                _const_spec(mkvn.shape), _const_spec(w_uq.shape), _const_spec(w_ukv.shape)]
    args = [x, mod, mod, nw, w_in, qn, kn, mqn, mkvn, w_uq, w_ukv]
    if use_rope:
        in_specs += [pl.BlockSpec((tm, 128), lambda b, i: (i, 0))] * 6
        args += list(rope)
    col = lambda w: pl.BlockSpec((None, w, tm), lambda b, i: (b, 0, i))
    outs_desc = ([(A_HEADS * A_HEAD_DIM, False), (B_HEADS * B_QK_PAD, False)] if with_q else []) + [
        (A_KV_HEADS * A_HEAD_DIM, False), (A_KV_HEADS * V_ROWS, True),
        (B_HEADS * B_QK_PAD, False), (B_HEADS * V_ROWS, True)]
    outs = pl.pallas_call(
        functools.partial(_attn_proj_body, use_rope=use_rope, with_q=with_q),
        grid=(bn, s // tm),
        in_specs=in_specs,
        out_specs=[col(w) if t else row(w) for w, t in outs_desc],
        out_shape=[jax.ShapeDtypeStruct((bn, w, s) if t else (bn, s, w), BF16) for w, t in outs_desc],
        scratch_shapes=[pltpu.VMEM((tm, D_MODEL), BF16)],
        compiler_params=_cparams(("parallel", "parallel")),
        name="attn_proj_ctx" if ctx else "attn_proj",
    )(*args)
    return outs if with_q else [None, None] + list(outs)


def _col_reduce(x, op, final):
    while x.shape[0] > 8 and x.shape[0] % 16 == 0:
        h = x.shape[0] // 2
        x = op(x[:h], x[h:])
    return final(x, axis=0, keepdims=True)


def _flash_body(*refs, group, dq, dv, tk, n_lat, depth):
    if n_lat:
        q_ref, kc_ref, vct_ref, k_ref, vt_ref, o_ref = refs
    else:
        q_ref, kc_ref, vct_ref, o_ref = refs
    tq = q_ref.shape[0]
    rows = [slice(r, r + QGROUP) for r in range(0, tq, QGROUP)]
    q_parts = [(r, slice(g * dq, (g + 1) * dq)) for r in rows for g in range(group)]
    o_parts = [(r, slice(g * dv, (g + 1) * dv)) for r in rows for g in range(group)]
    n = len(q_parts)
    chunks = [(kc_ref, vct_ref, 0, kc_ref.shape[0])] + [(k_ref, vt_ref, c * tk, tk) for c in range(n_lat)]
    m, acc = [None] * n, [None] * n
    blocks = [(ci, g) for ci in range(len(chunks)) for g in range(n)]

    def scores(ci, g):
        kr, _, off, size = chunks[ci]
        return lax.dot_general(kr[off:off + size, :], q_ref[q_parts[g]], (((1,), (1,)), ((), ())),
                               preferred_element_type=F32)

    s_next = [scores(*blocks[d]) for d in range(min(depth, len(blocks)))]
    for idx, (ci, g) in enumerate(blocks):
        s = s_next.pop(0)
        if idx + depth < len(blocks):
            s_next.append(scores(*blocks[idx + depth]))
        _, vr, off, size = chunks[ci]
        vt = vr[:, off:off + size]
        smax = _col_reduce(s, jnp.maximum, jnp.max)
        if ci == 0:
            m_new = smax
            p = jnp.exp2(s - m_new)
            acc[g] = jnp.dot(vt, p.astype(BF16), preferred_element_type=F32)
        else:
            m_new = jnp.maximum(m[g], smax)
            alpha = jnp.exp2(m[g] - m_new)
            p = jnp.exp2(s - m_new)
            acc[g] = alpha * acc[g] + jnp.dot(vt, p.astype(BF16), preferred_element_type=F32)
        m[g] = m_new
    for g in range(n):
        o_ref[o_parts[g]] = (acc[g][:dv] / acc[g][dv:dv + 1]).T.astype(o_ref.dtype)


def _flash(q, kc, vct, k, vt, *, group, dq, dv, tq, tk, name, depth=4):
    bn, s, qw = q.shape
    kvh = qw // (group * dq)
    c = kc.shape[1]
    n_lat = 0 if k is None else k.shape[1] // tk
    in_specs = [pl.BlockSpec((None, tq, group * dq), lambda b, h, i: (b, i, h)),
                pl.BlockSpec((None, c, dq), lambda b, h, i: (b, 0, h)),
                pl.BlockSpec((None, V_ROWS, c), lambda b, h, i: (b, h, 0))]
    args = [q, kc, vct]
    if n_lat:
        t = k.shape[1]
        in_specs += [pl.BlockSpec((None, t, dq), lambda b, h, i: (b, 0, h)),
                     pl.BlockSpec((None, V_ROWS, t), lambda b, h, i: (b, h, 0))]
        args += [k, vt]
    return pl.pallas_call(
        functools.partial(_flash_body, group=group, dq=dq, dv=dv, tk=tk, n_lat=n_lat, depth=depth),
        grid=(bn, kvh, s // tq),
        in_specs=in_specs,
        out_specs=pl.BlockSpec((None, tq, group * dv), lambda b, h, i: (b, i, h)),
        out_shape=jax.ShapeDtypeStruct((bn, s, kvh * group * dv), BF16),
        compiler_params=_cparams(("parallel", "parallel", "parallel")),
        name=name,
    )(*args)


def _proj_res_body(*refs, n_lhs, final):
    x_ref, g_ref = refs[0], refs[1]
    lhs = refs[2:2 + n_lhs]
    ws = refs[2 + n_lhs:2 + 2 * n_lhs]
    rest = refs[2 + 2 * n_lhs:]
    y = jnp.dot(lhs[0][...], ws[0][...], preferred_element_type=F32)
    for a, w in zip(lhs[1:], ws[1:]):
        y = y + jnp.dot(a[...], w[...], preferred_element_type=F32)
    out = x_ref[...] + g_ref[...] * y
    if final:
        fw_ref, o_ref = rest
        out = _rms(out) * fw_ref[...]
    else:
        (o_ref,) = rest
    o_ref[...] = out


def _proj_res(x, mod, layer, part, ctx, lhs, ws, tm, final_w=None, name="proj_res"):
    bn, s, _ = x.shape
    row = lambda w: pl.BlockSpec((None, tm, w), lambda b, i: (b, i, 0))
    in_specs = [row(D_MODEL), _mod_spec(layer, part, ctx)]
    in_specs += [row(a.shape[-1]) for a in lhs] + [_layer_spec(shape, idx) for _, shape, idx in ws]
    args = [x, mod] + list(lhs) + [w for w, _, _ in ws]
    if final_w is not None:
        in_specs.append(_const_spec((1, D_MODEL)))
        args.append(final_w)
    return pl.pallas_call(
        functools.partial(_proj_res_body, n_lhs=len(lhs), final=final_w is not None),
        grid=(bn, s // tm),
        in_specs=in_specs,
        out_specs=row(D_MODEL),
        out_shape=jax.ShapeDtypeStruct(x.shape, F32),
        compiler_params=_cparams(("parallel", "parallel")),
        name=name + ("_ctx" if ctx else ""),
    )(*args)


def _up_body(*refs, mode, tm, seg):
    x_ref, xp_ref, xn_ref, sh_ref, sc_ref, nw_ref = refs[:6]
    i = pl.program_id(1)
    j = pl.program_id(2)
    if mode == "ffn":
        wg_ref, wu_ref, cg_ref, cu_ref, o_ref, h_scr, g_scr, u_scr = refs[6:]
    else:
        wb_ref, wc_ref, wv_ref, cw_ref, o_ref, h_scr, g_scr = refs[6:]
    whole = seg >= tm

    @pl.when(j == 0)
    def _():
        gain = nw_ref[...] * (1.0 + sc_ref[...])
        sh = sh_ref[...]
        for r in range(0, tm, NORM_ROWS):
            h_scr[HALO + r:HALO + r + NORM_ROWS, :] = (
                _rms(x_ref[r:r + NORM_ROWS, :]) * gain + sh).astype(BF16)
        first = (i * tm) % seg == 0 if whole else True
        last = ((i + 1) * tm) % seg == 0 if whole else True
        hp = _rms(xp_ref[...]) * gain + sh
        h_scr[0:HALO, :] = jnp.where(first, 0.0, hp).astype(BF16)
        hn = _rms(xn_ref[...]) * gain + sh
        h_scr[HALO + tm:2 * HALO + tm, :] = jnp.where(last, 0.0, hn).astype(BF16)

    def conv(scr, cw_ref):
        full = scr[...]
        rows = full.shape[0]
        prev = pltpu.roll(full, 1, 0)[HALO:HALO + tm, :]
        nxt = pltpu.roll(full, rows - 1, 0)[HALO:HALO + tm, :]
        if not whole:
            pos = lax.broadcasted_iota(jnp.int32, (tm, 1), 0) % seg
            prev = jnp.where(pos == 0, 0.0, prev)
            nxt = jnp.where(pos == seg - 1, 0.0, nxt)
        return cw_ref[0:1, :] * prev + cw_ref[1:2, :] * scr[HALO:HALO + tm, :] + cw_ref[2:3, :] * nxt

    h = h_scr[...]
    if mode == "ffn":
        g_scr[...] = jnp.dot(h, wg_ref[...], preferred_element_type=F32)
        u_scr[...] = jnp.dot(h, wu_ref[...], preferred_element_type=F32)
        g = conv(g_scr, cg_ref)
        u = conv(u_scr, cu_ref)
        o_ref[...] = ((g / (1.0 + jnp.exp(-g))) * u).astype(o_ref.dtype)
    else:
        g_scr[...] = (jnp.dot(h, wc_ref[...], preferred_element_type=F32)
                      * jnp.dot(h, wv_ref[...], preferred_element_type=F32))
        b = jnp.dot(h_scr[HALO:HALO + tm, :], wb_ref[...], preferred_element_type=F32)
        o_ref[...] = (b * conv(g_scr, cw_ref)).astype(o_ref.dtype)


def _up(x, mod, layer, ctx, nw, w, cw, wl, mode, tm, tn):
    seg = x.shape[1]
    if ctx:
        x = x.reshape(1, -1, D_MODEL)
    bn, s, _ = x.shape
    parts = 2 if mode == "ffn" else 3
    width = w.shape[2] // parts
    nj = width // tn
    hb = tm // HALO
    last_hb = s // HALO - 1
    in_specs = [
        pl.BlockSpec((None, tm, D_MODEL), lambda b, i, j: (b, i, 0)),
        pl.BlockSpec((None, HALO, D_MODEL), lambda b, i, j: (b, jnp.maximum(i * hb - 1, 0), 0)),
        pl.BlockSpec((None, HALO, D_MODEL), lambda b, i, j: (b, jnp.minimum((i + 1) * hb, last_hb), 0)),
        _mod_spec(layer, 3 if mode == "ffn" else 0, ctx),
        _mod_spec(layer, 4 if mode == "ffn" else 1, ctx),
        _const_spec((1, D_MODEL)),
    ]
    args = [x, x, x, mod, mod, nw]
    wspec = lambda p: pl.BlockSpec((None, D_MODEL, tn), lambda b, i, j: (wl, 0, p * nj + j))
    cspec = lambda p: pl.BlockSpec((None, 3, tn), lambda b, i, j: (wl, 0, p * nj + j))
    rows = tm + 2 * HALO
    scratch = [pltpu.VMEM((rows, D_MODEL), BF16), pltpu.VMEM((rows, tn), F32)]
    if mode == "ffn":
        in_specs += [wspec(0), wspec(1), cspec(0), cspec(1)]
        args += [w, w, cw, cw]
        scratch.append(pltpu.VMEM((rows, tn), F32))
    else:
        in_specs += [wspec(0), wspec(1), wspec(2), cspec(0)]
        args += [w, w, w, cw]
    out = pl.pallas_call(
        functools.partial(_up_body, mode=mode, tm=tm, seg=seg),
        grid=(bn, s // tm, nj),
        in_specs=in_specs,
        out_specs=pl.BlockSpec((None, tm, tn), lambda b, i, j: (b, i, j)),
        out_shape=jax.ShapeDtypeStruct((bn, s, width), BF16),
        scratch_shapes=scratch,
        compiler_params=_cparams(("parallel", "parallel", "arbitrary")),
        name=mode + "_up" + ("_ctx" if ctx else ""),
    )(*args)
    return out.reshape(-1, seg, width)


def _rope_tables(seq, rot_dim):
    rows = seq // GRID_W
    r = jnp.repeat(jnp.arange(rows, dtype=F32), GRID_W)
    col = jnp.tile(jnp.arange(GRID_W, dtype=F32), rows)
    quarter = rot_dim // 4
    inv = ROPE_THETA ** (-jnp.arange(quarter, dtype=F32) / quarter)
    ar = r[:, None] * inv
    ac = col[:, None] * inv
    ang = jnp.concatenate([ar, ar, ac, ac], axis=-1)
    reps = 128 // rot_dim
    cos = jnp.tile(jnp.cos(ang), (1, reps))
    sin = jnp.tile(jnp.sin(ang), (1, reps))
    first = ((jnp.arange(128) // quarter) % 2 == 0)[None, :]
    return cos, jnp.where(first, -sin, 0.0), jnp.where(first, 0.0, sin)


def _attn_weights(w_in, qn, kn, mqn, mkvn, w_uq, w_ukv):
    w_in = jnp.pad(w_in, ((0, 0), (0, W_IN_PAD - w_in.shape[1]))).astype(BF16)
    uq = w_uq.reshape(B_Q_RANK, B_HEADS, B_NOPE_DIM + B_ROPE_DIM)
    uq_rope = jnp.pad(uq[:, :, B_NOPE_DIM:], ((0, 0), (0, 0), (0, 128 - B_ROPE_DIM)))
    w_uq = jnp.concatenate([uq[:, :, :B_NOPE_DIM].reshape(B_Q_RANK, -1),
                            uq_rope.reshape(B_Q_RANK, -1)], axis=1).astype(BF16)
    ukv = w_ukv.reshape(B_KV_RANK, B_HEADS, B_NOPE_DIM + B_V_DIM)
    w_ukv = jnp.concatenate([ukv[:, :, :B_NOPE_DIM].reshape(B_KV_RANK, -1),
                             ukv[:, :, B_NOPE_DIM:].reshape(B_KV_RANK, -1)], axis=1).astype(BF16)
    return (w_in, qn.reshape(1, -1), kn.reshape(1, -1), mqn.reshape(1, -1), mkvn.reshape(1, -1),
            w_uq, w_ukv)


def kernel(x, c, ctx, c_ctx, w_ada, b_ada, norm_mix, norm_ffn, attn_w_in, attn_q_norm, attn_k_norm,
           mla_q_norm, mla_kv_norm, mla_w_uq, mla_w_ukv, attn_w_o, sc_w_in, sc_conv, sc_w_out,
           ffn_w_up, ffn_conv, ffn_w_down, final_norm):
    bn, s, _ = x.shape
    cl = ctx.shape[1]
    assert bn <= MOD_ROWS // 2
    cvec = jnp.zeros((MOD_ROWS, D_MODEL), F32).at[:bn].set(c).at[MOD_ROWS // 2].set(c_ctx)
    mod = _adaln(cvec, w_ada, b_ada)
    rope = _rope_tables(s, A_HEAD_DIM) + _rope_tables(s, B_ROPE_DIM)

    w_o_all = attn_w_o.astype(BF16)
    sc_w_in_b, sc_w_out_b = sc_w_in.astype(BF16), sc_w_out.astype(BF16)
    ffn_w_up_b, ffn_w_down_b = ffn_w_up.astype(BF16), ffn_w_down.astype(BF16)
    half = A_HEADS * A_HEAD_DIM
    xc = ctx
    for l in range(DEPTH):
        later_attn = any(j % 2 == 0 for j in range(l + 1, DEPTH))
        i = l // 2
        nw = norm_mix[l].reshape(1, -1)
        if l % 2 == 0:
            wts = _attn_weights(attn_w_in[i], attn_q_norm[i], attn_k_norm[i], mla_q_norm[i],
                                mla_kv_norm[i], mla_w_uq[i], mla_w_ukv[i])
            w_o = [(w_o_all, (half, D_MODEL), (i, 0, 0)), (w_o_all, (half, D_MODEL), (i, 1, 0))]
            qa_c, qb_c, ka_c, va_c, kb_c, vb_c = _attn_proj(xc, mod, l, True, nw, wts, None, later_attn, cl)
            qa, qb, ka, va, kb, vb = _attn_proj(x, mod, l, False, nw, wts, rope, True, 512)
            oa = _flash(qa, ka_c, va_c, ka, va, group=A_GROUP, dq=A_HEAD_DIM, dv=A_HEAD_DIM,
                        tq=512, tk=512, name="gqa", depth=6)
            ob = _flash(qb, kb_c, vb_c, kb, vb, group=1, dq=B_QK_PAD, dv=B_V_DIM,
                        tq=2048, tk=512, name="mla", depth=6)
            x = _proj_res(x, mod, l, 2, False, [oa, ob], w_o, 512, name="attn_out")
            if later_attn:
                oa_c = _flash(qa_c, ka_c, va_c, None, None, group=A_GROUP, dq=A_HEAD_DIM, dv=A_HEAD_DIM,
                              tq=cl, tk=0, name="gqa_ctx")
                ob_c = _flash(qb_c, kb_c, vb_c, None, None, group=1, dq=B_QK_PAD, dv=B_V_DIM,
                              tq=cl, tk=0, name="mla_ctx")
                xc = _proj_res(xc, mod, l, 2, True, [oa_c, ob_c], w_o, cl, name="attn_out")
        else:
            w_out = [(sc_w_out_b, (SC_WIDTH, D_MODEL), (i, 0, 0))]
            z = _up(x, mod, l, False, nw, sc_w_in_b, sc_conv, i, "sc", 1024, 512)
            x = _proj_res(x, mod, l, 2, False, [z], w_out, 512, name="sc_out")
            if later_attn:
                zc = _up(xc, mod, l, True, nw, sc_w_in_b, sc_conv, i, "sc", 1024, 512)
                xc = _proj_res(xc, mod, l, 2, True, [zc], w_out, cl, name="sc_out")
        nwf = norm_ffn[l].reshape(1, -1)
        w_down = [(ffn_w_down_b, (D_FF, D_MODEL), (l, 0, 0))]
        hid = _up(x, mod, l, False, nwf, ffn_w_up_b, ffn_conv, l, "ffn", 1024, 512)
        fw = final_norm.reshape(1, -1) if l == DEPTH - 1 else None
        x = _proj_res(x, mod, l, 5, False, [hid], w_down, 512, final_w=fw, name="ffn_down")
        if later_attn:
            hid_c = _up(xc, mod, l, True, nwf, ffn_w_up_b, ffn_conv, l, "ffn", 1024, 512)
            xc = _proj_res(xc, mod, l, 5, True, [hid_c], w_down, cl, name="ffn_down")
    return x
```

```python
import functools

import jax
import jax.numpy as jnp
from jax import lax
from jax.experimental import pallas as pl
from jax.experimental.pallas import tpu as pltpu

F32 = jnp.float32
BF16 = jnp.bfloat16

D_MODEL = 2048
DEPTH = 4
GRID_W = 64
ROPE_THETA = 10000.0
EPS = 1e-6
A_HEADS = 8
A_KV_HEADS = 2
A_GROUP = A_HEADS // A_KV_HEADS
A_HEAD_DIM = 128
B_HEADS = 8
B_Q_RANK = 512
B_KV_RANK = 256
B_NOPE_DIM = 128
B_ROPE_DIM = 64
B_V_DIM = 128
B_QK_PAD = 256
LOG2E = 1.4426950408889634
A_SCALE = A_HEAD_DIM ** -0.5 * LOG2E
B_SCALE = (B_NOPE_DIM + B_ROPE_DIM) ** -0.5 * LOG2E
QGROUP = 256
V_ROWS = 128 + 16
SC_WIDTH = D_MODEL
D_FF = 256 * ((8 * D_MODEL // 3 + 255) // 256)
W_IN_PAD = 2432
MOD_ROWS = 16
HALO = 16
DOT_CHUNKS = 3
NORM_ROWS = 32
VMEM_LIMIT = 56 * 1024 * 1024


def _cparams(sem):
    return pltpu.CompilerParams(dimension_semantics=sem, vmem_limit_bytes=VMEM_LIMIT)


def _const_spec(shape):
    nd = len(shape)
    return pl.BlockSpec(shape, lambda *g: (0,) * nd, pipeline_mode=pl.Buffered(1))


def _layer_spec(shape, idx):
    return pl.BlockSpec((None,) + tuple(shape), lambda *g: idx, pipeline_mode=pl.Buffered(1))


def _mod_spec(layer, part, ctx):
    if ctx:
        idx = lambda b, *g: (layer, MOD_ROWS // 2, part, 0, 0)
    else:
        idx = lambda b, *g: (layer, b, part, 0, 0)
    return pl.BlockSpec((None, None, None, 1, D_MODEL), idx)


def _rms(x):
    return x * lax.rsqrt(jnp.mean(x * x, axis=-1, keepdims=True) + EPS)


def _rope(x, cos, sin_up, sin_dn, quarter):
    n = x.shape[-1]
    return x * cos + pltpu.roll(x, n - quarter, 1) * sin_up + pltpu.roll(x, quarter, 1) * sin_dn


def _adaln_body(c_ref, w_ref, b_ref, o_ref):
    c = c_ref[...]
    s = (c / (1.0 + jnp.exp(-c))).astype(BF16)
    o_ref[...] = jnp.dot(s, w_ref[...].astype(BF16), preferred_element_type=F32) + b_ref[...]


def _adaln(cvec, w_ada, b_ada):
    tn = 1024
    n = 6 * D_MODEL
    out = pl.pallas_call(
        _adaln_body,
        grid=(DEPTH, n // tn),
        in_specs=[
            pl.BlockSpec((MOD_ROWS, D_MODEL), lambda l, j: (0, 0)),
            pl.BlockSpec((None, D_MODEL, tn), lambda l, j: (l, 0, j)),
            pl.BlockSpec((None, 1, tn), lambda l, j: (l, 0, j)),
        ],
        out_specs=pl.BlockSpec((None, MOD_ROWS, tn), lambda l, j: (l, 0, j)),
        out_shape=jax.ShapeDtypeStruct((DEPTH, MOD_ROWS, n), F32),
        compiler_params=_cparams(("parallel", "parallel")),
        name="adaln",
    )(cvec, w_ada, b_ada.reshape(DEPTH, 1, n))
    return out.reshape(DEPTH, MOD_ROWS, 6, 1, D_MODEL)


def _attn_proj_body(*refs, use_rope, with_q):
    it = iter(refs)
    x_ref, sh_ref, sc_ref, nw_ref, win_ref = (next(it) for _ in range(5))
    qn_ref, kn_ref, mqn_ref, mkvn_ref, wuq_ref, wukv_ref = (next(it) for _ in range(6))
    if use_rope:
        ca, sau, sad, cb, sbu, sbd = (next(it)[...] for _ in range(6))
    if with_q:
        qa_ref, qb_ref = next(it), next(it)
    ka_ref, va_ref, kb_ref, vb_ref, h_scr = (next(it) for _ in range(5))

    def rope_a(v):
        return _rope(v, ca, sau, sad, A_HEAD_DIM // 4) if use_rope else v

    def rope_b(v):
        return _rope(v, cb, sbu, sbd, B_ROPE_DIM // 4) if use_rope else v

    gain = nw_ref[...] * (1.0 + sc_ref[...])
    sh = sh_ref[...]
    tm = x_ref.shape[0]
    for r in range(0, tm, NORM_ROWS):
        h_scr[r:r + NORM_ROWS, :] = (_rms(x_ref[r:r + NORM_ROWS, :]) * gain + sh).astype(BF16)
    proj = jnp.dot(h_scr[...], win_ref[...], preferred_element_type=F32)
    q_end = A_HEADS * A_HEAD_DIM
    k_end = q_end + A_KV_HEADS * A_HEAD_DIM
    v_end = k_end + A_KV_HEADS * A_HEAD_DIM
    cq_end = v_end + B_Q_RANK
    ckv_end = cq_end + B_KV_RANK
    nope_w = B_HEADS * B_NOPE_DIM

    ckv = (_rms(proj[:, cq_end:ckv_end]) * mkvn_ref[...]).astype(BF16)
    kv = jnp.dot(ckv, wukv_ref[...], preferred_element_type=F32)
    if with_q:
        cq = (_rms(proj[:, v_end:cq_end]) * mqn_ref[...]).astype(BF16)
        qb = jnp.dot(cq, wuq_ref[...], preferred_element_type=F32)

    def put_vt(ref, hh, v):
        lo = hh * V_ROWS
        ref[lo:lo + v.shape[1], :] = v.T.astype(BF16)
        ref[lo + v.shape[1]:lo + V_ROWS, :] = jnp.ones((V_ROWS - v.shape[1], v.shape[0]), BF16)

    kn = kn_ref[...]
    for kh in range(A_KV_HEADS):
        lo = q_end + kh * A_HEAD_DIM
        k = rope_a(_rms(proj[:, lo:lo + A_HEAD_DIM]) * kn)
        ka_ref[:, kh * A_HEAD_DIM:(kh + 1) * A_HEAD_DIM] = k.astype(BF16)
        lo = k_end + kh * A_HEAD_DIM
        put_vt(va_ref, kh, proj[:, lo:lo + A_HEAD_DIM])
    if with_q:
        qn = qn_ref[...]
        for hh in range(A_HEADS):
            lo = hh * A_HEAD_DIM
            q = rope_a(_rms(proj[:, lo:lo + A_HEAD_DIM]) * qn) * A_SCALE
            qa_ref[:, lo:lo + A_HEAD_DIM] = q.astype(BF16)

    kr = rope_b(proj[:, ckv_end:W_IN_PAD]).astype(BF16)
    for hh in range(B_HEADS):
        put_vt(vb_ref, hh, kv[:, nope_w + hh * B_V_DIM:nope_w + (hh + 1) * B_V_DIM])
        lo = hh * B_QK_PAD
        kb_ref[:, lo:lo + B_NOPE_DIM] = kv[:, hh * B_NOPE_DIM:(hh + 1) * B_NOPE_DIM].astype(BF16)
        kb_ref[:, lo + B_NOPE_DIM:lo + B_QK_PAD] = kr
    if with_q:
        for hh in range(B_HEADS):
            lo = hh * B_QK_PAD
            qb_ref[:, lo:lo + B_NOPE_DIM] = (qb[:, hh * B_NOPE_DIM:(hh + 1) * B_NOPE_DIM] * B_SCALE).astype(BF16)
            r = rope_b(qb[:, nope_w + hh * 128:nope_w + (hh + 1) * 128]) * B_SCALE
            qb_ref[:, lo + B_NOPE_DIM:lo + B_QK_PAD] = r.astype(BF16)


def _attn_proj(x, mod, layer, ctx, nw, wts, rope, with_q, tm):
    bn, s, _ = x.shape
    use_rope = rope is not None
    w_in, qn, kn, mqn, mkvn, w_uq, w_ukv = wts
    row = lambda w: pl.BlockSpec((None, tm, w), lambda b, i: (b, i, 0))
    in_specs = [row(D_MODEL), _mod_spec(layer, 0, ctx), _mod_spec(layer, 1, ctx),
                _const_spec((1, D_MODEL)), _const_spec(w_in.shape),
                _const_spec(qn.shape), _const_spec(kn.shape), _const_spec(mqn.shape),
                _const_spec(mkvn.shape), _const_spec(w_uq.shape), _const_spec(w_ukv.shape)]
    args = [x, mod, mod, nw, w_in, qn, kn, mqn, mkvn, w_uq, w_ukv]
    if use_rope:
        in_specs += [pl.BlockSpec((tm, 128), lambda b, i: (i, 0))] * 6
        args += list(rope)
    col = lambda w: pl.BlockSpec((None, w, tm), lambda b, i: (b, 0, i))
    outs_desc = ([(A_HEADS * A_HEAD_DIM, False), (B_HEADS * B_QK_PAD, False)] if with_q else []) + [
        (A_KV_HEADS * A_HEAD_DIM, False), (A_KV_HEADS * V_ROWS, True),
        (B_HEADS * B_QK_PAD, False), (B_HEADS * V_ROWS, True)]
    outs = pl.pallas_call(
        functools.partial(_attn_proj_body, use_rope=use_rope, with_q=with_q),
        grid=(bn, s // tm),
        in_specs=in_specs,
        out_specs=[col(w) if t else row(w) for w, t in outs_desc],
        out_shape=[jax.ShapeDtypeStruct((bn, w, s) if t else (bn, s, w), BF16) for w, t in outs_desc],
        scratch_shapes=[pltpu.VMEM((tm, D_MODEL), BF16)],
        compiler_params=_cparams(("parallel", "parallel")),
        name="attn_proj_ctx" if ctx else "attn_proj",
    )(*args)
    return outs if with_q else [None, None] + list(outs)


def _col_reduce(x, op, final):
    while x.shape[0] > 8 and x.shape[0] % 16 == 0:
        h = x.shape[0] // 2
        x = op(x[:h], x[h:])
    return final(x, axis=0, keepdims=True)


def _flash_body(*refs, group, dq, dv, tk, n_lat, depth):
    if n_lat:
        q_ref, kc_ref, vct_ref, k_ref, vt_ref, o_ref = refs
    else:
        q_ref, kc_ref, vct_ref, o_ref = refs
    tq = q_ref.shape[0]
    rows = [slice(r, r + QGROUP) for r in range(0, tq, QGROUP)]
    q_parts = [(r, slice(g * dq, (g + 1) * dq)) for r in rows for g in range(group)]
    o_parts = [(r, slice(g * dv, (g + 1) * dv)) for r in rows for g in range(group)]
    n = len(q_parts)
    chunks = [(kc_ref, vct_ref, 0, kc_ref.shape[0])] + [(k_ref, vt_ref, c * tk, tk) for c in range(n_lat)]
    m, acc = [None] * n, [None] * n
    blocks = [(ci, g) for ci in range(len(chunks)) for g in range(n)]

    def scores(ci, g):
        kr, _, off, size = chunks[ci]
        return lax.dot_general(kr[off:off + size, :], q_ref[q_parts[g]], (((1,), (1,)), ((), ())),
                               preferred_element_type=F32)

    s_next = [scores(*blocks[d]) for d in range(min(depth, len(blocks)))]
    for idx, (ci, g) in enumerate(blocks):
        s = s_next.pop(0)
        if idx + depth < len(blocks):
            s_next.append(scores(*blocks[idx + depth]))
        _, vr, off, size = chunks[ci]
        vt = vr[:, off:off + size]
        smax = _col_reduce(s, jnp.maximum, jnp.max)
        if ci == 0:
            m_new = smax
            p = jnp.exp2(s - m_new)
            acc[g] = jnp.dot(vt, p.astype(BF16), preferred_element_type=F32)
        else:
            m_new = jnp.maximum(m[g], smax)
            alpha = jnp.exp2(m[g] - m_new)
            p = jnp.exp2(s - m_new)
            acc[g] = alpha * acc[g] + jnp.dot(vt, p.astype(BF16), preferred_element_type=F32)
        m[g] = m_new
    for g in range(n):
        o_ref[o_parts[g]] = (acc[g][:dv] / acc[g][dv:dv + 1]).T.astype(o_ref.dtype)


def _flash(q, kc, vct, k, vt, *, group, dq, dv, tq, tk, name, depth=4):
    bn, s, qw = q.shape
    kvh = qw // (group * dq)
    c = kc.shape[1]
    n_lat = 0 if k is None else k.shape[1] // tk
    in_specs = [pl.BlockSpec((None, tq, group * dq), lambda b, h, i: (b, i, h)),
                pl.BlockSpec((None, c, dq), lambda b, h, i: (b, 0, h)),
                pl.BlockSpec((None, V_ROWS, c), lambda b, h, i: (b, h, 0))]
    args = [q, kc, vct]
    if n_lat:
        t = k.shape[1]
        in_specs += [pl.BlockSpec((None, t, dq), lambda b, h, i: (b, 0, h)),
                     pl.BlockSpec((None, V_ROWS, t), lambda b, h, i: (b, h, 0))]
        args += [k, vt]
    return pl.pallas_call(
        functools.partial(_flash_body, group=group, dq=dq, dv=dv, tk=tk, n_lat=n_lat, depth=depth),
        grid=(bn, kvh, s // tq),
        in_specs=in_specs,
        out_specs=pl.BlockSpec((None, tq, group * dv), lambda b, h, i: (b, i, h)),
        out_shape=jax.ShapeDtypeStruct((bn, s, kvh * group * dv), BF16),
        compiler_params=_cparams(("parallel", "parallel", "parallel")),
        name=name,
    )(*args)


def _proj_res_body(*refs, n_lhs, final):
    x_ref, g_ref = refs[0], refs[1]
    lhs = refs[2:2 + n_lhs]
    ws = refs[2 + n_lhs:2 + 2 * n_lhs]
    rest = refs[2 + 2 * n_lhs:]
    y = jnp.dot(lhs[0][...], ws[0][...], preferred_element_type=F32)
    for a, w in zip(lhs[1:], ws[1:]):
        y = y + jnp.dot(a[...], w[...], preferred_element_type=F32)
    out = x_ref[...] + g_ref[...] * y
    if final:
        fw_ref, o_ref = rest
        out = _rms(out) * fw_ref[...]
    else:
        (o_ref,) = rest
    o_ref[...] = out


def _proj_res(x, mod, layer, part, ctx, lhs, ws, tm, final_w=None, name="proj_res"):
    bn, s, _ = x.shape
    row = lambda w: pl.BlockSpec((None, tm, w), lambda b, i: (b, i, 0))
    in_specs = [row(D_MODEL), _mod_spec(layer, part, ctx)]
    in_specs += [row(a.shape[-1]) for a in lhs] + [_layer_spec(shape, idx) for _, shape, idx in ws]
    args = [x, mod] + list(lhs) + [w for w, _, _ in ws]
    if final_w is not None:
        in_specs.append(_const_spec((1, D_MODEL)))
        args.append(final_w)
    return pl.pallas_call(
        functools.partial(_proj_res_body, n_lhs=len(lhs), final=final_w is not None),
        grid=(bn, s // tm),
        in_specs=in_specs,
        out_specs=row(D_MODEL),
        out_shape=jax.ShapeDtypeStruct(x.shape, F32),
        compiler_params=_cparams(("parallel", "parallel")),
        name=name + ("_ctx" if ctx else ""),
    )(*args)


def _up_body(*refs, mode, tm, seg, dot_chunks):
    x_ref, xp_ref, xn_ref, sh_ref, sc_ref, nw_ref = refs[:6]
    i = pl.program_id(1)
    j = pl.program_id(2)
    if mode == "ffn":
        wg_ref, wu_ref, cg_ref, cu_ref, o_ref, h_scr = refs[6:]
    else:
        wb_ref, wc_ref, wv_ref, cw_ref, o_ref, h_scr = refs[6:]
    whole = seg >= tm

    @pl.when(j == 0)
    def _():
        gain = nw_ref[...] * (1.0 + sc_ref[...])
        sh = sh_ref[...]
        for r in range(0, tm, NORM_ROWS):
            h_scr[HALO + r:HALO + r + NORM_ROWS, :] = (
                _rms(x_ref[r:r + NORM_ROWS, :]) * gain + sh).astype(BF16)
        first = (i * tm) % seg == 0 if whole else True
        last = ((i + 1) * tm) % seg == 0 if whole else True
        hp = _rms(xp_ref[...]) * gain + sh
        h_scr[0:HALO, :] = jnp.where(first, 0.0, hp).astype(BF16)
        hn = _rms(xn_ref[...]) * gain + sh
        h_scr[HALO + tm:2 * HALO + tm, :] = jnp.where(last, 0.0, hn).astype(BF16)

    def conv(full, cw_ref):
        rows = full.shape[0]
        prev = pltpu.roll(full, 1, 0)[HALO:HALO + tm, :]
        nxt = pltpu.roll(full, rows - 1, 0)[HALO:HALO + tm, :]
        if not whole:
            pos = lax.broadcasted_iota(jnp.int32, (tm, 1), 0) % seg
            prev = jnp.where(pos == 0, 0.0, prev)
            nxt = jnp.where(pos == seg - 1, 0.0, nxt)
        return cw_ref[0:1, :] * prev + cw_ref[1:2, :] * full[HALO:HALO + tm, :] + cw_ref[2:3, :] * nxt

    h = h_scr[...]
    def up(w_ref):
        n = h.shape[0] // dot_chunks
        return jnp.concatenate([jnp.dot(h[k * n:(k + 1) * n], w_ref[...], preferred_element_type=F32)
                                for k in range(dot_chunks)], axis=0)

    if mode == "ffn":
        g = conv(up(wg_ref), cg_ref)
        u = conv(up(wu_ref), cu_ref)
        o_ref[...] = ((g / (1.0 + jnp.exp(-g))) * u).astype(o_ref.dtype)
    else:
        cv = up(wc_ref) * up(wv_ref)
        b = jnp.dot(h_scr[HALO:HALO + tm, :], wb_ref[...], preferred_element_type=F32)
        o_ref[...] = (b * conv(cv, cw_ref)).astype(o_ref.dtype)


def _up(x, mod, layer, ctx, nw, w, cw, wl, mode, tm, tn, dot_chunks=DOT_CHUNKS):
    seg = x.shape[1]
    if ctx:
        x = x.reshape(1, -1, D_MODEL)
    bn, s, _ = x.shape
    parts = 2 if mode == "ffn" else 3
    width = w.shape[2] // parts
    nj = width // tn
    hb = tm // HALO
    last_hb = s // HALO - 1
    in_specs = [
        pl.BlockSpec((None, tm, D_MODEL), lambda b, i, j: (b, i, 0)),
        pl.BlockSpec((None, HALO, D_MODEL), lambda b, i, j: (b, jnp.maximum(i * hb - 1, 0), 0)),
        pl.BlockSpec((None, HALO, D_MODEL), lambda b, i, j: (b, jnp.minimum((i + 1) * hb, last_hb), 0)),
        _mod_spec(layer, 3 if mode == "ffn" else 0, ctx),
        _mod_spec(layer, 4 if mode == "ffn" else 1, ctx),
        _const_spec((1, D_MODEL)),
    ]
    args = [x, x, x, mod, mod, nw]
    wspec = lambda p: pl.BlockSpec((None, D_MODEL, tn), lambda b, i, j: (wl, 0, p * nj + j))
    cspec = lambda p: pl.BlockSpec((None, 3, tn), lambda b, i, j: (wl, 0, p * nj + j))
    rows = tm + 2 * HALO
    scratch = [pltpu.VMEM((rows, D_MODEL), BF16)]
    if mode == "ffn":
        in_specs += [wspec(0), wspec(1), cspec(0), cspec(1)]
        args += [w, w, cw, cw]
    else:
        in_specs += [wspec(0), wspec(1), wspec(2), cspec(0)]
        args += [w, w, w, cw]
    out = pl.pallas_call(
        functools.partial(_up_body, mode=mode, tm=tm, seg=seg, dot_chunks=dot_chunks),
        grid=(bn, s // tm, nj),
        in_specs=in_specs,
        out_specs=pl.BlockSpec((None, tm, tn), lambda b, i, j: (b, i, j)),
        out_shape=jax.ShapeDtypeStruct((bn, s, width), BF16),
        scratch_shapes=scratch,
        compiler_params=_cparams(("parallel", "parallel", "arbitrary")),
        name=mode + "_up" + ("_ctx" if ctx else ""),
    )(*args)
    return out.reshape(-1, seg, width)


def _rope_tables(seq, rot_dim):
    rows = seq // GRID_W
    r = jnp.repeat(jnp.arange(rows, dtype=F32), GRID_W)
    col = jnp.tile(jnp.arange(GRID_W, dtype=F32), rows)
    quarter = rot_dim // 4
    inv = ROPE_THETA ** (-jnp.arange(quarter, dtype=F32) / quarter)
    ar = r[:, None] * inv
    ac = col[:, None] * inv
    ang = jnp.concatenate([ar, ar, ac, ac], axis=-1)
    reps = 128 // rot_dim
    cos = jnp.tile(jnp.cos(ang), (1, reps))
    sin = jnp.tile(jnp.sin(ang), (1, reps))
    first = ((jnp.arange(128) // quarter) % 2 == 0)[None, :]
    return cos, jnp.where(first, -sin, 0.0), jnp.where(first, 0.0, sin)


def _attn_weights(w_in, qn, kn, mqn, mkvn, w_uq, w_ukv):
    w_in = jnp.pad(w_in, ((0, 0), (0, W_IN_PAD - w_in.shape[1]))).astype(BF16)
    uq = w_uq.reshape(B_Q_RANK, B_HEADS, B_NOPE_DIM + B_ROPE_DIM)
    uq_rope = jnp.pad(uq[:, :, B_NOPE_DIM:], ((0, 0), (0, 0), (0, 128 - B_ROPE_DIM)))
    w_uq = jnp.concatenate([uq[:, :, :B_NOPE_DIM].reshape(B_Q_RANK, -1),
                            uq_rope.reshape(B_Q_RANK, -1)], axis=1).astype(BF16)
    ukv = w_ukv.reshape(B_KV_RANK, B_HEADS, B_NOPE_DIM + B_V_DIM)
    w_ukv = jnp.concatenate([ukv[:, :, :B_NOPE_DIM].reshape(B_KV_RANK, -1),
                             ukv[:, :, B_NOPE_DIM:].reshape(B_KV_RANK, -1)], axis=1).astype(BF16)
    return (w_in, qn.reshape(1, -1), kn.reshape(1, -1), mqn.reshape(1, -1), mkvn.reshape(1, -1),
            w_uq, w_ukv)


def kernel(x, c, ctx, c_ctx, w_ada, b_ada, norm_mix, norm_ffn, attn_w_in, attn_q_norm, attn_k_norm,
           mla_q_norm, mla_kv_norm, mla_w_uq, mla_w_ukv, attn_w_o, sc_w_in, sc_conv, sc_w_out,
           ffn_w_up, ffn_conv, ffn_w_down, final_norm):
    bn, s, _ = x.shape
    cl = ctx.shape[1]
    assert bn <= MOD_ROWS // 2
    cvec = jnp.zeros((MOD_ROWS, D_MODEL), F32).at[:bn].set(c).at[MOD_ROWS // 2].set(c_ctx)
    mod = _adaln(cvec, w_ada, b_ada)
    rope = _rope_tables(s, A_HEAD_DIM) + _rope_tables(s, B_ROPE_DIM)

    w_o_all = attn_w_o.astype(BF16)
    sc_w_in_b, sc_w_out_b = sc_w_in.astype(BF16), sc_w_out.astype(BF16)
    ffn_w_up_b, ffn_w_down_b = ffn_w_up.astype(BF16), ffn_w_down.astype(BF16)
    half = A_HEADS * A_HEAD_DIM
    xc = ctx
    for l in range(DEPTH):
        later_attn = any(j % 2 == 0 for j in range(l + 1, DEPTH))
        i = l // 2
        nw = norm_mix[l].reshape(1, -1)
        if l % 2 == 0:
            wts = _attn_weights(attn_w_in[i], attn_q_norm[i], attn_k_norm[i], mla_q_norm[i],
                                mla_kv_norm[i], mla_w_uq[i], mla_w_ukv[i])
            w_o = [(w_o_all, (half, D_MODEL), (i, 0, 0)), (w_o_all, (half, D_MODEL), (i, 1, 0))]
            qa_c, qb_c, ka_c, va_c, kb_c, vb_c = _attn_proj(xc, mod, l, True, nw, wts, None, later_attn, cl)
            qa, qb, ka, va, kb, vb = _attn_proj(x, mod, l, False, nw, wts, rope, True, 512)
            oa = _flash(qa, ka_c, va_c, ka, va, group=A_GROUP, dq=A_HEAD_DIM, dv=A_HEAD_DIM,
                        tq=512, tk=512, name="gqa", depth=6)
            ob = _flash(qb, kb_c, vb_c, kb, vb, group=1, dq=B_QK_PAD, dv=B_V_DIM,
                        tq=2048, tk=512, name="mla", depth=6)
            x = _proj_res(x, mod, l, 2, False, [oa, ob], w_o, 512, name="attn_out")
            if later_attn:
                oa_c = _flash(qa_c, ka_c, va_c, None, None, group=A_GROUP, dq=A_HEAD_DIM, dv=A_HEAD_DIM,
                              tq=cl, tk=0, name="gqa_ctx")
                ob_c = _flash(qb_c, kb_c, vb_c, None, None, group=1, dq=B_QK_PAD, dv=B_V_DIM,
                              tq=cl, tk=0, name="mla_ctx")
                xc = _proj_res(xc, mod, l, 2, True, [oa_c, ob_c], w_o, cl, name="attn_out")
        else:
            w_out = [(sc_w_out_b, (SC_WIDTH, D_MODEL), (i, 0, 0))]
            z = _up(x, mod, l, False, nw, sc_w_in_b, sc_conv, i, "sc", 1024, 512, dot_chunks=(3, 1)[i])
            x = _proj_res(x, mod, l, 2, False, [z], w_out, 512, name="sc_out")
            if later_attn:
                zc = _up(xc, mod, l, True, nw, sc_w_in_b, sc_conv, i, "sc", 1024, 512)
                xc = _proj_res(xc, mod, l, 2, True, [zc], w_out, cl, name="sc_out")
        nwf = norm_ffn[l].reshape(1, -1)
        w_down = [(ffn_w_down_b, (D_FF, D_MODEL), (l, 0, 0))]
        hid = _up(x, mod, l, False, nwf, ffn_w_up_b, ffn_conv, l, "ffn", 1024, 512, dot_chunks=(3, 6, 2, 1)[l])
        fw = final_norm.reshape(1, -1) if l == DEPTH - 1 else None
        x = _proj_res(x, mod, l, 5, False, [hid], w_down, 512, final_w=fw, name="ffn_down")
        if later_attn:
            hid_c = _up(xc, mod, l, True, nwf, ffn_w_up_b, ffn_conv, l, "ffn", 1024, 512)
            xc = _proj_res(xc, mod, l, 5, True, [hid_c], w_down, cl, name="ffn_down")
    return x
```

```python
import functools

import jax
import jax.numpy as jnp
from jax import lax
from jax.experimental import pallas as pl
from jax.experimental.pallas import tpu as pltpu

F32 = jnp.float32
BF16 = jnp.bfloat16

D_MODEL = 2048
DEPTH = 4
GRID_W = 64
ROPE_THETA = 10000.0
EPS = 1e-6
A_HEADS = 8
A_KV_HEADS = 2
A_GROUP = A_HEADS // A_KV_HEADS
A_HEAD_DIM = 128
B_HEADS = 8
B_Q_RANK = 512
B_KV_RANK = 256
B_NOPE_DIM = 128
B_ROPE_DIM = 64
B_V_DIM = 128
B_QK_PAD = 256
LOG2E = 1.4426950408889634
A_SCALE = A_HEAD_DIM ** -0.5 * LOG2E
B_SCALE = (B_NOPE_DIM + B_ROPE_DIM) ** -0.5 * LOG2E
QGROUP = 256
V_ROWS = 128 + 16
SC_WIDTH = D_MODEL
D_FF = 256 * ((8 * D_MODEL // 3 + 255) // 256)
W_IN_PAD = 2432
MOD_ROWS = 16
HALO = 16
DOT_CHUNKS = 3
NORM_ROWS = 32
VMEM_LIMIT = 56 * 1024 * 1024


def _cparams(sem):
    return pltpu.CompilerParams(dimension_semantics=sem, vmem_limit_bytes=VMEM_LIMIT)


def _const_spec(shape):
    nd = len(shape)
    return pl.BlockSpec(shape, lambda *g: (0,) * nd, pipeline_mode=pl.Buffered(1))


def _layer_spec(shape, idx):
    return pl.BlockSpec((None,) + tuple(shape), lambda *g: idx, pipeline_mode=pl.Buffered(1))


def _mod_spec(layer, part, ctx):
    if ctx:
        idx = lambda b, *g: (layer, MOD_ROWS // 2, part, 0, 0)
    else:
        idx = lambda b, *g: (layer, b, part, 0, 0)
    return pl.BlockSpec((None, None, None, 1, D_MODEL), idx)


def _rms(x):
    return x * lax.rsqrt(jnp.mean(x * x, axis=-1, keepdims=True) + EPS)


def _dot_rows(a, w, chunks):
    n = a.shape[0] // chunks
    return jnp.concatenate([jnp.dot(a[k * n:(k + 1) * n], w, preferred_element_type=F32)
                            for k in range(chunks)], axis=0)


def _rope(x, cos, sin_up, sin_dn, quarter):
    n = x.shape[-1]
    return x * cos + pltpu.roll(x, n - quarter, 1) * sin_up + pltpu.roll(x, quarter, 1) * sin_dn


def _adaln_body(c_ref, w_ref, b_ref, o_ref):
    c = c_ref[...]
    s = (c / (1.0 + jnp.exp(-c))).astype(BF16)
    o_ref[...] = jnp.dot(s, w_ref[...].astype(BF16), preferred_element_type=F32) + b_ref[...]


def _adaln(cvec, w_ada, b_ada):
    tn = 1024
    n = 6 * D_MODEL
    out = pl.pallas_call(
        _adaln_body,
        grid=(DEPTH, n // tn),
        in_specs=[
            pl.BlockSpec((MOD_ROWS, D_MODEL), lambda l, j: (0, 0)),
            pl.BlockSpec((None, D_MODEL, tn), lambda l, j: (l, 0, j)),
            pl.BlockSpec((None, 1, tn), lambda l, j: (l, 0, j)),
        ],
        out_specs=pl.BlockSpec((None, MOD_ROWS, tn), lambda l, j: (l, 0, j)),
        out_shape=jax.ShapeDtypeStruct((DEPTH, MOD_ROWS, n), F32),
        compiler_params=_cparams(("parallel", "parallel")),
        name="adaln",
    )(cvec, w_ada, b_ada.reshape(DEPTH, 1, n))
    return out.reshape(DEPTH, MOD_ROWS, 6, 1, D_MODEL)


def _attn_proj_body(*refs, use_rope, with_q, chunks):
    it = iter(refs)
    x_ref, sh_ref, sc_ref, nw_ref, win_ref = (next(it) for _ in range(5))
    qn_ref, kn_ref, mqn_ref, mkvn_ref, wuq_ref, wukv_ref = (next(it) for _ in range(6))
    if use_rope:
        ca, sau, sad, cb, sbu, sbd = (next(it)[...] for _ in range(6))
    if with_q:
        qa_ref, qb_ref = next(it), next(it)
    ka_ref, va_ref, kb_ref, vb_ref, h_scr = (next(it) for _ in range(5))

    def rope_a(v):
        return _rope(v, ca, sau, sad, A_HEAD_DIM // 4) if use_rope else v

    def rope_b(v):
        return _rope(v, cb, sbu, sbd, B_ROPE_DIM // 4) if use_rope else v

    gain = nw_ref[...] * (1.0 + sc_ref[...])
    sh = sh_ref[...]
    tm = x_ref.shape[0]
    for r in range(0, tm, NORM_ROWS):
        h_scr[r:r + NORM_ROWS, :] = (_rms(x_ref[r:r + NORM_ROWS, :]) * gain + sh).astype(BF16)
    proj = _dot_rows(h_scr[...], win_ref[...], chunks)
    q_end = A_HEADS * A_HEAD_DIM
    k_end = q_end + A_KV_HEADS * A_HEAD_DIM
    v_end = k_end + A_KV_HEADS * A_HEAD_DIM
    cq_end = v_end + B_Q_RANK
    ckv_end = cq_end + B_KV_RANK
    nope_w = B_HEADS * B_NOPE_DIM

    ckv = (_rms(proj[:, cq_end:ckv_end]) * mkvn_ref[...]).astype(BF16)
    kv = _dot_rows(ckv, wukv_ref[...], chunks)
    if with_q:
        cq = (_rms(proj[:, v_end:cq_end]) * mqn_ref[...]).astype(BF16)
        qb = _dot_rows(cq, wuq_ref[...], chunks)

    def put_vt(ref, hh, v):
        lo = hh * V_ROWS
        ref[lo:lo + v.shape[1], :] = v.T.astype(BF16)
        ref[lo + v.shape[1]:lo + V_ROWS, :] = jnp.ones((V_ROWS - v.shape[1], v.shape[0]), BF16)

    kn = kn_ref[...]
    for kh in range(A_KV_HEADS):
        lo = q_end + kh * A_HEAD_DIM
        k = rope_a(_rms(proj[:, lo:lo + A_HEAD_DIM]) * kn)
        ka_ref[:, kh * A_HEAD_DIM:(kh + 1) * A_HEAD_DIM] = k.astype(BF16)
        lo = k_end + kh * A_HEAD_DIM
        put_vt(va_ref, kh, proj[:, lo:lo + A_HEAD_DIM])
    if with_q:
        qn = qn_ref[...]
        for hh in range(A_HEADS):
            lo = hh * A_HEAD_DIM
            q = rope_a(_rms(proj[:, lo:lo + A_HEAD_DIM]) * qn) * A_SCALE
            qa_ref[:, lo:lo + A_HEAD_DIM] = q.astype(BF16)

    kr = rope_b(proj[:, ckv_end:W_IN_PAD]).astype(BF16)
    for hh in range(B_HEADS):
        put_vt(vb_ref, hh, kv[:, nope_w + hh * B_V_DIM:nope_w + (hh + 1) * B_V_DIM])
        lo = hh * B_QK_PAD
        kb_ref[:, lo:lo + B_NOPE_DIM] = kv[:, hh * B_NOPE_DIM:(hh + 1) * B_NOPE_DIM].astype(BF16)
        kb_ref[:, lo + B_NOPE_DIM:lo + B_QK_PAD] = kr
    if with_q:
        for hh in range(B_HEADS):
            lo = hh * B_QK_PAD
            qb_ref[:, lo:lo + B_NOPE_DIM] = (qb[:, hh * B_NOPE_DIM:(hh + 1) * B_NOPE_DIM] * B_SCALE).astype(BF16)
            r = rope_b(qb[:, nope_w + hh * 128:nope_w + (hh + 1) * 128]) * B_SCALE
            qb_ref[:, lo + B_NOPE_DIM:lo + B_QK_PAD] = r.astype(BF16)


def _attn_proj(x, mod, layer, ctx, nw, wts, rope, with_q, tm, chunks=1):
    bn, s, _ = x.shape
    use_rope = rope is not None
    w_in, qn, kn, mqn, mkvn, w_uq, w_ukv = wts
    row = lambda w: pl.BlockSpec((None, tm, w), lambda b, i: (b, i, 0))
    in_specs = [row(D_MODEL), _mod_spec(layer, 0, ctx), _mod_spec(layer, 1, ctx),
                _const_spec((1, D_MODEL)), _const_spec(w_in.shape),
                _const_spec(qn.shape), _const_spec(kn.shape), _const_spec(mqn.shape),
                _const_spec(mkvn.shape), _const_spec(w_uq.shape), _const_spec(w_ukv.shape)]
    args = [x, mod, mod, nw, w_in, qn, kn, mqn, mkvn, w_uq, w_ukv]
    if use_rope:
        in_specs += [pl.BlockSpec((tm, 128), lambda b, i: (i, 0))] * 6
        args += list(rope)
    col = lambda w: pl.BlockSpec((None, w, tm), lambda b, i: (b, 0, i))
    outs_desc = ([(A_HEADS * A_HEAD_DIM, False), (B_HEADS * B_QK_PAD, False)] if with_q else []) + [
        (A_KV_HEADS * A_HEAD_DIM, False), (A_KV_HEADS * V_ROWS, True),
        (B_HEADS * B_QK_PAD, False), (B_HEADS * V_ROWS, True)]
    outs = pl.pallas_call(
        functools.partial(_attn_proj_body, use_rope=use_rope, with_q=with_q, chunks=chunks),
        grid=(bn, s // tm),
        in_specs=in_specs,
        out_specs=[col(w) if t else row(w) for w, t in outs_desc],
        out_shape=[jax.ShapeDtypeStruct((bn, w, s) if t else (bn, s, w), BF16) for w, t in outs_desc],
        scratch_shapes=[pltpu.VMEM((tm, D_MODEL), BF16)],
        compiler_params=_cparams(("parallel", "parallel")),
        name="attn_proj_ctx" if ctx else "attn_proj",
    )(*args)
    return outs if with_q else [None, None] + list(outs)


def _col_reduce(x, op, final):
    while x.shape[0] > 8 and x.shape[0] % 16 == 0:
        h = x.shape[0] // 2
        x = op(x[:h], x[h:])
    return final(x, axis=0, keepdims=True)


def _flash_body(*refs, group, dq, dv, tk, n_lat, depth):
    if n_lat:
        q_ref, kc_ref, vct_ref, k_ref, vt_ref, o_ref = refs
    else:
        q_ref, kc_ref, vct_ref, o_ref = refs
    tq = q_ref.shape[0]
    rows = [slice(r, r + QGROUP) for r in range(0, tq, QGROUP)]
    q_parts = [(r, slice(g * dq, (g + 1) * dq)) for r in rows for g in range(group)]
    o_parts = [(r, slice(g * dv, (g + 1) * dv)) for r in rows for g in range(group)]
    n = len(q_parts)
    chunks = [(kc_ref, vct_ref, 0, kc_ref.shape[0])] + [(k_ref, vt_ref, c * tk, tk) for c in range(n_lat)]
    m, acc = [None] * n, [None] * n
    blocks = [(ci, g) for ci in range(len(chunks)) for g in range(n)]

    def scores(ci, g):
        kr, _, off, size = chunks[ci]
        return lax.dot_general(kr[off:off + size, :], q_ref[q_parts[g]], (((1,), (1,)), ((), ())),
                               preferred_element_type=F32)

    s_next = [scores(*blocks[d]) for d in range(min(depth, len(blocks)))]
    for idx, (ci, g) in enumerate(blocks):
        s = s_next.pop(0)
        if idx + depth < len(blocks):
            s_next.append(scores(*blocks[idx + depth]))
        _, vr, off, size = chunks[ci]
        vt = vr[:, off:off + size]
        smax = _col_reduce(s, jnp.maximum, jnp.max)
        if ci == 0:
            m_new = smax
            p = jnp.exp2(s - m_new)
            acc[g] = jnp.dot(vt, p.astype(BF16), preferred_element_type=F32)
        else:
            m_new = jnp.maximum(m[g], smax)
            alpha = jnp.exp2(m[g] - m_new)
            p = jnp.exp2(s - m_new)
            acc[g] = alpha * acc[g] + jnp.dot(vt, p.astype(BF16), preferred_element_type=F32)
        m[g] = m_new
    for g in range(n):
        o_ref[o_parts[g]] = (acc[g][:dv] / acc[g][dv:dv + 1]).T.astype(o_ref.dtype)


def _flash(q, kc, vct, k, vt, *, group, dq, dv, tq, tk, name, depth=4):
    bn, s, qw = q.shape
    kvh = qw // (group * dq)
    c = kc.shape[1]
    n_lat = 0 if k is None else k.shape[1] // tk
    in_specs = [pl.BlockSpec((None, tq, group * dq), lambda b, h, i: (b, i, h)),
                pl.BlockSpec((None, c, dq), lambda b, h, i: (b, 0, h)),
                pl.BlockSpec((None, V_ROWS, c), lambda b, h, i: (b, h, 0))]
    args = [q, kc, vct]
    if n_lat:
        t = k.shape[1]
        in_specs += [pl.BlockSpec((None, t, dq), lambda b, h, i: (b, 0, h)),
                     pl.BlockSpec((None, V_ROWS, t), lambda b, h, i: (b, h, 0))]
        args += [k, vt]
    return pl.pallas_call(
        functools.partial(_flash_body, group=group, dq=dq, dv=dv, tk=tk, n_lat=n_lat, depth=depth),
        grid=(bn, kvh, s // tq),
        in_specs=in_specs,
        out_specs=pl.BlockSpec((None, tq, group * dv), lambda b, h, i: (b, i, h)),
        out_shape=jax.ShapeDtypeStruct((bn, s, kvh * group * dv), BF16),
        compiler_params=_cparams(("parallel", "parallel", "parallel")),
        name=name,
    )(*args)


def _proj_res_body(*refs, n_lhs, final, chunks):
    x_ref, g_ref = refs[0], refs[1]
    lhs = refs[2:2 + n_lhs]
    ws = refs[2 + n_lhs:2 + 2 * n_lhs]
    rest = refs[2 + 2 * n_lhs:]
    y = _dot_rows(lhs[0][...], ws[0][...], chunks)
    for a, w in zip(lhs[1:], ws[1:]):
        y = y + _dot_rows(a[...], w[...], chunks)
    out = x_ref[...] + g_ref[...] * y
    if final:
        fw_ref, o_ref = rest
        out = _rms(out) * fw_ref[...]
    else:
        (o_ref,) = rest
    o_ref[...] = out


def _proj_res(x, mod, layer, part, ctx, lhs, ws, tm, final_w=None, name="proj_res", chunks=1):
    bn, s, _ = x.shape
    row = lambda w: pl.BlockSpec((None, tm, w), lambda b, i: (b, i, 0))
    in_specs = [row(D_MODEL), _mod_spec(layer, part, ctx)]
    in_specs += [row(a.shape[-1]) for a in lhs] + [_layer_spec(shape, idx) for _, shape, idx in ws]
    args = [x, mod] + list(lhs) + [w for w, _, _ in ws]
    if final_w is not None:
        in_specs.append(_const_spec((1, D_MODEL)))
        args.append(final_w)
    return pl.pallas_call(
        functools.partial(_proj_res_body, n_lhs=len(lhs), final=final_w is not None, chunks=chunks),
        grid=(bn, s // tm),
        in_specs=in_specs,
        out_specs=row(D_MODEL),
        out_shape=jax.ShapeDtypeStruct(x.shape, F32),
        compiler_params=_cparams(("parallel", "parallel")),
        name=name + ("_ctx" if ctx else ""),
    )(*args)


def _up_body(*refs, mode, tm, seg, dot_chunks):
    x_ref, xp_ref, xn_ref, sh_ref, sc_ref, nw_ref = refs[:6]
    i = pl.program_id(1)
    j = pl.program_id(2)
    if mode == "ffn":
        wg_ref, wu_ref, cg_ref, cu_ref, o_ref, h_scr = refs[6:]
    else:
        wb_ref, wc_ref, wv_ref, cw_ref, o_ref, h_scr = refs[6:]
    whole = seg >= tm

    @pl.when(j == 0)
    def _():
        gain = nw_ref[...] * (1.0 + sc_ref[...])
        sh = sh_ref[...]
        for r in range(0, tm, NORM_ROWS):
            h_scr[HALO + r:HALO + r + NORM_ROWS, :] = (
                _rms(x_ref[r:r + NORM_ROWS, :]) * gain + sh).astype(BF16)
        first = (i * tm) % seg == 0 if whole else True
        last = ((i + 1) * tm) % seg == 0 if whole else True
        hp = _rms(xp_ref[...]) * gain + sh
        h_scr[0:HALO, :] = jnp.where(first, 0.0, hp).astype(BF16)
        hn = _rms(xn_ref[...]) * gain + sh
        h_scr[HALO + tm:2 * HALO + tm, :] = jnp.where(last, 0.0, hn).astype(BF16)

    def conv(full, cw_ref):
        rows = full.shape[0]
        prev = pltpu.roll(full, 1, 0)[HALO:HALO + tm, :]
        nxt = pltpu.roll(full, rows - 1, 0)[HALO:HALO + tm, :]
        if not whole:
            pos = lax.broadcasted_iota(jnp.int32, (tm, 1), 0) % seg
            prev = jnp.where(pos == 0, 0.0, prev)
            nxt = jnp.where(pos == seg - 1, 0.0, nxt)
        return cw_ref[0:1, :] * prev + cw_ref[1:2, :] * full[HALO:HALO + tm, :] + cw_ref[2:3, :] * nxt

    h = h_scr[...]
    def up(w_ref):
        return _dot_rows(h, w_ref[...], dot_chunks)

    if mode == "ffn":
        g = conv(up(wg_ref), cg_ref)
        u = conv(up(wu_ref), cu_ref)
        o_ref[...] = ((g / (1.0 + jnp.exp(-g))) * u).astype(o_ref.dtype)
    else:
        cv = up(wc_ref) * up(wv_ref)
        b = jnp.dot(h_scr[HALO:HALO + tm, :], wb_ref[...], preferred_element_type=F32)
        o_ref[...] = (b * conv(cv, cw_ref)).astype(o_ref.dtype)


def _up(x, mod, layer, ctx, nw, w, cw, wl, mode, tm, tn, dot_chunks=DOT_CHUNKS):
    seg = x.shape[1]
    if ctx:
        x = x.reshape(1, -1, D_MODEL)
    bn, s, _ = x.shape
    parts = 2 if mode == "ffn" else 3
    width = w.shape[2] // parts
    nj = width // tn
    hb = tm // HALO
    last_hb = s // HALO - 1
    in_specs = [
        pl.BlockSpec((None, tm, D_MODEL), lambda b, i, j: (b, i, 0)),
        pl.BlockSpec((None, HALO, D_MODEL), lambda b, i, j: (b, jnp.maximum(i * hb - 1, 0), 0)),
        pl.BlockSpec((None, HALO, D_MODEL), lambda b, i, j: (b, jnp.minimum((i + 1) * hb, last_hb), 0)),
        _mod_spec(layer, 3 if mode == "ffn" else 0, ctx),
        _mod_spec(layer, 4 if mode == "ffn" else 1, ctx),
        _const_spec((1, D_MODEL)),
    ]
    args = [x, x, x, mod, mod, nw]
    wspec = lambda p: pl.BlockSpec((None, D_MODEL, tn), lambda b, i, j: (wl, 0, p * nj + j))
    cspec = lambda p: pl.BlockSpec((None, 3, tn), lambda b, i, j: (wl, 0, p * nj + j))
    rows = tm + 2 * HALO
    scratch = [pltpu.VMEM((rows, D_MODEL), BF16)]
    if mode == "ffn":
        in_specs += [wspec(0), wspec(1), cspec(0), cspec(1)]
        args += [w, w, cw, cw]
    else:
        in_specs += [wspec(0), wspec(1), wspec(2), cspec(0)]
        args += [w, w, w, cw]
    out = pl.pallas_call(
        functools.partial(_up_body, mode=mode, tm=tm, seg=seg, dot_chunks=dot_chunks),
        grid=(bn, s // tm, nj),
        in_specs=in_specs,
        out_specs=pl.BlockSpec((None, tm, tn), lambda b, i, j: (b, i, j)),
        out_shape=jax.ShapeDtypeStruct((bn, s, width), BF16),
        scratch_shapes=scratch,
        compiler_params=_cparams(("parallel", "parallel", "arbitrary")),
        name=mode + "_up" + ("_ctx" if ctx else ""),
    )(*args)
    return out.reshape(-1, seg, width)


def _rope_tables(seq, rot_dim):
    rows = seq // GRID_W
    r = jnp.repeat(jnp.arange(rows, dtype=F32), GRID_W)
    col = jnp.tile(jnp.arange(GRID_W, dtype=F32), rows)
    quarter = rot_dim // 4
    inv = ROPE_THETA ** (-jnp.arange(quarter, dtype=F32) / quarter)
    ar = r[:, None] * inv
    ac = col[:, None] * inv
    ang = jnp.concatenate([ar, ar, ac, ac], axis=-1)
    reps = 128 // rot_dim
    cos = jnp.tile(jnp.cos(ang), (1, reps))
    sin = jnp.tile(jnp.sin(ang), (1, reps))
    first = ((jnp.arange(128) // quarter) % 2 == 0)[None, :]
    return cos, jnp.where(first, -sin, 0.0), jnp.where(first, 0.0, sin)


def _attn_weights(w_in, qn, kn, mqn, mkvn, w_uq, w_ukv):
    w_in = jnp.pad(w_in, ((0, 0), (0, W_IN_PAD - w_in.shape[1]))).astype(BF16)
    uq = w_uq.reshape(B_Q_RANK, B_HEADS, B_NOPE_DIM + B_ROPE_DIM)
    uq_rope = jnp.pad(uq[:, :, B_NOPE_DIM:], ((0, 0), (0, 0), (0, 128 - B_ROPE_DIM)))
    w_uq = jnp.concatenate([uq[:, :, :B_NOPE_DIM].reshape(B_Q_RANK, -1),
                            uq_rope.reshape(B_Q_RANK, -1)], axis=1).astype(BF16)
    ukv = w_ukv.reshape(B_KV_RANK, B_HEADS, B_NOPE_DIM + B_V_DIM)
    w_ukv = jnp.concatenate([ukv[:, :, :B_NOPE_DIM].reshape(B_KV_RANK, -1),
                             ukv[:, :, B_NOPE_DIM:].reshape(B_KV_RANK, -1)], axis=1).astype(BF16)
    return (w_in, qn.reshape(1, -1), kn.reshape(1, -1), mqn.reshape(1, -1), mkvn.reshape(1, -1),
            w_uq, w_ukv)


def kernel(x, c, ctx, c_ctx, w_ada, b_ada, norm_mix, norm_ffn, attn_w_in, attn_q_norm, attn_k_norm,
           mla_q_norm, mla_kv_norm, mla_w_uq, mla_w_ukv, attn_w_o, sc_w_in, sc_conv, sc_w_out,
           ffn_w_up, ffn_conv, ffn_w_down, final_norm):
    bn, s, _ = x.shape
    cl = ctx.shape[1]
    assert bn <= MOD_ROWS // 2
    cvec = jnp.zeros((MOD_ROWS, D_MODEL), F32).at[:bn].set(c).at[MOD_ROWS // 2].set(c_ctx)
    mod = _adaln(cvec, w_ada, b_ada)
    rope = _rope_tables(s, A_HEAD_DIM) + _rope_tables(s, B_ROPE_DIM)

    w_o_all = attn_w_o.astype(BF16)
    sc_w_in_b, sc_w_out_b = sc_w_in.astype(BF16), sc_w_out.astype(BF16)
    ffn_w_up_b, ffn_w_down_b = ffn_w_up.astype(BF16), ffn_w_down.astype(BF16)
    half = A_HEADS * A_HEAD_DIM
    xc = ctx
    for l in range(DEPTH):
        later_attn = any(j % 2 == 0 for j in range(l + 1, DEPTH))
        i = l // 2
        nw = norm_mix[l].reshape(1, -1)
        if l % 2 == 0:
            wts = _attn_weights(attn_w_in[i], attn_q_norm[i], attn_k_norm[i], mla_q_norm[i],
                                mla_kv_norm[i], mla_w_uq[i], mla_w_ukv[i])
            w_o = [(w_o_all, (half, D_MODEL), (i, 0, 0)), (w_o_all, (half, D_MODEL), (i, 1, 0))]
            qa_c, qb_c, ka_c, va_c, kb_c, vb_c = _attn_proj(xc, mod, l, True, nw, wts, None, later_attn, cl)
            qa, qb, ka, va, kb, vb = _attn_proj(x, mod, l, False, nw, wts, rope, True, 512, chunks=(1, 2)[i])
            oa = _flash(qa, ka_c, va_c, ka, va, group=A_GROUP, dq=A_HEAD_DIM, dv=A_HEAD_DIM,
                        tq=512, tk=512, name="gqa", depth=6)
            ob = _flash(qb, kb_c, vb_c, kb, vb, group=1, dq=B_QK_PAD, dv=B_V_DIM,
                        tq=2048, tk=512, name="mla", depth=6)
            x = _proj_res(x, mod, l, 2, False, [oa, ob], w_o, 512, name="attn_out", chunks=(1, 2)[i])
            if later_attn:
                oa_c = _flash(qa_c, ka_c, va_c, None, None, group=A_GROUP, dq=A_HEAD_DIM, dv=A_HEAD_DIM,
                              tq=cl, tk=0, name="gqa_ctx")
                ob_c = _flash(qb_c, kb_c, vb_c, None, None, group=1, dq=B_QK_PAD, dv=B_V_DIM,
                              tq=cl, tk=0, name="mla_ctx")
                xc = _proj_res(xc, mod, l, 2, True, [oa_c, ob_c], w_o, cl, name="attn_out")
        else:
            w_out = [(sc_w_out_b, (SC_WIDTH, D_MODEL), (i, 0, 0))]
            z = _up(x, mod, l, False, nw, sc_w_in_b, sc_conv, i, "sc", 1024, 512)
            x = _proj_res(x, mod, l, 2, False, [z], w_out, 512, name="sc_out", chunks=(2, 4)[i])
            if later_attn:
                zc = _up(xc, mod, l, True, nw, sc_w_in_b, sc_conv, i, "sc", 1024, 512)
                xc = _proj_res(xc, mod, l, 2, True, [zc], w_out, cl, name="sc_out")
        nwf = norm_ffn[l].reshape(1, -1)
        w_down = [(ffn_w_down_b, (D_FF, D_MODEL), (l, 0, 0))]
        hid = _up(x, mod, l, False, nwf, ffn_w_up_b, ffn_conv, l, "ffn", 1024, 512)
        fw = final_norm.reshape(1, -1) if l == DEPTH - 1 else None
        x = _proj_res(x, mod, l, 5, False, [hid], w_down, 512, final_w=fw, name="ffn_down")
        if later_attn:
            hid_c = _up(xc, mod, l, True, nwf, ffn_w_up_b, ffn_conv, l, "ffn", 1024, 512)
            xc = _proj_res(xc, mod, l, 5, True, [hid_c], w_down, cl, name="ffn_down")
    return x
```

```python
import functools

import jax
import jax.numpy as jnp
from jax import lax
from jax.experimental import pallas as pl
from jax.experimental.pallas import tpu as pltpu

F32 = jnp.float32
BF16 = jnp.bfloat16

D_MODEL = 2048
DEPTH = 4
GRID_W = 64
ROPE_THETA = 10000.0
EPS = 1e-6
A_HEADS = 8
A_KV_HEADS = 2
A_GROUP = A_HEADS // A_KV_HEADS
A_HEAD_DIM = 128
B_HEADS = 8
B_Q_RANK = 512
B_KV_RANK = 256
B_NOPE_DIM = 128
B_ROPE_DIM = 64
B_V_DIM = 128
B_QK_PAD = 256
LOG2E = 1.4426950408889634
A_SCALE = A_HEAD_DIM ** -0.5 * LOG2E
B_SCALE = (B_NOPE_DIM + B_ROPE_DIM) ** -0.5 * LOG2E
QGROUP = 256
V_ROWS = 128 + 16
SC_WIDTH = D_MODEL
D_FF = 256 * ((8 * D_MODEL // 3 + 255) // 256)
W_IN_PAD = 2432
MOD_ROWS = 16
HALO = 16
DOT_CHUNKS = 3
NORM_ROWS = 32
VMEM_LIMIT = 56 * 1024 * 1024


def _cparams(sem):
    return pltpu.CompilerParams(dimension_semantics=sem, vmem_limit_bytes=VMEM_LIMIT)


def _const_spec(shape):
    nd = len(shape)
    return pl.BlockSpec(shape, lambda *g: (0,) * nd, pipeline_mode=pl.Buffered(1))


def _layer_spec(shape, idx):
    return pl.BlockSpec((None,) + tuple(shape), lambda *g: idx, pipeline_mode=pl.Buffered(1))


def _mod_spec(layer, part, ctx):
    if ctx:
        idx = lambda b, *g: (layer, MOD_ROWS // 2, part, 0, 0)
    else:
        idx = lambda b, *g: (layer, b, part, 0, 0)
    return pl.BlockSpec((None, None, None, 1, D_MODEL), idx)


def _rms(x):
    return x * lax.rsqrt(jnp.mean(x * x, axis=-1, keepdims=True) + EPS)


def _dot_rows(a, w, chunks):
    n = a.shape[0] // chunks
    return jnp.concatenate([jnp.dot(a[k * n:(k + 1) * n], w, preferred_element_type=F32)
                            for k in range(chunks)], axis=0)


def _rope(x, cos, sin_up, sin_dn, quarter):
    n = x.shape[-1]
    return x * cos + pltpu.roll(x, n - quarter, 1) * sin_up + pltpu.roll(x, quarter, 1) * sin_dn


def _adaln_body(c_ref, w_ref, b_ref, o_ref):
    c = c_ref[...]
    s = (c / (1.0 + jnp.exp(-c))).astype(BF16)
    o_ref[...] = jnp.dot(s, w_ref[...].astype(BF16), preferred_element_type=F32) + b_ref[...]


def _adaln(cvec, w_ada, b_ada):
    tn = 1024
    n = 6 * D_MODEL
    out = pl.pallas_call(
        _adaln_body,
        grid=(DEPTH, n // tn),
        in_specs=[
            pl.BlockSpec((MOD_ROWS, D_MODEL), lambda l, j: (0, 0)),
            pl.BlockSpec((None, D_MODEL, tn), lambda l, j: (l, 0, j)),
            pl.BlockSpec((None, 1, tn), lambda l, j: (l, 0, j)),
        ],
        out_specs=pl.BlockSpec((None, MOD_ROWS, tn), lambda l, j: (l, 0, j)),
        out_shape=jax.ShapeDtypeStruct((DEPTH, MOD_ROWS, n), F32),
        compiler_params=_cparams(("parallel", "parallel")),
        name="adaln",
    )(cvec, w_ada, b_ada.reshape(DEPTH, 1, n))
    return out.reshape(DEPTH, MOD_ROWS, 6, 1, D_MODEL)


def _attn_proj_body(*refs, use_rope, with_q, chunks):
    it = iter(refs)
    x_ref, sh_ref, sc_ref, nw_ref, win_ref = (next(it) for _ in range(5))
    qn_ref, kn_ref, mqn_ref, mkvn_ref, wuq_ref, wukv_ref = (next(it) for _ in range(6))
    if use_rope:
        ca, sau, sad, cb, sbu, sbd = (next(it)[...] for _ in range(6))
    if with_q:
        qa_ref, qb_ref = next(it), next(it)
    ka_ref, va_ref, kb_ref, vb_ref, h_scr = (next(it) for _ in range(5))

    def rope_a(v):
        return _rope(v, ca, sau, sad, A_HEAD_DIM // 4) if use_rope else v

    def rope_b(v):
        return _rope(v, cb, sbu, sbd, B_ROPE_DIM // 4) if use_rope else v

    gain = nw_ref[...] * (1.0 + sc_ref[...])
    sh = sh_ref[...]
    tm = x_ref.shape[0]
    for r in range(0, tm, NORM_ROWS):
        h_scr[r:r + NORM_ROWS, :] = (_rms(x_ref[r:r + NORM_ROWS, :]) * gain + sh).astype(BF16)
    proj = _dot_rows(h_scr[...], win_ref[...], chunks)
    q_end = A_HEADS * A_HEAD_DIM
    k_end = q_end + A_KV_HEADS * A_HEAD_DIM
    v_end = k_end + A_KV_HEADS * A_HEAD_DIM
    cq_end = v_end + B_Q_RANK
    ckv_end = cq_end + B_KV_RANK
    nope_w = B_HEADS * B_NOPE_DIM

    ckv = (_rms(proj[:, cq_end:ckv_end]) * mkvn_ref[...]).astype(BF16)
    kv = _dot_rows(ckv, wukv_ref[...], chunks)
    if with_q:
        cq = (_rms(proj[:, v_end:cq_end]) * mqn_ref[...]).astype(BF16)
        qb = _dot_rows(cq, wuq_ref[...], chunks)

    def put_vt(ref, hh, v):
        lo = hh * V_ROWS
        ref[lo:lo + v.shape[1], :] = v.T.astype(BF16)
        ref[lo + v.shape[1]:lo + V_ROWS, :] = jnp.ones((V_ROWS - v.shape[1], v.shape[0]), BF16)

    kn = kn_ref[...]
    for kh in range(A_KV_HEADS):
        lo = q_end + kh * A_HEAD_DIM
        k = rope_a(_rms(proj[:, lo:lo + A_HEAD_DIM]) * kn)
        ka_ref[:, kh * A_HEAD_DIM:(kh + 1) * A_HEAD_DIM] = k.astype(BF16)
        lo = k_end + kh * A_HEAD_DIM
        put_vt(va_ref, kh, proj[:, lo:lo + A_HEAD_DIM])
    if with_q:
        qn = qn_ref[...]
        for hh in range(A_HEADS):
            lo = hh * A_HEAD_DIM
            q = rope_a(_rms(proj[:, lo:lo + A_HEAD_DIM]) * qn) * A_SCALE
            qa_ref[:, lo:lo + A_HEAD_DIM] = q.astype(BF16)

    kr = rope_b(proj[:, ckv_end:W_IN_PAD]).astype(BF16)
    for hh in range(B_HEADS):
        put_vt(vb_ref, hh, kv[:, nope_w + hh * B_V_DIM:nope_w + (hh + 1) * B_V_DIM])
        lo = hh * B_QK_PAD
        kb_ref[:, lo:lo + B_NOPE_DIM] = kv[:, hh * B_NOPE_DIM:(hh + 1) * B_NOPE_DIM].astype(BF16)
        kb_ref[:, lo + B_NOPE_DIM:lo + B_QK_PAD] = kr
    if with_q:
        for hh in range(B_HEADS):
            lo = hh * B_QK_PAD
            qb_ref[:, lo:lo + B_NOPE_DIM] = (qb[:, hh * B_NOPE_DIM:(hh + 1) * B_NOPE_DIM] * B_SCALE).astype(BF16)
            r = rope_b(qb[:, nope_w + hh * 128:nope_w + (hh + 1) * 128]) * B_SCALE
            qb_ref[:, lo + B_NOPE_DIM:lo + B_QK_PAD] = r.astype(BF16)


def _attn_proj(x, mod, layer, ctx, nw, wts, rope, with_q, tm, chunks=1):
    bn, s, _ = x.shape
    use_rope = rope is not None
    w_in, qn, kn, mqn, mkvn, w_uq, w_ukv = wts
    row = lambda w: pl.BlockSpec((None, tm, w), lambda b, i: (b, i, 0))
    in_specs = [row(D_MODEL), _mod_spec(layer, 0, ctx), _mod_spec(layer, 1, ctx),
                _const_spec((1, D_MODEL)), _const_spec(w_in.shape),
                _const_spec(qn.shape), _const_spec(kn.shape), _const_spec(mqn.shape),
                _const_spec(mkvn.shape), _const_spec(w_uq.shape), _const_spec(w_ukv.shape)]
    args = [x, mod, mod, nw, w_in, qn, kn, mqn, mkvn, w_uq, w_ukv]
    if use_rope:
        in_specs += [pl.BlockSpec((tm, 128), lambda b, i: (i, 0))] * 6
        args += list(rope)
    col = lambda w: pl.BlockSpec((None, w, tm), lambda b, i: (b, 0, i))
    outs_desc = ([(A_HEADS * A_HEAD_DIM, False), (B_HEADS * B_QK_PAD, False)] if with_q else []) + [
        (A_KV_HEADS * A_HEAD_DIM, False), (A_KV_HEADS * V_ROWS, True),
        (B_HEADS * B_QK_PAD, False), (B_HEADS * V_ROWS, True)]
    outs = pl.pallas_call(
        functools.partial(_attn_proj_body, use_rope=use_rope, with_q=with_q, chunks=chunks),
        grid=(bn, s // tm),
        in_specs=in_specs,
        out_specs=[col(w) if t else row(w) for w, t in outs_desc],
        out_shape=[jax.ShapeDtypeStruct((bn, w, s) if t else (bn, s, w), BF16) for w, t in outs_desc],
        scratch_shapes=[pltpu.VMEM((tm, D_MODEL), BF16)],
        compiler_params=_cparams(("parallel", "parallel")),
        name="attn_proj_ctx" if ctx else "attn_proj",
    )(*args)
    return outs if with_q else [None, None] + list(outs)


def _col_reduce(x, op, final):
    while x.shape[0] > 8 and x.shape[0] % 16 == 0:
        h = x.shape[0] // 2
        x = op(x[:h], x[h:])
    return final(x, axis=0, keepdims=True)


def _flash_body(*refs, group, dq, dv, tk, n_lat, depth):
    if n_lat:
        q_ref, kc_ref, vct_ref, k_ref, vt_ref, o_ref = refs
    else:
        q_ref, kc_ref, vct_ref, o_ref = refs
    tq = q_ref.shape[0]
    rows = [slice(r, r + QGROUP) for r in range(0, tq, QGROUP)]
    q_parts = [(r, slice(g * dq, (g + 1) * dq)) for r in rows for g in range(group)]
    o_parts = [(r, slice(g * dv, (g + 1) * dv)) for r in rows for g in range(group)]
    n = len(q_parts)
    chunks = [(kc_ref, vct_ref, 0, kc_ref.shape[0])] + [(k_ref, vt_ref, c * tk, tk) for c in range(n_lat)]
    m, acc = [None] * n, [None] * n
    blocks = [(ci, g) for ci in range(len(chunks)) for g in range(n)]

    def scores(ci, g):
        kr, _, off, size = chunks[ci]
        return lax.dot_general(kr[off:off + size, :], q_ref[q_parts[g]], (((1,), (1,)), ((), ())),
                               preferred_element_type=F32)

    s_next = [scores(*blocks[d]) for d in range(min(depth, len(blocks)))]
    for idx, (ci, g) in enumerate(blocks):
        s = s_next.pop(0)
        if idx + depth < len(blocks):
            s_next.append(scores(*blocks[idx + depth]))
        _, vr, off, size = chunks[ci]
        vt = vr[:, off:off + size]
        smax = _col_reduce(s, jnp.maximum, jnp.max)
        if ci == 0:
            m_new = smax
            p = jnp.exp2(s - m_new)
            acc[g] = jnp.dot(vt, p.astype(BF16), preferred_element_type=F32)
        else:
            m_new = jnp.maximum(m[g], smax)
            alpha = jnp.exp2(m[g] - m_new)
            p = jnp.exp2(s - m_new)
            acc[g] = alpha * acc[g] + jnp.dot(vt, p.astype(BF16), preferred_element_type=F32)
        m[g] = m_new
    for g in range(n):
        o_ref[o_parts[g]] = (acc[g][:dv] / acc[g][dv:dv + 1]).T.astype(o_ref.dtype)


def _flash(q, kc, vct, k, vt, *, group, dq, dv, tq, tk, name, depth=4):
    bn, s, qw = q.shape
    kvh = qw // (group * dq)
    c = kc.shape[1]
    n_lat = 0 if k is None else k.shape[1] // tk
    in_specs = [pl.BlockSpec((None, tq, group * dq), lambda b, h, i: (b, i, h)),
                pl.BlockSpec((None, c, dq), lambda b, h, i: (b, 0, h)),
                pl.BlockSpec((None, V_ROWS, c), lambda b, h, i: (b, h, 0))]
    args = [q, kc, vct]
    if n_lat:
        t = k.shape[1]
        in_specs += [pl.BlockSpec((None, t, dq), lambda b, h, i: (b, 0, h)),
                     pl.BlockSpec((None, V_ROWS, t), lambda b, h, i: (b, h, 0))]
        args += [k, vt]
    return pl.pallas_call(
        functools.partial(_flash_body, group=group, dq=dq, dv=dv, tk=tk, n_lat=n_lat, depth=depth),
        grid=(bn, kvh, s // tq),
        in_specs=in_specs,
        out_specs=pl.BlockSpec((None, tq, group * dv), lambda b, h, i: (b, i, h)),
        out_shape=jax.ShapeDtypeStruct((bn, s, kvh * group * dv), BF16),
        compiler_params=_cparams(("parallel", "parallel", "parallel")),
        name=name,
    )(*args)


def _proj_res_body(*refs, n_lhs, final):
    x_ref, g_ref = refs[0], refs[1]
    lhs = refs[2:2 + n_lhs]
    ws = refs[2 + n_lhs:2 + 2 * n_lhs]
    rest = refs[2 + 2 * n_lhs:]
    y = jnp.dot(lhs[0][...], ws[0][...], preferred_element_type=F32)
    for a, w in zip(lhs[1:], ws[1:]):
        y = y + jnp.dot(a[...], w[...], preferred_element_type=F32)
    out = x_ref[...] + g_ref[...] * y
    if final:
        fw_ref, o_ref = rest
        out = _rms(out) * fw_ref[...]
    else:
        (o_ref,) = rest
    o_ref[...] = out


def _proj_res(x, mod, layer, part, ctx, lhs, ws, tm, final_w=None, name="proj_res"):
    bn, s, _ = x.shape
    row = lambda w: pl.BlockSpec((None, tm, w), lambda b, i: (b, i, 0))
    in_specs = [row(D_MODEL), _mod_spec(layer, part, ctx)]
    in_specs += [row(a.shape[-1]) for a in lhs] + [_layer_spec(shape, idx) for _, shape, idx in ws]
    args = [x, mod] + list(lhs) + [w for w, _, _ in ws]
    if final_w is not None:
        in_specs.append(_const_spec((1, D_MODEL)))
        args.append(final_w)
    return pl.pallas_call(
        functools.partial(_proj_res_body, n_lhs=len(lhs), final=final_w is not None),
        grid=(bn, s // tm),
        in_specs=in_specs,
        out_specs=row(D_MODEL),
        out_shape=jax.ShapeDtypeStruct(x.shape, F32),
        compiler_params=_cparams(("parallel", "parallel")),
        name=name + ("_ctx" if ctx else ""),
    )(*args)


def _up_body(*refs, mode, tm, seg):
    x_ref, xp_ref, xn_ref, sh_ref, sc_ref, nw_ref = refs[:6]
    i = pl.program_id(1)
    j = pl.program_id(2)
    if mode == "ffn":
        wg_ref, wu_ref, cg_ref, cu_ref, o_ref, h_scr = refs[6:]
    else:
        wb_ref, wc_ref, wv_ref, cw_ref, o_ref, h_scr = refs[6:]
    whole = seg >= tm

    @pl.when(j == 0)
    def _():
        gain = nw_ref[...] * (1.0 + sc_ref[...])
        sh = sh_ref[...]
        for r in range(0, tm, NORM_ROWS):
            h_scr[HALO + r:HALO + r + NORM_ROWS, :] = (
                _rms(x_ref[r:r + NORM_ROWS, :]) * gain + sh).astype(BF16)
        first = (i * tm) % seg == 0 if whole else True
        last = ((i + 1) * tm) % seg == 0 if whole else True
        hp = _rms(xp_ref[...]) * gain + sh
        h_scr[0:HALO, :] = jnp.where(first, 0.0, hp).astype(BF16)
        hn = _rms(xn_ref[...]) * gain + sh
        h_scr[HALO + tm:2 * HALO + tm, :] = jnp.where(last, 0.0, hn).astype(BF16)

    def conv(full, cw_ref):
        rows = full.shape[0]
        prev = pltpu.roll(full, 1, 0)[HALO:HALO + tm, :]
        nxt = pltpu.roll(full, rows - 1, 0)[HALO:HALO + tm, :]
        if not whole:
            pos = lax.broadcasted_iota(jnp.int32, (tm, 1), 0) % seg
            prev = jnp.where(pos == 0, 0.0, prev)
            nxt = jnp.where(pos == seg - 1, 0.0, nxt)
        return cw_ref[0:1, :] * prev + cw_ref[1:2, :] * full[HALO:HALO + tm, :] + cw_ref[2:3, :] * nxt

    h = h_scr[...]
    def up(w_ref):
        return _dot_rows(h, w_ref[...], DOT_CHUNKS)

    if mode == "ffn":
        g = conv(up(wg_ref), cg_ref)
        u = conv(up(wu_ref), cu_ref)
        o_ref[...] = ((g / (1.0 + jnp.exp(-g))) * u).astype(o_ref.dtype)
    else:
        cv = up(wc_ref) * up(wv_ref)
        b = jnp.dot(h_scr[HALO:HALO + tm, :], wb_ref[...], preferred_element_type=F32)
        o_ref[...] = (b * conv(cv, cw_ref)).astype(o_ref.dtype)


def _up(x, mod, layer, ctx, nw, w, cw, wl, mode, tm, tn):
    seg = x.shape[1]
    if ctx:
        x = x.reshape(1, -1, D_MODEL)
    bn, s, _ = x.shape
    parts = 2 if mode == "ffn" else 3
    width = w.shape[2] // parts
    nj = width // tn
    hb = tm // HALO
    last_hb = s // HALO - 1
    in_specs = [
        pl.BlockSpec((None, tm, D_MODEL), lambda b, i, j: (b, i, 0)),
        pl.BlockSpec((None, HALO, D_MODEL), lambda b, i, j: (b, jnp.maximum(i * hb - 1, 0), 0)),
        pl.BlockSpec((None, HALO, D_MODEL), lambda b, i, j: (b, jnp.minimum((i + 1) * hb, last_hb), 0)),
        _mod_spec(layer, 3 if mode == "ffn" else 0, ctx),
        _mod_spec(layer, 4 if mode == "ffn" else 1, ctx),
        _const_spec((1, D_MODEL)),
    ]
    args = [x, x, x, mod, mod, nw]
    wspec = lambda p: pl.BlockSpec((None, D_MODEL, tn), lambda b, i, j: (wl, 0, p * nj + j))
    cspec = lambda p: pl.BlockSpec((None, 3, tn), lambda b, i, j: (wl, 0, p * nj + j))
    rows = tm + 2 * HALO
    scratch = [pltpu.VMEM((rows, D_MODEL), BF16)]
    if mode == "ffn":
        in_specs += [wspec(0), wspec(1), cspec(0), cspec(1)]
        args += [w, w, cw, cw]
    else:
        in_specs += [wspec(0), wspec(1), wspec(2), cspec(0)]
        args += [w, w, w, cw]
    out = pl.pallas_call(
        functools.partial(_up_body, mode=mode, tm=tm, seg=seg),
        grid=(bn, s // tm, nj),
        in_specs=in_specs,
        out_specs=pl.BlockSpec((None, tm, tn), lambda b, i, j: (b, i, j)),
        out_shape=jax.ShapeDtypeStruct((bn, s, width), BF16),
        scratch_shapes=scratch,
        compiler_params=_cparams(("parallel", "parallel", "arbitrary")),
        name=mode + "_up" + ("_ctx" if ctx else ""),
    )(*args)
    return out.reshape(-1, seg, width)


def _rope_tables(seq, rot_dim):
    rows = seq // GRID_W
    r = jnp.repeat(jnp.arange(rows, dtype=F32), GRID_W)
    col = jnp.tile(jnp.arange(GRID_W, dtype=F32), rows)
    quarter = rot_dim // 4
    inv = ROPE_THETA ** (-jnp.arange(quarter, dtype=F32) / quarter)
    ar = r[:, None] * inv
    ac = col[:, None] * inv
    ang = jnp.concatenate([ar, ar, ac, ac], axis=-1)
    reps = 128 // rot_dim
    cos = jnp.tile(jnp.cos(ang), (1, reps))
    sin = jnp.tile(jnp.sin(ang), (1, reps))
    first = ((jnp.arange(128) // quarter) % 2 == 0)[None, :]
    return cos, jnp.where(first, -sin, 0.0), jnp.where(first, 0.0, sin)


def _attn_weights(w_in, qn, kn, mqn, mkvn, w_uq, w_ukv):
    w_in = jnp.pad(w_in, ((0, 0), (0, W_IN_PAD - w_in.shape[1]))).astype(BF16)
    uq = w_uq.reshape(B_Q_RANK, B_HEADS, B_NOPE_DIM + B_ROPE_DIM)
    uq_rope = jnp.pad(uq[:, :, B_NOPE_DIM:], ((0, 0), (0, 0), (0, 128 - B_ROPE_DIM)))
    w_uq = jnp.concatenate([uq[:, :, :B_NOPE_DIM].reshape(B_Q_RANK, -1),
                            uq_rope.reshape(B_Q_RANK, -1)], axis=1).astype(BF16)
    ukv = w_ukv.reshape(B_KV_RANK, B_HEADS, B_NOPE_DIM + B_V_DIM)
    w_ukv = jnp.concatenate([ukv[:, :, :B_NOPE_DIM].reshape(B_KV_RANK, -1),
                             ukv[:, :, B_NOPE_DIM:].reshape(B_KV_RANK, -1)], axis=1).astype(BF16)
    return (w_in, qn.reshape(1, -1), kn.reshape(1, -1), mqn.reshape(1, -1), mkvn.reshape(1, -1),
            w_uq, w_ukv)


def kernel(x, c, ctx, c_ctx, w_ada, b_ada, norm_mix, norm_ffn, attn_w_in, attn_q_norm, attn_k_norm,
           mla_q_norm, mla_kv_norm, mla_w_uq, mla_w_ukv, attn_w_o, sc_w_in, sc_conv, sc_w_out,
           ffn_w_up, ffn_conv, ffn_w_down, final_norm):
    bn, s, _ = x.shape
    cl = ctx.shape[1]
    assert bn <= MOD_ROWS // 2
    cvec = jnp.zeros((MOD_ROWS, D_MODEL), F32).at[:bn].set(c).at[MOD_ROWS // 2].set(c_ctx)
    mod = _adaln(cvec, w_ada, b_ada)
    rope = _rope_tables(s, A_HEAD_DIM) + _rope_tables(s, B_ROPE_DIM)

    w_o_all = attn_w_o.astype(BF16)
    sc_w_in_b, sc_w_out_b = sc_w_in.astype(BF16), sc_w_out.astype(BF16)
    ffn_w_up_b, ffn_w_down_b = ffn_w_up.astype(BF16), ffn_w_down.astype(BF16)
    half = A_HEADS * A_HEAD_DIM
    xc = ctx
    for l in range(DEPTH):
        later_attn = any(j % 2 == 0 for j in range(l + 1, DEPTH))
        i = l // 2
        nw = norm_mix[l].reshape(1, -1)
        if l % 2 == 0:
            wts = _attn_weights(attn_w_in[i], attn_q_norm[i], attn_k_norm[i], mla_q_norm[i],
                                mla_kv_norm[i], mla_w_uq[i], mla_w_ukv[i])
            w_o = [(w_o_all, (half, D_MODEL), (i, 0, 0)), (w_o_all, (half, D_MODEL), (i, 1, 0))]
            qa_c, qb_c, ka_c, va_c, kb_c, vb_c = _attn_proj(xc, mod, l, True, nw, wts, None, later_attn, cl)
            qa, qb, ka, va, kb, vb = _attn_proj(x, mod, l, False, nw, wts, rope, True, 512, chunks=2)
            oa = _flash(qa, ka_c, va_c, ka, va, group=A_GROUP, dq=A_HEAD_DIM, dv=A_HEAD_DIM,
                        tq=512, tk=512, name="gqa", depth=6)
            ob = _flash(qb, kb_c, vb_c, kb, vb, group=1, dq=B_QK_PAD, dv=B_V_DIM,
                        tq=2048, tk=512, name="mla", depth=6)
            x = _proj_res(x, mod, l, 2, False, [oa, ob], w_o, 512, name="attn_out")
            if later_attn:
                oa_c = _flash(qa_c, ka_c, va_c, None, None, group=A_GROUP, dq=A_HEAD_DIM, dv=A_HEAD_DIM,
                              tq=cl, tk=0, name="gqa_ctx")
                ob_c = _flash(qb_c, kb_c, vb_c, None, None, group=1, dq=B_QK_PAD, dv=B_V_DIM,
                              tq=cl, tk=0, name="mla_ctx")
                xc = _proj_res(xc, mod, l, 2, True, [oa_c, ob_c], w_o, cl, name="attn_out")
        else:
            w_out = [(sc_w_out_b, (SC_WIDTH, D_MODEL), (i, 0, 0))]
            z = _up(x, mod, l, False, nw, sc_w_in_b, sc_conv, i, "sc", 1024, 512)
            x = _proj_res(x, mod, l, 2, False, [z], w_out, 512, name="sc_out")
            if later_attn:
                zc = _up(xc, mod, l, True, nw, sc_w_in_b, sc_conv, i, "sc", 1024, 512)
                xc = _proj_res(xc, mod, l, 2, True, [zc], w_out, cl, name="sc_out")
        nwf = norm_ffn[l].reshape(1, -1)
        w_down = [(ffn_w_down_b, (D_FF, D_MODEL), (l, 0, 0))]
        hid = _up(x, mod, l, False, nwf, ffn_w_up_b, ffn_conv, l, "ffn", 1024, 512)
        fw = final_norm.reshape(1, -1) if l == DEPTH - 1 else None
        x = _proj_res(x, mod, l, 5, False, [hid], w_down, 512, final_w=fw, name="ffn_down")
        if later_attn:
            hid_c = _up(xc, mod, l, True, nwf, ffn_w_up_b, ffn_conv, l, "ffn", 1024, 512)
            xc = _proj_res(xc, mod, l, 5, True, [hid_c], w_down, cl, name="ffn_down")
    return x
```

```python
import functools

import jax
import jax.numpy as jnp
from jax import lax
from jax.experimental import pallas as pl
from jax.experimental.pallas import tpu as pltpu

F32 = jnp.float32
BF16 = jnp.bfloat16

D_MODEL = 2048
DEPTH = 4
GRID_W = 64
ROPE_THETA = 10000.0
EPS = 1e-6
A_HEADS = 8
A_KV_HEADS = 2
A_GROUP = A_HEADS // A_KV_HEADS
A_HEAD_DIM = 128
B_HEADS = 8
B_Q_RANK = 512
B_KV_RANK = 256
B_NOPE_DIM = 128
B_ROPE_DIM = 64
B_V_DIM = 128
B_QK_PAD = 256
LOG2E = 1.4426950408889634
A_SCALE = A_HEAD_DIM ** -0.5 * LOG2E
B_SCALE = (B_NOPE_DIM + B_ROPE_DIM) ** -0.5 * LOG2E
QGROUP = 256
V_ROWS = 128 + 16
SC_WIDTH = D_MODEL
D_FF = 256 * ((8 * D_MODEL // 3 + 255) // 256)
W_IN_PAD = 2432
MOD_ROWS = 16
HALO = 16
DOT_CHUNKS = 3
NORM_ROWS = 32
VMEM_LIMIT = 56 * 1024 * 1024


def _cparams(sem):
    return pltpu.CompilerParams(dimension_semantics=sem, vmem_limit_bytes=VMEM_LIMIT)


def _const_spec(shape):
    nd = len(shape)
    return pl.BlockSpec(shape, lambda *g: (0,) * nd, pipeline_mode=pl.Buffered(1))


def _layer_spec(shape, idx):
    return pl.BlockSpec((None,) + tuple(shape), lambda *g: idx, pipeline_mode=pl.Buffered(1))


def _mod_spec(layer, part, ctx):
    if ctx:
        idx = lambda b, *g: (layer, MOD_ROWS // 2, part, 0, 0)
    else:
        idx = lambda b, *g: (layer, b, part, 0, 0)
    return pl.BlockSpec((None, None, None, 1, D_MODEL), idx)


def _rms(x):
    return x * lax.rsqrt(jnp.mean(x * x, axis=-1, keepdims=True) + EPS)


def _dot_rows(a, w, chunks):
    n = a.shape[0] // chunks
    return jnp.concatenate([jnp.dot(a[k * n:(k + 1) * n], w, preferred_element_type=F32)
                            for k in range(chunks)], axis=0)


def _rope(x, cos, sin_up, sin_dn, quarter):
    n = x.shape[-1]
    return x * cos + pltpu.roll(x, n - quarter, 1) * sin_up + pltpu.roll(x, quarter, 1) * sin_dn


def _adaln_body(c_ref, w_ref, b_ref, o_ref):
    c = c_ref[...]
    s = (c / (1.0 + jnp.exp(-c))).astype(BF16)
    o_ref[...] = jnp.dot(s, w_ref[...].astype(BF16), preferred_element_type=F32) + b_ref[...]


def _adaln(cvec, w_ada, b_ada):
    tn = 1024
    n = 6 * D_MODEL
    out = pl.pallas_call(
        _adaln_body,
        grid=(DEPTH, n // tn),
        in_specs=[
            pl.BlockSpec((MOD_ROWS, D_MODEL), lambda l, j: (0, 0)),
            pl.BlockSpec((None, D_MODEL, tn), lambda l, j: (l, 0, j)),
            pl.BlockSpec((None, 1, tn), lambda l, j: (l, 0, j)),
        ],
        out_specs=pl.BlockSpec((None, MOD_ROWS, tn), lambda l, j: (l, 0, j)),
        out_shape=jax.ShapeDtypeStruct((DEPTH, MOD_ROWS, n), F32),
        compiler_params=_cparams(("parallel", "parallel")),
        name="adaln",
    )(cvec, w_ada, b_ada.reshape(DEPTH, 1, n))
    return out.reshape(DEPTH, MOD_ROWS, 6, 1, D_MODEL)


def _attn_proj_body(*refs, use_rope, with_q, chunks):
    it = iter(refs)
    x_ref, sh_ref, sc_ref, nw_ref, win_ref = (next(it) for _ in range(5))
    qn_ref, kn_ref, mqn_ref, mkvn_ref, wuq_ref, wukv_ref = (next(it) for _ in range(6))
    if use_rope:
        ca, sau, sad, cb, sbu, sbd = (next(it)[...] for _ in range(6))
    if with_q:
        qa_ref, qb_ref = next(it), next(it)
    ka_ref, va_ref, kb_ref, vb_ref, h_scr = (next(it) for _ in range(5))

    def rope_a(v):
        return _rope(v, ca, sau, sad, A_HEAD_DIM // 4) if use_rope else v

    def rope_b(v):
        return _rope(v, cb, sbu, sbd, B_ROPE_DIM // 4) if use_rope else v

    gain = nw_ref[...] * (1.0 + sc_ref[...])
    sh = sh_ref[...]
    tm = x_ref.shape[0]
    for r in range(0, tm, NORM_ROWS):
        h_scr[r:r + NORM_ROWS, :] = (_rms(x_ref[r:r + NORM_ROWS, :]) * gain + sh).astype(BF16)
    proj = _dot_rows(h_scr[...], win_ref[...], chunks)
    q_end = A_HEADS * A_HEAD_DIM
    k_end = q_end + A_KV_HEADS * A_HEAD_DIM
    v_end = k_end + A_KV_HEADS * A_HEAD_DIM
    cq_end = v_end + B_Q_RANK
    ckv_end = cq_end + B_KV_RANK
    nope_w = B_HEADS * B_NOPE_DIM

    ckv = (_rms(proj[:, cq_end:ckv_end]) * mkvn_ref[...]).astype(BF16)
    kv = _dot_rows(ckv, wukv_ref[...], chunks)
    if with_q:
        cq = (_rms(proj[:, v_end:cq_end]) * mqn_ref[...]).astype(BF16)
        qb = _dot_rows(cq, wuq_ref[...], chunks)

    def put_vt(ref, hh, v):
        lo = hh * V_ROWS
        ref[lo:lo + v.shape[1], :] = v.T.astype(BF16)
        ref[lo + v.shape[1]:lo + V_ROWS, :] = jnp.ones((V_ROWS - v.shape[1], v.shape[0]), BF16)

    kn = kn_ref[...]
    for kh in range(A_KV_HEADS):
        lo = q_end + kh * A_HEAD_DIM
        k = rope_a(_rms(proj[:, lo:lo + A_HEAD_DIM]) * kn)
        ka_ref[:, kh * A_HEAD_DIM:(kh + 1) * A_HEAD_DIM] = k.astype(BF16)
        lo = k_end + kh * A_HEAD_DIM
        put_vt(va_ref, kh, proj[:, lo:lo + A_HEAD_DIM])
    if with_q:
        qn = qn_ref[...]
        for hh in range(A_HEADS):
            lo = hh * A_HEAD_DIM
            q = rope_a(_rms(proj[:, lo:lo + A_HEAD_DIM]) * qn) * A_SCALE
            qa_ref[:, lo:lo + A_HEAD_DIM] = q.astype(BF16)

    kr = rope_b(proj[:, ckv_end:W_IN_PAD]).astype(BF16)
    for hh in range(B_HEADS):
        put_vt(vb_ref, hh, kv[:, nope_w + hh * B_V_DIM:nope_w + (hh + 1) * B_V_DIM])
        lo = hh * B_QK_PAD
        kb_ref[:, lo:lo + B_NOPE_DIM] = kv[:, hh * B_NOPE_DIM:(hh + 1) * B_NOPE_DIM].astype(BF16)
        kb_ref[:, lo + B_NOPE_DIM:lo + B_QK_PAD] = kr
    if with_q:
        for hh in range(B_HEADS):
            lo = hh * B_QK_PAD
            qb_ref[:, lo:lo + B_NOPE_DIM] = (qb[:, hh * B_NOPE_DIM:(hh + 1) * B_NOPE_DIM] * B_SCALE).astype(BF16)
            r = rope_b(qb[:, nope_w + hh * 128:nope_w + (hh + 1) * 128]) * B_SCALE
            qb_ref[:, lo + B_NOPE_DIM:lo + B_QK_PAD] = r.astype(BF16)


def _attn_proj(x, mod, layer, ctx, nw, wts, rope, with_q, tm, chunks=1):
    bn, s, _ = x.shape
    use_rope = rope is not None
    w_in, qn, kn, mqn, mkvn, w_uq, w_ukv = wts
    row = lambda w: pl.BlockSpec((None, tm, w), lambda b, i: (b, i, 0))
    in_specs = [row(D_MODEL), _mod_spec(layer, 0, ctx), _mod_spec(layer, 1, ctx),
                _const_spec((1, D_MODEL)), _const_spec(w_in.shape),
                _const_spec(qn.shape), _const_spec(kn.shape), _const_spec(mqn.shape),
                _const_spec(mkvn.shape), _const_spec(w_uq.shape), _const_spec(w_ukv.shape)]
    args = [x, mod, mod, nw, w_in, qn, kn, mqn, mkvn, w_uq, w_ukv]
    if use_rope:
        in_specs += [pl.BlockSpec((tm, 128), lambda b, i: (i, 0))] * 6
        args += list(rope)
    col = lambda w: pl.BlockSpec((None, w, tm), lambda b, i: (b, 0, i))
    outs_desc = ([(A_HEADS * A_HEAD_DIM, False), (B_HEADS * B_QK_PAD, False)] if with_q else []) + [
        (A_KV_HEADS * A_HEAD_DIM, False), (A_KV_HEADS * V_ROWS, True),
        (B_HEADS * B_QK_PAD, False), (B_HEADS * V_ROWS, True)]
    outs = pl.pallas_call(
        functools.partial(_attn_proj_body, use_rope=use_rope, with_q=with_q, chunks=chunks),
        grid=(bn, s // tm),
        in_specs=in_specs,
        out_specs=[col(w) if t else row(w) for w, t in outs_desc],
        out_shape=[jax.ShapeDtypeStruct((bn, w, s) if t else (bn, s, w), BF16) for w, t in outs_desc],
        scratch_shapes=[pltpu.VMEM((tm, D_MODEL), BF16)],
        compiler_params=_cparams(("parallel", "parallel")),
        name="attn_proj_ctx" if ctx else "attn_proj",
    )(*args)
    return outs if with_q else [None, None] + list(outs)


def _col_reduce(x, op, final):
    while x.shape[0] > 8 and x.shape[0] % 16 == 0:
        h = x.shape[0] // 2
        x = op(x[:h], x[h:])
    return final(x, axis=0, keepdims=True)


def _flash_body(*refs, group, dq, dv, tk, n_lat, depth):
    if n_lat:
        q_ref, kc_ref, vct_ref, k_ref, vt_ref, o_ref = refs
    else:
        q_ref, kc_ref, vct_ref, o_ref = refs
    tq = q_ref.shape[0]
    rows = [slice(r, r + QGROUP) for r in range(0, tq, QGROUP)]
    q_parts = [(r, slice(g * dq, (g + 1) * dq)) for r in rows for g in range(group)]
    o_parts = [(r, slice(g * dv, (g + 1) * dv)) for r in rows for g in range(group)]
    n = len(q_parts)
    chunks = [(kc_ref, vct_ref, 0, kc_ref.shape[0])] + [(k_ref, vt_ref, c * tk, tk) for c in range(n_lat)]
    m, acc = [None] * n, [None] * n
    blocks = [(ci, g) for ci in range(len(chunks)) for g in range(n)]

    def scores(ci, g):
        kr, _, off, size = chunks[ci]
        return lax.dot_general(kr[off:off + size, :], q_ref[q_parts[g]], (((1,), (1,)), ((), ())),
                               preferred_element_type=F32)

    s_next = [scores(*blocks[d]) for d in range(min(depth, len(blocks)))]
    for idx, (ci, g) in enumerate(blocks):
        s = s_next.pop(0)
        if idx + depth < len(blocks):
            s_next.append(scores(*blocks[idx + depth]))
        _, vr, off, size = chunks[ci]
        vt = vr[:, off:off + size]
        smax = _col_reduce(s, jnp.maximum, jnp.max)
        if ci == 0:
            m_new = smax
            p = jnp.exp2(s - m_new)
            acc[g] = jnp.dot(vt, p.astype(BF16), preferred_element_type=F32)
        else:
            m_new = jnp.maximum(m[g], smax)
            alpha = jnp.exp2(m[g] - m_new)
            p = jnp.exp2(s - m_new)
            acc[g] = alpha * acc[g] + jnp.dot(vt, p.astype(BF16), preferred_element_type=F32)
        m[g] = m_new
    for g in range(n):
        o_ref[o_parts[g]] = (acc[g][:dv] / acc[g][dv:dv + 1]).T.astype(o_ref.dtype)


def _flash(q, kc, vct, k, vt, *, group, dq, dv, tq, tk, name, depth=4):
    bn, s, qw = q.shape
    kvh = qw // (group * dq)
    c = kc.shape[1]
    n_lat = 0 if k is None else k.shape[1] // tk
    in_specs = [pl.BlockSpec((None, tq, group * dq), lambda b, h, i: (b, i, h)),
                pl.BlockSpec((None, c, dq), lambda b, h, i: (b, 0, h)),
                pl.BlockSpec((None, V_ROWS, c), lambda b, h, i: (b, h, 0))]
    args = [q, kc, vct]
    if n_lat:
        t = k.shape[1]
        in_specs += [pl.BlockSpec((None, t, dq), lambda b, h, i: (b, 0, h)),
                     pl.BlockSpec((None, V_ROWS, t), lambda b, h, i: (b, h, 0))]
        args += [k, vt]
    return pl.pallas_call(
        functools.partial(_flash_body, group=group, dq=dq, dv=dv, tk=tk, n_lat=n_lat, depth=depth),
        grid=(bn, kvh, s // tq),
        in_specs=in_specs,
        out_specs=pl.BlockSpec((None, tq, group * dv), lambda b, h, i: (b, i, h)),
        out_shape=jax.ShapeDtypeStruct((bn, s, kvh * group * dv), BF16),
        compiler_params=_cparams(("parallel", "parallel", "parallel")),
        name=name,
    )(*args)


def _proj_res_body(*refs, n_lhs, final):
    x_ref, g_ref = refs[0], refs[1]
    lhs = refs[2:2 + n_lhs]
    ws = refs[2 + n_lhs:2 + 2 * n_lhs]
    rest = refs[2 + 2 * n_lhs:]
    y = jnp.dot(lhs[0][...], ws[0][...], preferred_element_type=F32)
    for a, w in zip(lhs[1:], ws[1:]):
        y = y + jnp.dot(a[...], w[...], preferred_element_type=F32)
    out = x_ref[...] + g_ref[...] * y
    if final:
        fw_ref, o_ref = rest
        out = _rms(out) * fw_ref[...]
    else:
        (o_ref,) = rest
    o_ref[...] = out


def _proj_res(x, mod, layer, part, ctx, lhs, ws, tm, final_w=None, name="proj_res"):
    bn, s, _ = x.shape
    row = lambda w: pl.BlockSpec((None, tm, w), lambda b, i: (b, i, 0))
    in_specs = [row(D_MODEL), _mod_spec(layer, part, ctx)]
    in_specs += [row(a.shape[-1]) for a in lhs] + [_layer_spec(shape, idx) for _, shape, idx in ws]
    args = [x, mod] + list(lhs) + [w for w, _, _ in ws]
    if final_w is not None:
        in_specs.append(_const_spec((1, D_MODEL)))
        args.append(final_w)
    return pl.pallas_call(
        functools.partial(_proj_res_body, n_lhs=len(lhs), final=final_w is not None),
        grid=(bn, s // tm),
        in_specs=in_specs,
        out_specs=row(D_MODEL),
        out_shape=jax.ShapeDtypeStruct(x.shape, F32),
        compiler_params=_cparams(("parallel", "parallel")),
        name=name + ("_ctx" if ctx else ""),
    )(*args)


def _up_body(*refs, mode, tm, seg):
    x_ref, xp_ref, xn_ref, sh_ref, sc_ref, nw_ref = refs[:6]
    i = pl.program_id(1)
    j = pl.program_id(2)
    if mode == "ffn":
        wg_ref, wu_ref, cg_ref, cu_ref, o_ref, h_scr = refs[6:]
    else:
        wb_ref, wc_ref, wv_ref, cw_ref, o_ref, h_scr = refs[6:]
    whole = seg >= tm

    @pl.when(j == 0)
    def _():
        gain = nw_ref[...] * (1.0 + sc_ref[...])
        sh = sh_ref[...]
        for r in range(0, tm, NORM_ROWS):
            h_scr[HALO + r:HALO + r + NORM_ROWS, :] = (
                _rms(x_ref[r:r + NORM_ROWS, :]) * gain + sh).astype(BF16)
        first = (i * tm) % seg == 0 if whole else True
        last = ((i + 1) * tm) % seg == 0 if whole else True
        hp = _rms(xp_ref[...]) * gain + sh
        h_scr[0:HALO, :] = jnp.where(first, 0.0, hp).astype(BF16)
        hn = _rms(xn_ref[...]) * gain + sh
        h_scr[HALO + tm:2 * HALO + tm, :] = jnp.where(last, 0.0, hn).astype(BF16)

    def conv(full, cw_ref):
        rows = full.shape[0]
        prev = pltpu.roll(full, 1, 0)[HALO:HALO + tm, :]
        nxt = pltpu.roll(full, rows - 1, 0)[HALO:HALO + tm, :]
        if not whole:
            pos = lax.broadcasted_iota(jnp.int32, (tm, 1), 0) % seg
            prev = jnp.where(pos == 0, 0.0, prev)
            nxt = jnp.where(pos == seg - 1, 0.0, nxt)
        return cw_ref[0:1, :] * prev + cw_ref[1:2, :] * full[HALO:HALO + tm, :] + cw_ref[2:3, :] * nxt

    h = h_scr[...]
    def up(w_ref):
        return _dot_rows(h, w_ref[...], DOT_CHUNKS)

    if mode == "ffn":
        g = conv(up(wg_ref), cg_ref)
        u = conv(up(wu_ref), cu_ref)
        o_ref[...] = ((g / (1.0 + jnp.exp(-g))) * u).astype(o_ref.dtype)
    else:
        cv = up(wc_ref) * up(wv_ref)
        b = jnp.dot(h_scr[HALO:HALO + tm, :], wb_ref[...], preferred_element_type=F32)
        o_ref[...] = (b * conv(cv, cw_ref)).astype(o_ref.dtype)


def _up(x, mod, layer, ctx, nw, w, cw, wl, mode, tm, tn):
    seg = x.shape[1]
    if ctx:
        x = x.reshape(1, -1, D_MODEL)
    bn, s, _ = x.shape
    parts = 2 if mode == "ffn" else 3
    width = w.shape[2] // parts
    nj = width // tn
    hb = tm // HALO
    last_hb = s // HALO - 1
    in_specs = [
        pl.BlockSpec((None, tm, D_MODEL), lambda b, i, j: (b, i, 0)),
        pl.BlockSpec((None, HALO, D_MODEL), lambda b, i, j: (b, jnp.maximum(i * hb - 1, 0), 0)),
        pl.BlockSpec((None, HALO, D_MODEL), lambda b, i, j: (b, jnp.minimum((i + 1) * hb, last_hb), 0)),
        _mod_spec(layer, 3 if mode == "ffn" else 0, ctx),
        _mod_spec(layer, 4 if mode == "ffn" else 1, ctx),
        _const_spec((1, D_MODEL)),
    ]
    args = [x, x, x, mod, mod, nw]
    wspec = lambda p: pl.BlockSpec((None, D_MODEL, tn), lambda b, i, j: (wl, 0, p * nj + j))
    cspec = lambda p: pl.BlockSpec((None, 3, tn), lambda b, i, j: (wl, 0, p * nj + j))
    rows = tm + 2 * HALO
    scratch = [pltpu.VMEM((rows, D_MODEL), BF16)]
    if mode == "ffn":
        in_specs += [wspec(0), wspec(1), cspec(0), cspec(1)]
        args += [w, w, cw, cw]
    else:
        in_specs += [wspec(0), wspec(1), wspec(2), cspec(0)]
        args += [w, w, w, cw]
    out = pl.pallas_call(
        functools.partial(_up_body, mode=mode, tm=tm, seg=seg),
        grid=(bn, s // tm, nj),
        in_specs=in_specs,
        out_specs=pl.BlockSpec((None, tm, tn), lambda b, i, j: (b, i, j)),
        out_shape=jax.ShapeDtypeStruct((bn, s, width), BF16),
        scratch_shapes=scratch,
        compiler_params=_cparams(("parallel", "parallel", "arbitrary")),
        name=mode + "_up" + ("_ctx" if ctx else ""),
    )(*args)
    return out.reshape(-1, seg, width)


def _rope_tables(seq, rot_dim):
    rows = seq // GRID_W
    r = jnp.repeat(jnp.arange(rows, dtype=F32), GRID_W)
    col = jnp.tile(jnp.arange(GRID_W, dtype=F32), rows)
    quarter = rot_dim // 4
    inv = ROPE_THETA ** (-jnp.arange(quarter, dtype=F32) / quarter)
    ar = r[:, None] * inv
    ac = col[:, None] * inv
    ang = jnp.concatenate([ar, ar, ac, ac], axis=-1)
    reps = 128 // rot_dim
    cos = jnp.tile(jnp.cos(ang), (1, reps))
    sin = jnp.tile(jnp.sin(ang), (1, reps))
    first = ((jnp.arange(128) // quarter) % 2 == 0)[None, :]
    return cos, jnp.where(first, -sin, 0.0), jnp.where(first, 0.0, sin)


def _attn_weights(w_in, qn, kn, mqn, mkvn, w_uq, w_ukv):
    w_in = jnp.pad(w_in, ((0, 0), (0, W_IN_PAD - w_in.shape[1]))).astype(BF16)
    uq = w_uq.reshape(B_Q_RANK, B_HEADS, B_NOPE_DIM + B_ROPE_DIM)
    uq_rope = jnp.pad(uq[:, :, B_NOPE_DIM:], ((0, 0), (0, 0), (0, 128 - B_ROPE_DIM)))
    w_uq = jnp.concatenate([uq[:, :, :B_NOPE_DIM].reshape(B_Q_RANK, -1),
                            uq_rope.reshape(B_Q_RANK, -1)], axis=1).astype(BF16)
    ukv = w_ukv.reshape(B_KV_RANK, B_HEADS, B_NOPE_DIM + B_V_DIM)
    w_ukv = jnp.concatenate([ukv[:, :, :B_NOPE_DIM].reshape(B_KV_RANK, -1),
                             ukv[:, :, B_NOPE_DIM:].reshape(B_KV_RANK, -1)], axis=1).astype(BF16)
    return (w_in, qn.reshape(1, -1), kn.reshape(1, -1), mqn.reshape(1, -1), mkvn.reshape(1, -1),
            w_uq, w_ukv)


def kernel(x, c, ctx, c_ctx, w_ada, b_ada, norm_mix, norm_ffn, attn_w_in, attn_q_norm, attn_k_norm,
           mla_q_norm, mla_kv_norm, mla_w_uq, mla_w_ukv, attn_w_o, sc_w_in, sc_conv, sc_w_out,
           ffn_w_up, ffn_conv, ffn_w_down, final_norm):
    bn, s, _ = x.shape
    cl = ctx.shape[1]
    assert bn <= MOD_ROWS // 2
    cvec = jnp.zeros((MOD_ROWS, D_MODEL), F32).at[:bn].set(c).at[MOD_ROWS // 2].set(c_ctx)
    mod = _adaln(cvec, w_ada, b_ada)
    rope = _rope_tables(s, A_HEAD_DIM) + _rope_tables(s, B_ROPE_DIM)

    w_o_all = attn_w_o.astype(BF16)
    sc_w_in_b, sc_w_out_b = sc_w_in.astype(BF16), sc_w_out.astype(BF16)
    ffn_w_up_b, ffn_w_down_b = ffn_w_up.astype(BF16), ffn_w_down.astype(BF16)
    half = A_HEADS * A_HEAD_DIM
    xc = ctx
    for l in range(DEPTH):
        later_attn = any(j % 2 == 0 for j in range(l + 1, DEPTH))
        i = l // 2
        nw = norm_mix[l].reshape(1, -1)
        if l % 2 == 0:
            wts = _attn_weights(attn_w_in[i], attn_q_norm[i], attn_k_norm[i], mla_q_norm[i],
                                mla_kv_norm[i], mla_w_uq[i], mla_w_ukv[i])
            w_o = [(w_o_all, (half, D_MODEL), (i, 0, 0)), (w_o_all, (half, D_MODEL), (i, 1, 0))]
            qa_c, qb_c, ka_c, va_c, kb_c, vb_c = _attn_proj(xc, mod, l, True, nw, wts, None, later_attn, cl)
            qa, qb, ka, va, kb, vb = _attn_proj(x, mod, l, False, nw, wts, rope, True, 512, chunks=2)
            oa = _flash(qa, ka_c, va_c, ka, va, group=A_GROUP, dq=A_HEAD_DIM, dv=A_HEAD_DIM,
                        tq=512, tk=512, name="gqa", depth=6)
            ob = _flash(qb, kb_c, vb_c, kb, vb, group=1, dq=B_QK_PAD, dv=B_V_DIM,
                        tq=2048, tk=256, name="mla", depth=10)
            x = _proj_res(x, mod, l, 2, False, [oa, ob], w_o, 512, name="attn_out")
            if later_attn:
                oa_c = _flash(qa_c, ka_c, va_c, None, None, group=A_GROUP, dq=A_HEAD_DIM, dv=A_HEAD_DIM,
                              tq=cl, tk=0, name="gqa_ctx")
                ob_c = _flash(qb_c, kb_c, vb_c, None, None, group=1, dq=B_QK_PAD, dv=B_V_DIM,
                              tq=cl, tk=0, name="mla_ctx")
                xc = _proj_res(xc, mod, l, 2, True, [oa_c, ob_c], w_o, cl, name="attn_out")
        else:
            w_out = [(sc_w_out_b, (SC_WIDTH, D_MODEL), (i, 0, 0))]
            z = _up(x, mod, l, False, nw, sc_w_in_b, sc_conv, i, "sc", 1024, 512)
            x = _proj_res(x, mod, l, 2, False, [z], w_out, 512, name="sc_out")
            if later_attn:
                zc = _up(xc, mod, l, True, nw, sc_w_in_b, sc_conv, i, "sc", 1024, 512)
                xc = _proj_res(xc, mod, l, 2, True, [zc], w_out, cl, name="sc_out")
        nwf = norm_ffn[l].reshape(1, -1)
        w_down = [(ffn_w_down_b, (D_FF, D_MODEL), (l, 0, 0))]
        hid = _up(x, mod, l, False, nwf, ffn_w_up_b, ffn_conv, l, "ffn", 1024, 512)
        fw = final_norm.reshape(1, -1) if l == DEPTH - 1 else None
        x = _proj_res(x, mod, l, 5, False, [hid], w_down, 512, final_w=fw, name="ffn_down")
        if later_attn:
            hid_c = _up(xc, mod, l, True, nwf, ffn_w_up_b, ffn_conv, l, "ffn", 1024, 512)
            xc = _proj_res(xc, mod, l, 5, True, [hid_c], w_down, cl, name="ffn_down")
    return x
```

```python
import functools

import jax
import jax.numpy as jnp
from jax import lax
from jax.experimental import pallas as pl
from jax.experimental.pallas import tpu as pltpu

F32 = jnp.float32
BF16 = jnp.bfloat16

D_MODEL = 2048
DEPTH = 4
GRID_W = 64
ROPE_THETA = 10000.0
EPS = 1e-6
A_HEADS = 8
A_KV_HEADS = 2
A_GROUP = A_HEADS // A_KV_HEADS
A_HEAD_DIM = 128
B_HEADS = 8
B_Q_RANK = 512
B_KV_RANK = 256
B_NOPE_DIM = 128
B_ROPE_DIM = 64
B_V_DIM = 128
B_QK_PAD = 256
LOG2E = 1.4426950408889634
A_SCALE = A_HEAD_DIM ** -0.5 * LOG2E
B_SCALE = (B_NOPE_DIM + B_ROPE_DIM) ** -0.5 * LOG2E
QGROUP = 256
V_ROWS = 128 + 16
SC_WIDTH = D_MODEL
D_FF = 256 * ((8 * D_MODEL // 3 + 255) // 256)
W_IN_PAD = 2432
MOD_ROWS = 16
HALO = 16
DOT_CHUNKS = 3
NORM_ROWS = 32
VMEM_LIMIT = 56 * 1024 * 1024


def _cparams(sem):
    return pltpu.CompilerParams(dimension_semantics=sem, vmem_limit_bytes=VMEM_LIMIT)


def _const_spec(shape):
    nd = len(shape)
    return pl.BlockSpec(shape, lambda *g: (0,) * nd, pipeline_mode=pl.Buffered(1))


def _layer_spec(shape, idx):
    return pl.BlockSpec((None,) + tuple(shape), lambda *g: idx, pipeline_mode=pl.Buffered(1))


def _mod_spec(layer, part, ctx):
    if ctx:
        idx = lambda b, *g: (layer, MOD_ROWS // 2, part, 0, 0)
    else:
        idx = lambda b, *g: (layer, b, part, 0, 0)
    return pl.BlockSpec((None, None, None, 1, D_MODEL), idx)


def _rms(x):
    return x * lax.rsqrt(jnp.mean(x * x, axis=-1, keepdims=True) + EPS)


def _dot_rows(a, w, chunks):
    n = a.shape[0] // chunks
    return jnp.concatenate([jnp.dot(a[k * n:(k + 1) * n], w, preferred_element_type=F32)
                            for k in range(chunks)], axis=0)


def _rope(x, cos, sin_up, sin_dn, quarter):
    n = x.shape[-1]
    return x * cos + pltpu.roll(x, n - quarter, 1) * sin_up + pltpu.roll(x, quarter, 1) * sin_dn


def _adaln_body(c_ref, w_ref, b_ref, o_ref):
    c = c_ref[...]
    s = (c / (1.0 + jnp.exp(-c))).astype(BF16)
    o_ref[...] = jnp.dot(s, w_ref[...].astype(BF16), preferred_element_type=F32) + b_ref[...]


def _adaln(cvec, w_ada, b_ada):
    tn = 1024
    n = 6 * D_MODEL
    out = pl.pallas_call(
        _adaln_body,
        grid=(DEPTH, n // tn),
        in_specs=[
            pl.BlockSpec((MOD_ROWS, D_MODEL), lambda l, j: (0, 0)),
            pl.BlockSpec((None, D_MODEL, tn), lambda l, j: (l, 0, j)),
            pl.BlockSpec((None, 1, tn), lambda l, j: (l, 0, j)),
        ],
        out_specs=pl.BlockSpec((None, MOD_ROWS, tn), lambda l, j: (l, 0, j)),
        out_shape=jax.ShapeDtypeStruct((DEPTH, MOD_ROWS, n), F32),
        compiler_params=_cparams(("parallel", "parallel")),
        name="adaln",
    )(cvec, w_ada, b_ada.reshape(DEPTH, 1, n))
    return out.reshape(DEPTH, MOD_ROWS, 6, 1, D_MODEL)


def _attn_proj_body(*refs, use_rope, with_q, chunks):
    it = iter(refs)
    x_ref, sh_ref, sc_ref, nw_ref, win_ref = (next(it) for _ in range(5))
    qn_ref, kn_ref, mqn_ref, mkvn_ref, wuq_ref, wukv_ref = (next(it) for _ in range(6))
    if use_rope:
        ca, sau, sad, cb, sbu, sbd = (next(it)[...] for _ in range(6))
    if with_q:
        qa_ref, qb_ref = next(it), next(it)
    ka_ref, va_ref, kb_ref, vb_ref, h_scr = (next(it) for _ in range(5))

    def rope_a(v):
        return _rope(v, ca, sau, sad, A_HEAD_DIM // 4) if use_rope else v

    def rope_b(v):
        return _rope(v, cb, sbu, sbd, B_ROPE_DIM // 4) if use_rope else v

    gain = nw_ref[...] * (1.0 + sc_ref[...])
    sh = sh_ref[...]
    tm = x_ref.shape[0]
    for r in range(0, tm, NORM_ROWS):
        h_scr[r:r + NORM_ROWS, :] = (_rms(x_ref[r:r + NORM_ROWS, :]) * gain + sh).astype(BF16)
    proj = _dot_rows(h_scr[...], win_ref[...], chunks)
    q_end = A_HEADS * A_HEAD_DIM
    k_end = q_end + A_KV_HEADS * A_HEAD_DIM
    v_end = k_end + A_KV_HEADS * A_HEAD_DIM
    cq_end = v_end + B_Q_RANK
    ckv_end = cq_end + B_KV_RANK
    nope_w = B_HEADS * B_NOPE_DIM

    ckv = (_rms(proj[:, cq_end:ckv_end]) * mkvn_ref[...]).astype(BF16)
    kv = _dot_rows(ckv, wukv_ref[...], chunks)
    if with_q:
        cq = (_rms(proj[:, v_end:cq_end]) * mqn_ref[...]).astype(BF16)
        qb = _dot_rows(cq, wuq_ref[...], chunks)

    def put_vt(ref, hh, v):
        lo = hh * V_ROWS
        ref[lo:lo + v.shape[1], :] = v.T.astype(BF16)
        ref[lo + v.shape[1]:lo + V_ROWS, :] = jnp.ones((V_ROWS - v.shape[1], v.shape[0]), BF16)

    kn = kn_ref[...]
    for kh in range(A_KV_HEADS):
        lo = q_end + kh * A_HEAD_DIM
        k = rope_a(_rms(proj[:, lo:lo + A_HEAD_DIM]) * kn)
        ka_ref[:, kh * A_HEAD_DIM:(kh + 1) * A_HEAD_DIM] = k.astype(BF16)
        lo = k_end + kh * A_HEAD_DIM
        put_vt(va_ref, kh, proj[:, lo:lo + A_HEAD_DIM])
    if with_q:
        qn = qn_ref[...]
        for hh in range(A_HEADS):
            lo = hh * A_HEAD_DIM
            q = rope_a(_rms(proj[:, lo:lo + A_HEAD_DIM]) * qn) * A_SCALE
            qa_ref[:, lo:lo + A_HEAD_DIM] = q.astype(BF16)

    kr = rope_b(proj[:, ckv_end:W_IN_PAD]).astype(BF16)
    for hh in range(B_HEADS):
        put_vt(vb_ref, hh, kv[:, nope_w + hh * B_V_DIM:nope_w + (hh + 1) * B_V_DIM])
        lo = hh * B_QK_PAD
        kb_ref[:, lo:lo + B_NOPE_DIM] = kv[:, hh * B_NOPE_DIM:(hh + 1) * B_NOPE_DIM].astype(BF16)
        kb_ref[:, lo + B_NOPE_DIM:lo + B_QK_PAD] = kr
    if with_q:
        for hh in range(B_HEADS):
            lo = hh * B_QK_PAD
            qb_ref[:, lo:lo + B_NOPE_DIM] = (qb[:, hh * B_NOPE_DIM:(hh + 1) * B_NOPE_DIM] * B_SCALE).astype(BF16)
            r = rope_b(qb[:, nope_w + hh * 128:nope_w + (hh + 1) * 128]) * B_SCALE
            qb_ref[:, lo + B_NOPE_DIM:lo + B_QK_PAD] = r.astype(BF16)


def _attn_proj(x, mod, layer, ctx, nw, wts, rope, with_q, tm, chunks=1):
    bn, s, _ = x.shape
    use_rope = rope is not None
    w_in, qn, kn, mqn, mkvn, w_uq, w_ukv = wts
    row = lambda w: pl.BlockSpec((None, tm, w), lambda b, i: (b, i, 0))
    in_specs = [row(D_MODEL), _mod_spec(layer, 0, ctx), _mod_spec(layer, 1, ctx),
                _const_spec((1, D_MODEL)), _const_spec(w_in.shape),
                _const_spec(qn.shape), _const_spec(kn.shape), _const_spec(mqn.shape),
                _const_spec(mkvn.shape), _const_spec(w_uq.shape), _const_spec(w_ukv.shape)]
    args = [x, mod, mod, nw, w_in, qn, kn, mqn, mkvn, w_uq, w_ukv]
    if use_rope:
        in_specs += [pl.BlockSpec((tm, 128), lambda b, i: (i, 0))] * 6
        args += list(rope)
    col = lambda w: pl.BlockSpec((None, w, tm), lambda b, i: (b, 0, i))
    outs_desc = ([(A_HEADS * A_HEAD_DIM, False), (B_HEADS * B_QK_PAD, False)] if with_q else []) + [
        (A_KV_HEADS * A_HEAD_DIM, False), (A_KV_HEADS * V_ROWS, True),
        (B_HEADS * B_QK_PAD, False), (B_HEADS * V_ROWS, True)]
    outs = pl.pallas_call(
        functools.partial(_attn_proj_body, use_rope=use_rope, with_q=with_q, chunks=chunks),
        grid=(bn, s // tm),
        in_specs=in_specs,
        out_specs=[col(w) if t else row(w) for w, t in outs_desc],
        out_shape=[jax.ShapeDtypeStruct((bn, w, s) if t else (bn, s, w), BF16) for w, t in outs_desc],
        scratch_shapes=[pltpu.VMEM((tm, D_MODEL), BF16)],
        compiler_params=_cparams(("parallel", "parallel")),
        name="attn_proj_ctx" if ctx else "attn_proj",
    )(*args)
    return outs if with_q else [None, None] + list(outs)


def _col_reduce(x, op, final):
    while x.shape[0] > 8 and x.shape[0] % 16 == 0:
        h = x.shape[0] // 2
        x = op(x[:h], x[h:])
    return final(x, axis=0, keepdims=True)


def _flash_body(*refs, group, dq, dv, tk, n_lat, depth):
    if n_lat:
        q_ref, kc_ref, vct_ref, k_ref, vt_ref, o_ref = refs
    else:
        q_ref, kc_ref, vct_ref, o_ref = refs
    tq = q_ref.shape[0]
    rows = [slice(r, r + QGROUP) for r in range(0, tq, QGROUP)]
    q_parts = [(r, slice(g * dq, (g + 1) * dq)) for r in rows for g in range(group)]
    o_parts = [(r, slice(g * dv, (g + 1) * dv)) for r in rows for g in range(group)]
    n = len(q_parts)
    chunks = [(kc_ref, vct_ref, 0, kc_ref.shape[0])] + [(k_ref, vt_ref, c * tk, tk) for c in range(n_lat)]
    m, acc = [None] * n, [None] * n
    blocks = [(ci, g) for ci in range(len(chunks)) for g in range(n)]

    def scores(ci, g):
        kr, _, off, size = chunks[ci]
        return lax.dot_general(kr[off:off + size, :], q_ref[q_parts[g]], (((1,), (1,)), ((), ())),
                               preferred_element_type=F32)

    s_next = [scores(*blocks[d]) for d in range(min(depth, len(blocks)))]
    for idx, (ci, g) in enumerate(blocks):
        s = s_next.pop(0)
        if idx + depth < len(blocks):
            s_next.append(scores(*blocks[idx + depth]))
        _, vr, off, size = chunks[ci]
        vt = vr[:, off:off + size]
        smax = _col_reduce(s, jnp.maximum, jnp.max)
        if ci == 0:
            m_new = smax
            p = jnp.exp2(s - m_new)
            acc[g] = jnp.dot(vt, p.astype(BF16), preferred_element_type=F32)
        else:
            m_new = jnp.maximum(m[g], smax)
            alpha = jnp.exp2(m[g] - m_new)
            p = jnp.exp2(s - m_new)
            acc[g] = alpha * acc[g] + jnp.dot(vt, p.astype(BF16), preferred_element_type=F32)
        m[g] = m_new
    for g in range(n):
        o_ref[o_parts[g]] = (acc[g][:dv] / acc[g][dv:dv + 1]).T.astype(o_ref.dtype)


def _flash(q, kc, vct, k, vt, *, group, dq, dv, tq, tk, name, depth=4):
    bn, s, qw = q.shape
    kvh = qw // (group * dq)
    c = kc.shape[1]
    n_lat = 0 if k is None else k.shape[1] // tk
    in_specs = [pl.BlockSpec((None, tq, group * dq), lambda b, h, i: (b, i, h)),
                pl.BlockSpec((None, c, dq), lambda b, h, i: (b, 0, h)),
                pl.BlockSpec((None, V_ROWS, c), lambda b, h, i: (b, h, 0))]
    args = [q, kc, vct]
    if n_lat:
        t = k.shape[1]
        in_specs += [pl.BlockSpec((None, t, dq), lambda b, h, i: (b, 0, h)),
                     pl.BlockSpec((None, V_ROWS, t), lambda b, h, i: (b, h, 0))]
        args += [k, vt]
    return pl.pallas_call(
        functools.partial(_flash_body, group=group, dq=dq, dv=dv, tk=tk, n_lat=n_lat, depth=depth),
        grid=(bn, kvh, s // tq),
        in_specs=in_specs,
        out_specs=pl.BlockSpec((None, tq, group * dv), lambda b, h, i: (b, i, h)),
        out_shape=jax.ShapeDtypeStruct((bn, s, kvh * group * dv), BF16),
        compiler_params=_cparams(("parallel", "parallel", "parallel")),
        name=name,
    )(*args)


def _proj_res_body(*refs, n_lhs, final):
    x_ref, g_ref = refs[0], refs[1]
    lhs = refs[2:2 + n_lhs]
    ws = refs[2 + n_lhs:2 + 2 * n_lhs]
    rest = refs[2 + 2 * n_lhs:]
    y = jnp.dot(lhs[0][...], ws[0][...], preferred_element_type=F32)
    for a, w in zip(lhs[1:], ws[1:]):
        y = y + jnp.dot(a[...], w[...], preferred_element_type=F32)
    out = x_ref[...] + g_ref[...] * y
    if final:
        fw_ref, o_ref = rest
        out = _rms(out) * fw_ref[...]
    else:
        (o_ref,) = rest
    o_ref[...] = out


def _proj_res(x, mod, layer, part, ctx, lhs, ws, tm, final_w=None, name="proj_res"):
    bn, s, _ = x.shape
    row = lambda w: pl.BlockSpec((None, tm, w), lambda b, i: (b, i, 0))
    in_specs = [row(D_MODEL), _mod_spec(layer, part, ctx)]
    in_specs += [row(a.shape[-1]) for a in lhs] + [_layer_spec(shape, idx) for _, shape, idx in ws]
    args = [x, mod] + list(lhs) + [w for w, _, _ in ws]
    if final_w is not None:
        in_specs.append(_const_spec((1, D_MODEL)))
        args.append(final_w)
    return pl.pallas_call(
        functools.partial(_proj_res_body, n_lhs=len(lhs), final=final_w is not None),
        grid=(bn, s // tm),
        in_specs=in_specs,
        out_specs=row(D_MODEL),
        out_shape=jax.ShapeDtypeStruct(x.shape, F32),
        compiler_params=_cparams(("parallel", "parallel")),
        name=name + ("_ctx" if ctx else ""),
    )(*args)


def _up_body(*refs, mode, tm, seg):
    x_ref, xp_ref, xn_ref, sh_ref, sc_ref, nw_ref = refs[:6]
    i = pl.program_id(1)
    j = pl.program_id(2)
    if mode == "ffn":
        wg_ref, wu_ref, cg_ref, cu_ref, o_ref, h_scr = refs[6:]
    else:
        wb_ref, wc_ref, wv_ref, cw_ref, o_ref, h_scr = refs[6:]
    whole = seg >= tm

    @pl.when(j == 0)
    def _():
        gain = nw_ref[...] * (1.0 + sc_ref[...])
        sh = sh_ref[...]
        for r in range(0, tm, NORM_ROWS):
            h_scr[HALO + r:HALO + r + NORM_ROWS, :] = (
                _rms(x_ref[r:r + NORM_ROWS, :]) * gain + sh).astype(BF16)
        first = (i * tm) % seg == 0 if whole else True
        last = ((i + 1) * tm) % seg == 0 if whole else True
        hp = _rms(xp_ref[...]) * gain + sh
        h_scr[0:HALO, :] = jnp.where(first, 0.0, hp).astype(BF16)
        hn = _rms(xn_ref[...]) * gain + sh
        h_scr[HALO + tm:2 * HALO + tm, :] = jnp.where(last, 0.0, hn).astype(BF16)

    def conv(full, cw_ref):
        rows = full.shape[0]
        prev = pltpu.roll(full, 1, 0)[HALO:HALO + tm, :]
        nxt = pltpu.roll(full, rows - 1, 0)[HALO:HALO + tm, :]
        if not whole:
            pos = lax.broadcasted_iota(jnp.int32, (tm, 1), 0) % seg
            prev = jnp.where(pos == 0, 0.0, prev)
            nxt = jnp.where(pos == seg - 1, 0.0, nxt)
        return cw_ref[0:1, :] * prev + cw_ref[1:2, :] * full[HALO:HALO + tm, :] + cw_ref[2:3, :] * nxt

    h = h_scr[...]
    def up(w_ref):
        return _dot_rows(h, w_ref[...], DOT_CHUNKS)

    if mode == "ffn":
        g = conv(up(wg_ref), cg_ref)
        u = conv(up(wu_ref), cu_ref)
        o_ref[...] = ((g / (1.0 + jnp.exp(-g))) * u).astype(o_ref.dtype)
    else:
        cv = up(wc_ref) * up(wv_ref)
        b = jnp.dot(h_scr[HALO:HALO + tm, :], wb_ref[...], preferred_element_type=F32)
        o_ref[...] = (b * conv(cv, cw_ref)).astype(o_ref.dtype)


def _up(x, mod, layer, ctx, nw, w, cw, wl, mode, tm, tn):
    seg = x.shape[1]
    if ctx:
        x = x.reshape(1, -1, D_MODEL)
    bn, s, _ = x.shape
    parts = 2 if mode == "ffn" else 3
    width = w.shape[2] // parts
    nj = width // tn
    hb = tm // HALO
    last_hb = s // HALO - 1
    in_specs = [
        pl.BlockSpec((None, tm, D_MODEL), lambda b, i, j: (b, i, 0)),
        pl.BlockSpec((None, HALO, D_MODEL), lambda b, i, j: (b, jnp.maximum(i * hb - 1, 0), 0)),
        pl.BlockSpec((None, HALO, D_MODEL), lambda b, i, j: (b, jnp.minimum((i + 1) * hb, last_hb), 0)),
        _mod_spec(layer, 3 if mode == "ffn" else 0, ctx),
        _mod_spec(layer, 4 if mode == "ffn" else 1, ctx),
        _const_spec((1, D_MODEL)),
    ]
    args = [x, x, x, mod, mod, nw]
    wspec = lambda p: pl.BlockSpec((None, D_MODEL, tn), lambda b, i, j: (wl, 0, p * nj + j))
    cspec = lambda p: pl.BlockSpec((None, 3, tn), lambda b, i, j: (wl, 0, p * nj + j))
    rows = tm + 2 * HALO
    scratch = [pltpu.VMEM((rows, D_MODEL), BF16)]
    if mode == "ffn":
        in_specs += [wspec(0), wspec(1), cspec(0), cspec(1)]
        args += [w, w, cw, cw]
    else:
        in_specs += [wspec(0), wspec(1), wspec(2), cspec(0)]
        args += [w, w, w, cw]
    out = pl.pallas_call(
        functools.partial(_up_body, mode=mode, tm=tm, seg=seg),
        grid=(bn, s // tm, nj),
        in_specs=in_specs,
        out_specs=pl.BlockSpec((None, tm, tn), lambda b, i, j: (b, i, j)),
        out_shape=jax.ShapeDtypeStruct((bn, s, width), BF16),
        scratch_shapes=scratch,
        compiler_params=_cparams(("parallel", "parallel", "arbitrary")),
        name=mode + "_up" + ("_ctx" if ctx else ""),
    )(*args)
    return out.reshape(-1, seg, width)


def _rope_tables(seq, rot_dim):
    rows = seq // GRID_W
    r = jnp.repeat(jnp.arange(rows, dtype=F32), GRID_W)
    col = jnp.tile(jnp.arange(GRID_W, dtype=F32), rows)
    quarter = rot_dim // 4
    inv = ROPE_THETA ** (-jnp.arange(quarter, dtype=F32) / quarter)
    ar = r[:, None] * inv
    ac = col[:, None] * inv
    ang = jnp.concatenate([ar, ar, ac, ac], axis=-1)
    reps = 128 // rot_dim
    cos = jnp.tile(jnp.cos(ang), (1, reps))
    sin = jnp.tile(jnp.sin(ang), (1, reps))
    first = ((jnp.arange(128) // quarter) % 2 == 0)[None, :]
    return cos, jnp.where(first, -sin, 0.0), jnp.where(first, 0.0, sin)


def _attn_weights(w_in, qn, kn, mqn, mkvn, w_uq, w_ukv):
    w_in = jnp.pad(w_in, ((0, 0), (0, W_IN_PAD - w_in.shape[1]))).astype(BF16)
    uq = w_uq.reshape(B_Q_RANK, B_HEADS, B_NOPE_DIM + B_ROPE_DIM)
    uq_rope = jnp.pad(uq[:, :, B_NOPE_DIM:], ((0, 0), (0, 0), (0, 128 - B_ROPE_DIM)))
    w_uq = jnp.concatenate([uq[:, :, :B_NOPE_DIM].reshape(B_Q_RANK, -1),
                            uq_rope.reshape(B_Q_RANK, -1)], axis=1).astype(BF16)
    ukv = w_ukv.reshape(B_KV_RANK, B_HEADS, B_NOPE_DIM + B_V_DIM)
    w_ukv = jnp.concatenate([ukv[:, :, :B_NOPE_DIM].reshape(B_KV_RANK, -1),
                             ukv[:, :, B_NOPE_DIM:].reshape(B_KV_RANK, -1)], axis=1).astype(BF16)
    return (w_in, qn.reshape(1, -1), kn.reshape(1, -1), mqn.reshape(1, -1), mkvn.reshape(1, -1),
            w_uq, w_ukv)


def kernel(x, c, ctx, c_ctx, w_ada, b_ada, norm_mix, norm_ffn, attn_w_in, attn_q_norm, attn_k_norm,
           mla_q_norm, mla_kv_norm, mla_w_uq, mla_w_ukv, attn_w_o, sc_w_in, sc_conv, sc_w_out,
           ffn_w_up, ffn_conv, ffn_w_down, final_norm):
    bn, s, _ = x.shape
    cl = ctx.shape[1]
    assert bn <= MOD_ROWS // 2
    cvec = jnp.zeros((MOD_ROWS, D_MODEL), F32).at[:bn].set(c).at[MOD_ROWS // 2].set(c_ctx)
    mod = _adaln(cvec, w_ada, b_ada)
    rope = _rope_tables(s, A_HEAD_DIM) + _rope_tables(s, B_ROPE_DIM)

    w_o_all = attn_w_o.astype(BF16)
    sc_w_in_b, sc_w_out_b = sc_w_in.astype(BF16), sc_w_out.astype(BF16)
    ffn_w_up_b, ffn_w_down_b = ffn_w_up.astype(BF16), ffn_w_down.astype(BF16)
    half = A_HEADS * A_HEAD_DIM
    xc = ctx
    for l in range(DEPTH):
        later_attn = any(j % 2 == 0 for j in range(l + 1, DEPTH))
        i = l // 2
        nw = norm_mix[l].reshape(1, -1)
        if l % 2 == 0:
            wts = _attn_weights(attn_w_in[i], attn_q_norm[i], attn_k_norm[i], mla_q_norm[i],
                                mla_kv_norm[i], mla_w_uq[i], mla_w_ukv[i])
            w_o = [(w_o_all, (half, D_MODEL), (i, 0, 0)), (w_o_all, (half, D_MODEL), (i, 1, 0))]
            qa_c, qb_c, ka_c, va_c, kb_c, vb_c = _attn_proj(xc, mod, l, True, nw, wts, None, later_attn, cl)
            qa, qb, ka, va, kb, vb = _attn_proj(x, mod, l, False, nw, wts, rope, True, 512, chunks=2)
            oa = _flash(qa, ka_c, va_c, ka, va, group=A_GROUP, dq=A_HEAD_DIM, dv=A_HEAD_DIM,
                        tq=512, tk=256, name="gqa", depth=10)
            ob = _flash(qb, kb_c, vb_c, kb, vb, group=1, dq=B_QK_PAD, dv=B_V_DIM,
                        tq=2048, tk=256, name="mla", depth=10)
            x = _proj_res(x, mod, l, 2, False, [oa, ob], w_o, 512, name="attn_out")
            if later_attn:
                oa_c = _flash(qa_c, ka_c, va_c, None, None, group=A_GROUP, dq=A_HEAD_DIM, dv=A_HEAD_DIM,
                              tq=cl, tk=0, name="gqa_ctx")
                ob_c = _flash(qb_c, kb_c, vb_c, None, None, group=1, dq=B_QK_PAD, dv=B_V_DIM,
                              tq=cl, tk=0, name="mla_ctx")
                xc = _proj_res(xc, mod, l, 2, True, [oa_c, ob_c], w_o, cl, name="attn_out")
        else:
            w_out = [(sc_w_out_b, (SC_WIDTH, D_MODEL), (i, 0, 0))]
            z = _up(x, mod, l, False, nw, sc_w_in_b, sc_conv, i, "sc", 1024, 512)
            x = _proj_res(x, mod, l, 2, False, [z], w_out, 512, name="sc_out")
            if later_attn:
                zc = _up(xc, mod, l, True, nw, sc_w_in_b, sc_conv, i, "sc", 1024, 512)
                xc = _proj_res(xc, mod, l, 2, True, [zc], w_out, cl, name="sc_out")
        nwf = norm_ffn[l].reshape(1, -1)
        w_down = [(ffn_w_down_b, (D_FF, D_MODEL), (l, 0, 0))]
        hid = _up(x, mod, l, False, nwf, ffn_w_up_b, ffn_conv, l, "ffn", 1024, 512)
        fw = final_norm.reshape(1, -1) if l == DEPTH - 1 else None
        x = _proj_res(x, mod, l, 5, False, [hid], w_down, 512, final_w=fw, name="ffn_down")
        if later_attn:
            hid_c = _up(xc, mod, l, True, nwf, ffn_w_up_b, ffn_conv, l, "ffn", 1024, 512)
            xc = _proj_res(xc, mod, l, 5, True, [hid_c], w_down, cl, name="ffn_down")
    return x
```
